```python
import math
import jax, jax.numpy as jnp
from jax import lax
import numpy as np

D_MODEL = 1024
BATCH = 16
SEQ = 2048
DEPTH = 1

D_MIX = D_MODEL
D_SSM = D_MIX // 2
D_ATTN = D_MIX // 2
SSM_GROUP_CH = 16
SSM_GROUPS = D_SSM // SSM_GROUP_CH
SSM_STATE = 64
HEAD_DIM = 64
N_HEADS = D_ATTN // HEAD_DIM
KV_HEADS = 2
Q_PER_KV = N_HEADS // KV_HEADS
WINDOW = 128
BLOCK = 128
D_PLE = 256
EPS = 1e-6

SPLIT_SIZES = (D_SSM, D_SSM, N_HEADS * HEAD_DIM, KV_HEADS * HEAD_DIM, KV_HEADS * HEAD_DIM, D_ATTN)
SPLIT_IDX = tuple(int(s) for s in np.cumsum(SPLIT_SIZES)[:-1])
D_IN = sum(SPLIT_SIZES)

kernel_name = "hymba_s5_swa_sink_alibi_layer"


def rms_norm(x, g):
    xf = x.astype(jnp.float32)
    y = xf * lax.rsqrt(jnp.mean(xf * xf, axis=-1, keepdims=True) + EPS)
    return (y * g.astype(jnp.float32)).astype(x.dtype)


def alibi_slopes(n_heads):
    return jnp.exp2(-8.0 * (jnp.arange(n_heads, dtype=jnp.float32) + 1.0) / n_heads)


def s5_branch(u, lam_re, lam_im, log_step, b_re, b_im, c_re, c_im, d, w_glu, b_glu):
    f32 = jnp.float32
    bsz, seq, _ = u.shape
    uf = u.astype(f32).reshape(bsz, seq, SSM_GROUPS, SSM_GROUP_CH)
    lam = lax.complex(lam_re.astype(f32), lam_im.astype(f32))
    step = jnp.exp(log_step.astype(f32))[:, None]
    lam_bar = jnp.exp(lam * step)
    b = lax.complex(b_re.astype(f32), b_im.astype(f32))
    b_bar = ((lam_bar - 1.0) / lam)[..., None] * b
    bu = jnp.einsum('blgp,gnp->blgn', uf.astype(jnp.complex64), b_bar)
    a = jnp.broadcast_to(lam_bar[None, None], (1, seq, SSM_GROUPS, SSM_STATE))

    def combine(left, right):
        a_l, b_l = left
        a_r, b_r = right
        return a_r * a_l, a_r * b_l + b_r

    _, states = lax.associative_scan(combine, (a, bu), axis=1)
    c = lax.complex(c_re.astype(f32), c_im.astype(f32))
    y = jnp.einsum('blgn,gpn->blgp', states, c).real \
        + d.astype(f32).reshape(SSM_GROUPS, SSM_GROUP_CH) * uf
    y = y.reshape(bsz, seq, D_SSM)
    g = jax.nn.gelu(y)
    out = g * jax.nn.sigmoid(g @ w_glu.astype(f32) + b_glu.astype(f32))
    return out.astype(u.dtype)


def swa_branch(q, k, v, sinks):
    f32 = jnp.float32
    bsz, seq = q.shape[:2]
    nb = seq // BLOCK
    qb = q.reshape(bsz, nb, BLOCK, KV_HEADS, Q_PER_KV, HEAD_DIM)

    def band(t):
        tb = t.reshape(bsz, nb, BLOCK, KV_HEADS, HEAD_DIM)
        prev = jnp.concatenate([jnp.zeros_like(tb[:, :1]), tb[:, :-1]], axis=1)
        return jnp.concatenate([prev, tb], axis=2)

    kb, vb = band(k), band(v)
    scale = 1.0 / math.sqrt(HEAD_DIM)
    scores = jnp.einsum('bnqkgd,bnskd->bnkgqs', qb, kb, preferred_element_type=f32) * scale
    q_idx = jnp.arange(BLOCK)[:, None]
    s_idx = jnp.arange(2 * BLOCK)[None, :]
    dist = q_idx + BLOCK - s_idx
    valid = (dist >= 0) & (dist < WINDOW)
    block_ok = (jnp.arange(nb)[:, None] > 0) | (jnp.arange(2 * BLOCK)[None, :] >= BLOCK)
    mask = valid[None, :, :] & block_ok[:, None, :]
    slopes = alibi_slopes(N_HEADS).reshape(KV_HEADS, Q_PER_KV)
    bias = -slopes[:, :, None, None] * dist.astype(f32)[None, None]
    scores = jnp.where(mask[None, :, None, None], scores + bias[None, None], -jnp.inf)
    sink = sinks.astype(f32).reshape(KV_HEADS, Q_PER_KV)[None, None, :, :, None, None]
    m = jnp.maximum(jnp.max(scores, axis=-1, keepdims=True), sink)
    e = jnp.exp(scores - m)
    probs = e / (jnp.sum(e, axis=-1, keepdims=True) + jnp.exp(sink - m))
    out = jnp.einsum('bnkgqs,bnskd->bnqkgd', probs.astype(v.dtype), vb)
    return out.reshape(bsz, seq, N_HEADS * HEAD_DIM)


def setup_inputs(seed: int = 0) -> dict:
    key = jax.random.key(seed)
    ks = jax.random.split(key, 24)
    f32 = jnp.float32
    nrm = lambda k, shape, s: jax.random.normal(k, shape, f32) * s
    x = jax.random.normal(ks[0], (BATCH, SEQ, D_MODEL), f32)
    p = jax.random.normal(ks[1], (DEPTH, BATCH, SEQ, D_PLE), f32)
    pre_norm_g = 1.0 + nrm(ks[2], (DEPTH, D_MODEL), 0.02)
    w_in = nrm(ks[3], (DEPTH, D_MODEL, D_IN), D_MODEL ** -0.5)
    n = jnp.arange(SSM_STATE, dtype=f32)
    ssm_lam_re = -0.5 * jnp.exp(nrm(ks[4], (DEPTH, SSM_GROUPS, SSM_STATE), 0.05))
    ssm_lam_im = jnp.pi * n[None, None, :] + nrm(ks[5], (DEPTH, SSM_GROUPS, SSM_STATE), 0.01)
    ssm_log_step = jax.random.uniform(ks[6], (DEPTH, SSM_GROUPS), f32, math.log(1e-3), math.log(1e-1))
    bs = (2.0 * SSM_GROUP_CH) ** -0.5
    ssm_b_re = nrm(ks[7], (DEPTH, SSM_GROUPS, SSM_STATE, SSM_GROUP_CH), bs)
    ssm_b_im = nrm(ks[8], (DEPTH, SSM_GROUPS, SSM_STATE, SSM_GROUP_CH), bs)
    cs = (2.0 * SSM_STATE) ** -0.5
    ssm_c_re = nrm(ks[9], (DEPTH, SSM_GROUPS, SSM_GROUP_CH, SSM_STATE), cs)
    ssm_c_im = nrm(ks[10], (DEPTH, SSM_GROUPS, SSM_GROUP_CH, SSM_STATE), cs)
    ssm_d = nrm(ks[11], (DEPTH, D_SSM), 1.0)
    ssm_w_glu = nrm(ks[12], (DEPTH, D_SSM, D_SSM), D_SSM ** -0.5)
    ssm_b_glu = nrm(ks[13], (DEPTH, D_SSM), 0.01)
    attn_sinks = nrm(ks[14], (DEPTH, N_HEADS), 1.0)
    w_out = nrm(ks[15], (DEPTH, D_MIX, D_MODEL), D_MIX ** -0.5)
    post_norm_g = 1.0 + nrm(ks[16], (DEPTH, D_MODEL), 0.02)
    pl_w_proj = nrm(ks[17], (DEPTH, D_PLE, D_MODEL), D_PLE ** -0.5)
    pl_w_gate = nrm(ks[18], (DEPTH, D_MODEL, D_MODEL), D_MODEL ** -0.5)
    pl_b_gate = nrm(ks[19], (DEPTH, D_MODEL), 0.01)
    return {"x": x, "p": p, "pre_norm_g": pre_norm_g, "w_in": w_in,
            "ssm_lam_re": ssm_lam_re, "ssm_lam_im": ssm_lam_im, "ssm_log_step": ssm_log_step,
            "ssm_b_re": ssm_b_re, "ssm_b_im": ssm_b_im, "ssm_c_re": ssm_c_re, "ssm_c_im": ssm_c_im,
            "ssm_d": ssm_d, "ssm_w_glu": ssm_w_glu, "ssm_b_glu": ssm_b_glu,
            "attn_sinks": attn_sinks, "w_out": w_out, "post_norm_g": post_norm_g,
            "pl_w_proj": pl_w_proj, "pl_w_gate": pl_w_gate, "pl_b_gate": pl_b_gate}


def reference(x, p, pre_norm_g, w_in, ssm_lam_re, ssm_lam_im, ssm_log_step, ssm_b_re, ssm_b_im,
              ssm_c_re, ssm_c_im, ssm_d, ssm_w_glu, ssm_b_glu, attn_sinks, w_out, post_norm_g,
              pl_w_proj, pl_w_gate, pl_b_gate):
    bsz, seq, _ = x.shape
    h = x
    for i in range(DEPTH):
        hn = rms_norm(h, pre_norm_g[i])
        proj = hn @ w_in[i]
        u_ssm, z_ssm, q, k, v, z_attn = jnp.split(proj, SPLIT_IDX, axis=-1)
        ssm_out = s5_branch(u_ssm, ssm_lam_re[i], ssm_lam_im[i], ssm_log_step[i],
                            ssm_b_re[i], ssm_b_im[i], ssm_c_re[i], ssm_c_im[i], ssm_d[i],
                            ssm_w_glu[i], ssm_b_glu[i]) * jax.nn.silu(z_ssm)
        attn_out = swa_branch(q.reshape(bsz, seq, N_HEADS, HEAD_DIM),
                              k.reshape(bsz, seq, KV_HEADS, HEAD_DIM),
                              v.reshape(bsz, seq, KV_HEADS, HEAD_DIM),
                              attn_sinks[i]) * jax.nn.silu(z_attn)
        mixed = jnp.concatenate([ssm_out, attn_out], axis=-1) @ w_out[i]
        h = h + rms_norm(mixed, post_norm_g[i])
        gate = jax.nn.sigmoid(h @ pl_w_gate[i] + pl_b_gate[i])
        h = h + gate * (p[i] @ pl_w_proj[i])
    return h
```

```python
import functools
import math

import numpy as np
import jax
import jax.numpy as jnp
from jax import lax
from jax.experimental import pallas as pl
from jax.experimental.pallas import tpu as pltpu

F32 = jnp.float32
BF16 = jnp.bfloat16

D_MODEL = 1024
D_SSM = 512
D_ATTN = 512
SSM_GROUP_CH = 16
SSM_GROUPS = D_SSM // SSM_GROUP_CH
SSM_STATE = 64
N_STATE = SSM_GROUPS * SSM_STATE
HEAD_DIM = 64
N_HEADS = D_ATTN // HEAD_DIM
KV_HEADS = 2
Q_PER_KV = N_HEADS // KV_HEADS
D_KV = KV_HEADS * HEAD_DIM
WINDOW = 128
D_PLE = 256
EPS = 1e-6
D_IN = 2 * D_SSM + D_ATTN + 2 * D_KV + D_ATTN

OFF_U, OFF_ZS, OFF_Q, OFF_K, OFF_V, OFF_ZA = 0, 512, 1024, 1536, 1664, 1792

LANES = 128
SLAB_CH = LANES
N_SLABS = D_SSM // SLAB_CH
SLAB_STATE = (SLAB_CH // SSM_GROUP_CH) * SSM_STATE
SCAN_COLS = 256

TOK_IN = 512
T_SCAN = 32
TOK_OUT = 256
VMEM_LIMIT = 48 * 1024 * 1024


def _silu(z):
    return z * jax.nn.sigmoid(z)


def _in_proj_kernel(x_ref, g_ref, w_ref, u_ref, zs_ref, q_ref, k_ref, v_ref, za_ref):
    x = x_ref[...]
    ms = jnp.mean(x * x, axis=-1, keepdims=True)
    hn = (x * lax.rsqrt(ms + EPS) * g_ref[...]).astype(BF16)

    def proj(lo, hi):
        return jnp.dot(hn, w_ref[:, lo:hi], preferred_element_type=F32)

    u_ref[...] = proj(OFF_U, OFF_ZS).astype(BF16)
    zs_ref[...] = _silu(proj(OFF_ZS, OFF_Q)).astype(BF16)
    q_ref[...] = (proj(OFF_Q, OFF_K) * (1.0 / math.sqrt(HEAD_DIM))).astype(BF16)
    k_ref[...] = proj(OFF_K, OFF_V).astype(BF16)
    v_ref[...] = proj(OFF_V, OFF_ZA).astype(BF16)
    za_ref[...] = _silu(proj(OFF_ZA, D_IN)).astype(BF16)


def _in_proj(x, g, w_bf16):
    bsz, seq, _ = x.shape
    nt = seq // TOK_IN
    tok = lambda width: pl.BlockSpec((None, TOK_IN, width), lambda b, i: (b, i, 0))
    full = lambda shape: pl.BlockSpec(shape, lambda b, i: (0,) * len(shape))
    out_shape = (
        jax.ShapeDtypeStruct((seq, bsz * D_SSM), BF16),
        jax.ShapeDtypeStruct((bsz, seq, D_SSM), BF16),
        jax.ShapeDtypeStruct((bsz, seq, D_ATTN), BF16),
        jax.ShapeDtypeStruct((bsz, seq, D_KV), BF16),
        jax.ShapeDtypeStruct((bsz, seq, D_KV), BF16),
        jax.ShapeDtypeStruct((bsz, seq, D_ATTN), BF16),
    )
    out_specs = (
        pl.BlockSpec((TOK_IN, D_SSM), lambda b, i: (i, b)),
        tok(D_SSM), tok(D_ATTN), tok(D_KV), tok(D_KV), tok(D_ATTN),
    )
    return pl.pallas_call(
        _in_proj_kernel,
        grid=(bsz, nt),
        in_specs=[tok(D_MODEL), full((1, D_MODEL)), full((D_MODEL, D_IN))],
        out_specs=out_specs,
        out_shape=out_shape,
        compiler_params=pltpu.CompilerParams(
            dimension_semantics=("parallel", "parallel"), vmem_limit_bytes=VMEM_LIMIT),
        name="in_proj",
    )(x, g, w_bf16)


def _ssm_kernel(u_ref, bre_ref, bim_ref, lre_ref, lim_ref, cre_ref, cim_ref, d_ref,
                wglu_ref, bglu_ref, o_ref, sre_ref, sim_ref, bu_ref, st_ref, y_ref, *, bsz):
    @pl.when(pl.program_id(0) == 0)
    def _():
        sre_ref[...] = jnp.zeros_like(sre_ref)
        sim_ref[...] = jnp.zeros_like(sim_ref)

    u = u_ref[...]
    for j in range(N_SLABS):
        uj = u[:, j * SLAB_CH:(j + 1) * SLAB_CH]
        bu_ref[:, :SLAB_STATE] = jnp.dot(uj, bre_ref[j], preferred_element_type=F32)
        bu_ref[:, SLAB_STATE:] = jnp.dot(uj, bim_ref[j], preferred_element_type=F32)
        for c in range(SLAB_STATE // SCAN_COLS):
            lo = j * SLAB_STATE + c * SCAN_COLS
            cre = slice(c * SCAN_COLS, (c + 1) * SCAN_COLS)
            cim = slice(SLAB_STATE + c * SCAN_COLS, SLAB_STATE + (c + 1) * SCAN_COLS)
            lr = jnp.broadcast_to(lre_ref[:, lo:lo + SCAN_COLS], (bsz, SCAN_COLS))
            li = jnp.broadcast_to(lim_ref[:, lo:lo + SCAN_COLS], (bsz, SCAN_COLS))

            def step(t, carry, lr=lr, li=li, cre=cre, cim=cim):
                sr, si = carry
                rows = pl.ds(pl.multiple_of(t * bsz, bsz), bsz)
                nr = lr * sr - li * si + bu_ref[rows, cre]
                ni = lr * si + li * sr + bu_ref[rows, cim]
                st_ref[rows, cre] = nr.astype(BF16)
                st_ref[rows, cim] = ni.astype(BF16)
                return nr, ni

            sr, si = lax.fori_loop(
                0, T_SCAN, step,
                (sre_ref[:, lo:lo + SCAN_COLS], sim_ref[:, lo:lo + SCAN_COLS]), unroll=4)
            sre_ref[:, lo:lo + SCAN_COLS] = sr
            sim_ref[:, lo:lo + SCAN_COLS] = si
        y_ref[:, j * SLAB_CH:(j + 1) * SLAB_CH] = (
            jnp.dot(st_ref[:, :SLAB_STATE], cre_ref[j], preferred_element_type=F32)
            + jnp.dot(st_ref[:, SLAB_STATE:], cim_ref[j], preferred_element_type=F32))

    y = y_ref[...] + d_ref[...] * u.astype(F32)
    g = jax.nn.gelu(y)
    gate = jnp.dot(g.astype(BF16), wglu_ref[...], preferred_element_type=F32) + bglu_ref[...]
    o_ref[...] = (g * jax.nn.sigmoid(gate)).astype(BF16)


def _ssm(u_tb, bre, bim, lre, lim, cre, cim_neg, d, wglu, bglu, *, bsz, seq):
    rows = T_SCAN * bsz
    full = lambda shape: pl.BlockSpec(shape, lambda i: (0,) * len(shape))
    return pl.pallas_call(
        functools.partial(_ssm_kernel, bsz=bsz),
        grid=(seq // T_SCAN,),
        in_specs=[
            pl.BlockSpec((rows, D_SSM), lambda i: (i, 0)),
            full((N_SLABS, SLAB_CH, SLAB_STATE)), full((N_SLABS, SLAB_CH, SLAB_STATE)),
            full((1, N_STATE)), full((1, N_STATE)),
            full((N_SLABS, SLAB_STATE, SLAB_CH)), full((N_SLABS, SLAB_STATE, SLAB_CH)),
            full((1, D_SSM)), full((D_SSM, D_SSM)), full((1, D_SSM)),
        ],
        out_specs=pl.BlockSpec((rows, D_SSM), lambda i: (i, 0)),
        out_shape=jax.ShapeDtypeStruct((seq * bsz, D_SSM), BF16),
        scratch_shapes=[
            pltpu.VMEM((bsz, N_STATE), F32),
            pltpu.VMEM((bsz, N_STATE), F32),
            pltpu.VMEM((rows, 2 * SLAB_STATE), F32),
            pltpu.VMEM((rows, 2 * SLAB_STATE), BF16),
            pltpu.VMEM((rows, D_SSM), F32),
        ],
        compiler_params=pltpu.CompilerParams(
            dimension_semantics=("arbitrary",), vmem_limit_bytes=VMEM_LIMIT),
        name="s5_scan",
    )(u_tb, bre, bim, lre, lim, cre, cim_neg, d, wglu, bglu)


def _attn_out_kernel(sink_ref, x_ref, p_ref, ssm_ref, zs_ref, q_ref, k_ref, v_ref, kp_ref, vp_ref,
                     za_ref, bias_ref, wout_ref, pg_ref, wgate_ref, bgate_ref, wproj_ref,
                     o_ref, attn_ref):
    first = (pl.program_id(1) == 0).astype(jnp.int32)
    for jj in range(TOK_OUT // WINDOW):
        rows = slice(jj * WINDOW, (jj + 1) * WINDOW)
        heads = []
        for g in range(KV_HEADS):
            cols = slice(g * HEAD_DIM, (g + 1) * HEAD_DIM)
            if jj == 0:
                kp, vp = kp_ref[:, cols], vp_ref[:, cols]
            else:
                prev = slice((jj - 1) * WINDOW, jj * WINDOW)
                kp, vp = k_ref[prev, cols], v_ref[prev, cols]
            kk = jnp.concatenate([kp, k_ref[rows, cols]], axis=0)
            vv = jnp.concatenate([vp, v_ref[rows, cols]], axis=0)
            for hh in range(Q_PER_KV):
                h = g * Q_PER_KV + hh
                qh = q_ref[rows, h * HEAD_DIM:(h + 1) * HEAD_DIM]
                s = lax.dot_general(qh, kk, (((1,), (1,)), ((), ())),
                                    preferred_element_type=F32)
                s = s + (bias_ref[h + N_HEADS * first] if jj == 0 else bias_ref[h])
                sink = sink_ref[h]
                m = jnp.maximum(jnp.max(s, axis=-1, keepdims=True), sink)
                e = jnp.exp(s - m)
                den = jnp.sum(e, axis=-1, keepdims=True) + jnp.exp(sink - m)
                o = jnp.dot(e.astype(BF16), vv, preferred_element_type=F32)
                heads.append(o / den)
        attn_ref[rows, :] = jnp.concatenate(heads, axis=1)

    a = (attn_ref[...] * za_ref[...].astype(F32)).astype(BF16)
    s5 = (ssm_ref[...].astype(F32) * zs_ref[...].astype(F32)).astype(BF16)
    mixed = (jnp.dot(s5, wout_ref[:D_SSM, :], preferred_element_type=F32)
             + jnp.dot(a, wout_ref[D_SSM:, :], preferred_element_type=F32))
    ms = jnp.mean(mixed * mixed, axis=-1, keepdims=True)
    h1 = x_ref[...] + mixed * lax.rsqrt(ms + EPS) * pg_ref[...]
    gate = jax.nn.sigmoid(
        jnp.dot(h1.astype(BF16), wgate_ref[...], preferred_element_type=F32) + bgate_ref[...])
    ple = jnp.dot(p_ref[...].astype(BF16), wproj_ref[...], preferred_element_type=F32)
    o_ref[...] = h1 + gate * ple


def _attn_bias_table():
    q_idx = np.arange(WINDOW)[:, None]
    s_idx = np.arange(2 * WINDOW)[None, :]
    dist = q_idx + WINDOW - s_idx
    valid = (dist >= 0) & (dist < WINDOW)
    slopes = np.exp2(-8.0 * (np.arange(N_HEADS, dtype=np.float32) + 1.0) / N_HEADS).astype(np.float32)
    bias = -slopes[:, None, None] * dist.astype(np.float32)[None]
    normal = np.where(valid[None], bias, -np.inf)
    first = np.where((valid & (s_idx >= WINDOW))[None], bias, -np.inf)
    return np.concatenate([normal, first], axis=0).astype(np.float32)


def _attn_out(sinks, x, p, ssm2d, zs, q, k, v, za, wout, pg, wgate, bgate, wproj):
    bsz, seq, _ = x.shape
    nt = seq // TOK_OUT
    sub = TOK_OUT // WINDOW
    tok = lambda width: pl.BlockSpec((None, TOK_OUT, width), lambda b, i: (b, i, 0))
    prev = pl.BlockSpec((None, WINDOW, D_KV), lambda b, i: (b, jnp.maximum(i * sub - 1, 0), 0))
    full = lambda shape: pl.BlockSpec(shape, lambda b, i: (0,) * len(shape))
    bias = jnp.asarray(_attn_bias_table())
    return pl.pallas_call(
        _attn_out_kernel,
        grid=(bsz, nt),
        in_specs=[
            pl.BlockSpec(memory_space=pltpu.SMEM),
            tok(D_MODEL), tok(D_PLE),
            pl.BlockSpec((TOK_OUT, D_SSM), lambda b, i: (i, b)),
            tok(D_SSM), tok(D_ATTN), tok(D_KV), tok(D_KV), prev, prev, tok(D_ATTN),
            full(bias.shape),
            full((D_MODEL, D_MODEL)), full((1, D_MODEL)),
            full((D_MODEL, D_MODEL)), full((1, D_MODEL)), full((D_PLE, D_MODEL)),
        ],
        out_specs=tok(D_MODEL),
        out_shape=jax.ShapeDtypeStruct((bsz, seq, D_MODEL), F32),
        scratch_shapes=[pltpu.VMEM((TOK_OUT, D_ATTN), F32)],
        compiler_params=pltpu.CompilerParams(
            dimension_semantics=("parallel", "parallel"), vmem_limit_bytes=VMEM_LIMIT),
        name="attn_out",
    )(sinks, x, p, ssm2d, zs, q, k, v, k, v, za, bias, wout, pg, wgate, bgate, wproj)


def _block_diag(blocks):
    s, n, a, b = blocks.shape
    eye = jnp.eye(n, dtype=blocks.dtype)
    return jnp.einsum('snab,nm->snamb', blocks, eye).reshape(s, n * a, n * b)


def _ssm_params(lam_re, lam_im, log_step, b_re, b_im, c_re, c_im):
    lam = lax.complex(lam_re.astype(F32), lam_im.astype(F32))
    step = jnp.exp(log_step.astype(F32))[:, None]
    lam_bar = jnp.exp(lam * step)
    b = lax.complex(b_re.astype(F32), b_im.astype(F32))
    b_bar = ((lam_bar - 1.0) / lam)[..., None] * b
    gps = SLAB_CH // SSM_GROUP_CH

    def in_w(part):
        blocks = part.reshape(N_SLABS, gps, SSM_STATE, SSM_GROUP_CH).transpose(0, 1, 3, 2)
        return _block_diag(blocks).astype(BF16)

    def out_w(part):
        blocks = part.reshape(N_SLABS, gps, SSM_GROUP_CH, SSM_STATE).transpose(0, 1, 3, 2)
        return _block_diag(blocks).astype(BF16)

    return (in_w(jnp.real(b_bar)), in_w(jnp.imag(b_bar)),
            jnp.real(lam_bar).reshape(1, N_STATE), jnp.imag(lam_bar).reshape(1, N_STATE),
            out_w(c_re.astype(F32)), out_w(-c_im.astype(F32)))


def _layer(h, p, pre_norm_g, w_in, ssm_lam_re, ssm_lam_im, ssm_log_step, ssm_b_re, ssm_b_im,
           ssm_c_re, ssm_c_im, ssm_d, ssm_w_glu, ssm_b_glu, attn_sinks, w_out, post_norm_g,
           pl_w_proj, pl_w_gate, pl_b_gate):
    bsz, seq, _ = h.shape
    row = lambda a: a.astype(F32).reshape(1, -1)
    u_tm, zs, q, k, v, za = _in_proj(h, row(pre_norm_g), w_in.astype(BF16))
    bre, bim, lre, lim, cre, cim_neg = _ssm_params(
        ssm_lam_re, ssm_lam_im, ssm_log_step, ssm_b_re, ssm_b_im, ssm_c_re, ssm_c_im)
    ssm_tb = _ssm(u_tm.reshape(seq * bsz, D_SSM), bre, bim, lre, lim, cre, cim_neg,
                  row(ssm_d), ssm_w_glu.astype(BF16), row(ssm_b_glu), bsz=bsz, seq=seq)
    return _attn_out(attn_sinks.astype(F32), h, p, ssm_tb.reshape(seq, bsz * D_SSM), zs, q, k, v, za,
                     w_out.astype(BF16), row(post_norm_g), pl_w_gate.astype(BF16), row(pl_b_gate),
                     pl_w_proj.astype(BF16))


def kernel(x, p, pre_norm_g, w_in, ssm_lam_re, ssm_lam_im, ssm_log_step, ssm_b_re, ssm_b_im,
           ssm_c_re, ssm_c_im, ssm_d, ssm_w_glu, ssm_b_glu, attn_sinks, w_out, post_norm_g,
           pl_w_proj, pl_w_gate, pl_b_gate):
    h = x
    for i in range(p.shape[0]):
        h = _layer(h, p[i], pre_norm_g[i], w_in[i], ssm_lam_re[i], ssm_lam_im[i], ssm_log_step[i],
                   ssm_b_re[i], ssm_b_im[i], ssm_c_re[i], ssm_c_im[i], ssm_d[i], ssm_w_glu[i],
                   ssm_b_glu[i], attn_sinks[i], w_out[i], post_norm_g[i], pl_w_proj[i],
                   pl_w_gate[i], pl_b_gate[i])
    return h
```

```python
import functools
import math

import numpy as np
import jax
import jax.numpy as jnp
from jax import lax
from jax.experimental import pallas as pl
from jax.experimental.pallas import tpu as pltpu

F32 = jnp.float32
BF16 = jnp.bfloat16

D_MODEL = 1024
D_SSM = 512
D_ATTN = 512
SSM_GROUP_CH = 16
SSM_GROUPS = D_SSM // SSM_GROUP_CH
SSM_STATE = 64
N_STATE = SSM_GROUPS * SSM_STATE
HEAD_DIM = 64
N_HEADS = D_ATTN // HEAD_DIM
KV_HEADS = 2
Q_PER_KV = N_HEADS // KV_HEADS
D_KV = KV_HEADS * HEAD_DIM
WINDOW = 128
D_PLE = 256
EPS = 1e-6

LANES = 128
assert D_KV == LANES and 2 * HEAD_DIM == LANES

OFF_U, OFF_ZS, OFF_Q, OFF_K, OFF_V, OFF_ZA, D_IN_WIDE = 0, 512, 1024, 1536, 1792, 2048, 2560

SLAB_CH = LANES
N_SLABS = D_SSM // SLAB_CH
SLAB_STATE = (SLAB_CH // SSM_GROUP_CH) * SSM_STATE
SCAN_COLS = 256

TOK_IN = 512
T_SCAN = 32
TOK_OUT = 256
VMEM_LIMIT = 48 * 1024 * 1024


def _silu(z):
    return z * jax.nn.sigmoid(z)


def _in_proj_kernel(x_ref, g_ref, w_ref, u_ref, zs_ref, q_ref, k_ref, v_ref, za_ref):
    x = x_ref[...]
    ms = jnp.mean(x * x, axis=-1, keepdims=True)
    hn = (x * lax.rsqrt(ms + EPS) * g_ref[...]).astype(BF16)

    def proj(lo, hi):
        return jnp.dot(hn, w_ref[:, lo:hi], preferred_element_type=F32)

    u_ref[...] = proj(OFF_U, OFF_ZS).astype(BF16)
    zs_ref[...] = _silu(proj(OFF_ZS, OFF_Q)).astype(BF16)
    q_ref[...] = (proj(OFF_Q, OFF_K) * (1.0 / math.sqrt(HEAD_DIM))).astype(BF16)
    k_ref[...] = proj(OFF_K, OFF_V).astype(BF16)
    v_ref[...] = proj(OFF_V, OFF_ZA).astype(BF16)
    za_ref[...] = _silu(proj(OFF_ZA, D_IN_WIDE)).astype(BF16)


def _in_proj(x, g, w_wide):
    bsz, seq, _ = x.shape
    tok = lambda width: pl.BlockSpec((None, TOK_IN, width), lambda b, i: (b, i, 0))
    full = lambda shape: pl.BlockSpec(shape, lambda b, i: (0,) * len(shape))
    widths = (D_SSM, D_SSM, D_ATTN, 2 * D_KV, 2 * D_KV, D_ATTN)
    return pl.pallas_call(
        _in_proj_kernel,
        grid=(bsz, seq // TOK_IN),
        in_specs=[tok(D_MODEL), full((1, D_MODEL)), full((D_MODEL, D_IN_WIDE))],
        out_specs=tuple(tok(w) for w in widths),
        out_shape=tuple(jax.ShapeDtypeStruct((bsz, seq, w), BF16) for w in widths),
        compiler_params=pltpu.CompilerParams(
            dimension_semantics=("parallel", "parallel"), vmem_limit_bytes=VMEM_LIMIT),
        name="in_proj",
    )(x, g, w_wide)


def _ssm_kernel(u_ref, perm_ref, permt_ref, bre_ref, bim_ref, lre_ref, lim_ref, cre_ref, cim_ref,
                d_ref, wglu_ref, bglu_ref, o_ref, sre_ref, sim_ref, bu_ref, st_ref, *, bsz):
    rows = bsz * T_SCAN

    @pl.when(pl.program_id(0) == 0)
    def _():
        sre_ref[...] = jnp.zeros_like(sre_ref)
        sim_ref[...] = jnp.zeros_like(sim_ref)

    u_bt = u_ref[...].reshape(rows, D_SSM)
    u = jnp.dot(perm_ref[...], u_bt, preferred_element_type=F32).astype(BF16)

    ys = []
    for j in range(N_SLABS):
        uj = u[:, j * SLAB_CH:(j + 1) * SLAB_CH]
        bu_ref[j, :, :SLAB_STATE] = jnp.dot(uj, bre_ref[j], preferred_element_type=F32)
        bu_ref[j, :, SLAB_STATE:] = jnp.dot(uj, bim_ref[j], preferred_element_type=F32)
        for c in range(SLAB_STATE // SCAN_COLS):
            lo = j * SLAB_STATE + c * SCAN_COLS
            cre = slice(c * SCAN_COLS, (c + 1) * SCAN_COLS)
            cim = slice(SLAB_STATE + c * SCAN_COLS, SLAB_STATE + (c + 1) * SCAN_COLS)
            lr = jnp.broadcast_to(lre_ref[:, lo:lo + SCAN_COLS], (bsz, SCAN_COLS))
            li = jnp.broadcast_to(lim_ref[:, lo:lo + SCAN_COLS], (bsz, SCAN_COLS))
            sr = sre_ref[:, lo:lo + SCAN_COLS]
            si = sim_ref[:, lo:lo + SCAN_COLS]
            for t in range(T_SCAN):
                r = slice(t * bsz, (t + 1) * bsz)
                sr, si = (lr * sr - li * si + bu_ref[j, r, cre],
                          lr * si + li * sr + bu_ref[j, r, cim])
                st_ref[j, r, cre] = sr.astype(BF16)
                st_ref[j, r, cim] = si.astype(BF16)
            sre_ref[:, lo:lo + SCAN_COLS] = sr
            sim_ref[:, lo:lo + SCAN_COLS] = si
        ys.append(jnp.dot(st_ref[j, :, :SLAB_STATE], cre_ref[j], preferred_element_type=F32)
                  + jnp.dot(st_ref[j, :, SLAB_STATE:], cim_ref[j], preferred_element_type=F32))

    y = jnp.concatenate(ys, axis=1) + d_ref[...] * u.astype(F32)
    g = jax.nn.gelu(y)
    gate = jnp.dot(g.astype(BF16), wglu_ref[...], preferred_element_type=F32) + bglu_ref[...]
    out_tb = (g * jax.nn.sigmoid(gate)).astype(BF16)
    out_bt = jnp.dot(permt_ref[...], out_tb, preferred_element_type=F32).astype(BF16)
    o_ref[...] = out_bt.reshape(bsz, T_SCAN, D_SSM)


def _row_permutation(bsz):
    rows = bsz * T_SCAN
    perm = np.zeros((rows, rows), np.float32)
    t, b = np.meshgrid(np.arange(T_SCAN), np.arange(bsz), indexing="ij")
    perm[(t * bsz + b).ravel(), (b * T_SCAN + t).ravel()] = 1.0
    return perm


def _ssm(u, bre, bim, lre, lim, cre, cim_neg, d, wglu, bglu):
    bsz, seq, _ = u.shape
    rows = T_SCAN * bsz
    perm = _row_permutation(bsz)
    full = lambda shape: pl.BlockSpec(shape, lambda i: (0,) * len(shape))
    tok = pl.BlockSpec((bsz, T_SCAN, D_SSM), lambda i: (0, i, 0))
    return pl.pallas_call(
        functools.partial(_ssm_kernel, bsz=bsz),
        grid=(seq // T_SCAN,),
        in_specs=[
            tok, full((rows, rows)), full((rows, rows)),
            full((N_SLABS, SLAB_CH, SLAB_STATE)), full((N_SLABS, SLAB_CH, SLAB_STATE)),
            full((1, N_STATE)), full((1, N_STATE)),
            full((N_SLABS, SLAB_STATE, SLAB_CH)), full((N_SLABS, SLAB_STATE, SLAB_CH)),
            full((1, D_SSM)), full((D_SSM, D_SSM)), full((1, D_SSM)),
        ],
        out_specs=tok,
        out_shape=jax.ShapeDtypeStruct((bsz, seq, D_SSM), BF16),
        scratch_shapes=[
            pltpu.VMEM((bsz, N_STATE), F32),
            pltpu.VMEM((bsz, N_STATE), F32),
            pltpu.VMEM((N_SLABS, rows, 2 * SLAB_STATE), F32),
            pltpu.VMEM((N_SLABS, rows, 2 * SLAB_STATE), BF16),
        ],
        compiler_params=pltpu.CompilerParams(
            dimension_semantics=("arbitrary",), vmem_limit_bytes=VMEM_LIMIT),
        name="s5_scan",
    )(u, jnp.asarray(perm, BF16), jnp.asarray(perm.T, BF16),
      bre, bim, lre, lim, cre, cim_neg, d, wglu, bglu)


def _attn_out_kernel(sink_ref, x_ref, p_ref, ssm_ref, zs_ref, q_ref, k_ref, v_ref, kp_ref, vp_ref,
                     za_ref, bias_ref, half_ref, wout_ref, pg_ref, wgate_ref, bgate_ref, wproj_ref,
                     o_ref):
    first = (pl.program_id(1) == 0).astype(jnp.int32)
    keep_even, keep_odd = half_ref[0], half_ref[1]
    low_q = lax.broadcasted_iota(jnp.int32, (WINDOW, LANES), 1) < HEAD_DIM

    blocks = []
    for jj in range(TOK_OUT // WINDOW):
        rows = slice(jj * WINDOW, (jj + 1) * WINDOW)
        if jj == 0:
            kp, vp = kp_ref[...], vp_ref[...]
        else:
            prev = slice((jj - 1) * WINDOW, jj * WINDOW)
            kp, vp = k_ref[prev, :], v_ref[prev, :]
        kk = jnp.concatenate([kp, k_ref[rows, :]], axis=0)
        vv = jnp.concatenate([vp, v_ref[rows, :]], axis=0)
        pairs = []
        for g in range(KV_HEADS):
            at_low = slice(g * D_KV, (g + 1) * D_KV)
            at_high = slice((1 - g) * D_KV, (2 - g) * D_KV)
            kcat = jnp.concatenate([kk[:, at_low] * keep_even,
                                    kk[:, at_high] * keep_odd], axis=0)
            vcat = jnp.concatenate(
                [jnp.concatenate([vv[:, at_low] * keep_even, keep_even], axis=1),
                 jnp.concatenate([vv[:, at_high] * keep_odd, keep_odd], axis=1)],
                axis=0)
            for pp in range(Q_PER_KV // 2):
                pair = g * (Q_PER_KV // 2) + pp
                q2 = q_ref[rows, pair * LANES:(pair + 1) * LANES]
                s = lax.dot_general(q2, kcat, (((1,), (1,)), ((), ())),
                                    preferred_element_type=F32)
                s = s + (bias_ref[pair + (N_HEADS // 2) * first] if jj == 0 else bias_ref[pair])
                sink_e, sink_o = sink_ref[2 * pair], sink_ref[2 * pair + 1]
                m_e = jnp.maximum(jnp.max(s[:, :2 * WINDOW], axis=-1, keepdims=True), sink_e)
                m_o = jnp.maximum(jnp.max(s[:, 2 * WINDOW:], axis=-1, keepdims=True), sink_o)
                e = jnp.concatenate([jnp.exp(s[:, :2 * WINDOW] - m_e),
                                     jnp.exp(s[:, 2 * WINDOW:] - m_o)], axis=1).astype(BF16)
                o = jnp.dot(e, vcat, preferred_element_type=F32)
                den = o[:, LANES:] + jnp.where(low_q, jnp.exp(sink_e - m_e), jnp.exp(sink_o - m_o))
                pairs.append(o[:, :LANES] / den)
        blocks.append(jnp.concatenate(pairs, axis=1))
    attn = jnp.concatenate(blocks, axis=0)

    a = (attn * za_ref[...].astype(F32)).astype(BF16)
    s5 = (ssm_ref[...].astype(F32) * zs_ref[...].astype(F32)).astype(BF16)
    mixed = (jnp.dot(s5, wout_ref[:D_SSM, :], preferred_element_type=F32)
             + jnp.dot(a, wout_ref[D_SSM:, :], preferred_element_type=F32))
    ms = jnp.mean(mixed * mixed, axis=-1, keepdims=True)
    h1 = x_ref[...] + mixed * lax.rsqrt(ms + EPS) * pg_ref[...]
    gate = jax.nn.sigmoid(
        jnp.dot(h1.astype(BF16), wgate_ref[...], preferred_element_type=F32) + bgate_ref[...])
    ple = jnp.dot(p_ref[...].astype(BF16), wproj_ref[...], preferred_element_type=F32)
    o_ref[...] = h1 + gate * ple


def _attn_bias_table():
    q_idx = np.arange(WINDOW)[:, None]
    s_idx = np.arange(2 * WINDOW)[None, :]
    dist = q_idx + WINDOW - s_idx
    valid = (dist >= 0) & (dist < WINDOW)
    slopes = np.exp2(-8.0 * (np.arange(N_HEADS, dtype=np.float32) + 1.0) / N_HEADS).astype(np.float32)
    bias = -slopes[:, None, None] * dist.astype(np.float32)[None]
    normal = np.where(valid[None], bias, -np.inf)
    first = np.where((valid & (s_idx >= WINDOW))[None], bias, -np.inf)
    paired = lambda t: t.reshape(N_HEADS // 2, 2, WINDOW, 2 * WINDOW).transpose(0, 2, 1, 3).reshape(
        N_HEADS // 2, WINDOW, 4 * WINDOW)
    return np.concatenate([paired(normal), paired(first)], axis=0).astype(np.float32)


def _attn_out(sinks, x, p, ssm, zs, q, k, v, za, wout, pg, wgate, bgate, wproj):
    bsz, seq, _ = x.shape
    sub = TOK_OUT // WINDOW
    tok = lambda width: pl.BlockSpec((None, TOK_OUT, width), lambda b, i: (b, i, 0))
    prev = pl.BlockSpec((None, WINDOW, 2 * D_KV), lambda b, i: (b, jnp.maximum(i * sub - 1, 0), 0))
    full = lambda shape: pl.BlockSpec(shape, lambda b, i: (0,) * len(shape))
    bias = jnp.asarray(_attn_bias_table())
    lane_is_even = np.broadcast_to(np.arange(LANES) < HEAD_DIM, (2 * WINDOW, LANES))
    half = jnp.asarray(np.stack([lane_is_even, ~lane_is_even]), BF16)
    return pl.pallas_call(
        _attn_out_kernel,
        grid=(bsz, seq // TOK_OUT),
        in_specs=[
            pl.BlockSpec(memory_space=pltpu.SMEM),
            tok(D_MODEL), tok(D_PLE), tok(D_SSM), tok(D_SSM), tok(D_ATTN),
            tok(2 * D_KV), tok(2 * D_KV), prev, prev, tok(D_ATTN),
            full(bias.shape), full(half.shape),
            full((D_MODEL, D_MODEL)), full((1, D_MODEL)),
            full((D_MODEL, D_MODEL)), full((1, D_MODEL)), full((D_PLE, D_MODEL)),
        ],
        out_specs=tok(D_MODEL),
        out_shape=jax.ShapeDtypeStruct((bsz, seq, D_MODEL), F32),
        compiler_params=pltpu.CompilerParams(
            dimension_semantics=("parallel", "parallel"), vmem_limit_bytes=VMEM_LIMIT),
        name="attn_out",
    )(sinks, x, p, ssm, zs, q, k, v, k, v, za, bias, half, wout, pg, wgate, bgate, wproj)


def _block_diag(blocks):
    s, n, a, b = blocks.shape
    eye = jnp.eye(n, dtype=blocks.dtype)
    return jnp.einsum('snab,nm->snamb', blocks, eye).reshape(s, n * a, n * b)


def _ssm_params(lam_re, lam_im, log_step, b_re, b_im, c_re, c_im):
    lr, li = lam_re.astype(F32), lam_im.astype(F32)
    step = jnp.exp(log_step.astype(F32))[:, None]
    mag = jnp.exp(lr * step)
    lbr, lbi = mag * jnp.cos(li * step), mag * jnp.sin(li * step)
    den = lr * lr + li * li
    fr = ((lbr - 1.0) * lr + lbi * li) / den
    fi = (lbi * lr - (lbr - 1.0) * li) / den
    br, bi = b_re.astype(F32), b_im.astype(F32)
    bbr = fr[..., None] * br - fi[..., None] * bi
    bbi = fr[..., None] * bi + fi[..., None] * br
    gps = SLAB_CH // SSM_GROUP_CH

    def in_w(part):
        blocks = part.reshape(N_SLABS, gps, SSM_STATE, SSM_GROUP_CH).transpose(0, 1, 3, 2)
        return _block_diag(blocks).astype(BF16)

    def out_w(part):
        blocks = part.reshape(N_SLABS, gps, SSM_GROUP_CH, SSM_STATE).transpose(0, 1, 3, 2)
        return _block_diag(blocks).astype(BF16)

    return (in_w(bbr), in_w(bbi), lbr.reshape(1, N_STATE), lbi.reshape(1, N_STATE),
            out_w(c_re.astype(F32)), out_w(-c_im.astype(F32)))


def _widen_w_in(w_in):
    w = w_in.astype(BF16)
    off_k = 2 * D_SSM + D_ATTN
    swap = lambda c: jnp.concatenate([c[:, HEAD_DIM:], c[:, :HEAD_DIM]], axis=1)
    wk, wv = w[:, off_k:off_k + D_KV], w[:, off_k + D_KV:off_k + 2 * D_KV]
    return jnp.concatenate([w[:, :off_k], wk, swap(wk), wv, swap(wv), w[:, off_k + 2 * D_KV:]], axis=1)


def _layer(h, p, pre_norm_g, w_in, ssm_lam_re, ssm_lam_im, ssm_log_step, ssm_b_re, ssm_b_im,
           ssm_c_re, ssm_c_im, ssm_d, ssm_w_glu, ssm_b_glu, attn_sinks, w_out, post_norm_g,
           pl_w_proj, pl_w_gate, pl_b_gate):
    row = lambda a: a.astype(F32).reshape(1, -1)
    u, zs, q, k, v, za = _in_proj(h, row(pre_norm_g), _widen_w_in(w_in))
    bre, bim, lre, lim, cre, cim_neg = _ssm_params(
        ssm_lam_re, ssm_lam_im, ssm_log_step, ssm_b_re, ssm_b_im, ssm_c_re, ssm_c_im)
    ssm = _ssm(u, bre, bim, lre, lim, cre, cim_neg,
               row(ssm_d), ssm_w_glu.astype(BF16), row(ssm_b_glu))
    return _attn_out(attn_sinks.astype(F32), h, p, ssm, zs, q, k, v, za,
                     w_out.astype(BF16), row(post_norm_g), pl_w_gate.astype(BF16), row(pl_b_gate),
                     pl_w_proj.astype(BF16))


def kernel(x, p, pre_norm_g, w_in, ssm_lam_re, ssm_lam_im, ssm_log_step, ssm_b_re, ssm_b_im,
           ssm_c_re, ssm_c_im, ssm_d, ssm_w_glu, ssm_b_glu, attn_sinks, w_out, post_norm_g,
           pl_w_proj, pl_w_gate, pl_b_gate):
    h = x
    for i in range(p.shape[0]):
        h = _layer(h, p[i], pre_norm_g[i], w_in[i], ssm_lam_re[i], ssm_lam_im[i], ssm_log_step[i],
                   ssm_b_re[i], ssm_b_im[i], ssm_c_re[i], ssm_c_im[i], ssm_d[i], ssm_w_glu[i],
                   ssm_b_glu[i], attn_sinks[i], w_out[i], post_norm_g[i], pl_w_proj[i],
                   pl_w_gate[i], pl_b_gate[i])
    return h
```

```python
import functools
import math

import numpy as np
import jax
import jax.numpy as jnp
from jax import lax
from jax.experimental import pallas as pl
from jax.experimental.pallas import tpu as pltpu

F32 = jnp.float32
BF16 = jnp.bfloat16

D_MODEL = 1024
D_SSM = 512
D_ATTN = 512
SSM_GROUP_CH = 16
SSM_GROUPS = D_SSM // SSM_GROUP_CH
SSM_STATE = 64
N_STATE = SSM_GROUPS * SSM_STATE
HEAD_DIM = 64
N_HEADS = D_ATTN // HEAD_DIM
KV_HEADS = 2
Q_PER_KV = N_HEADS // KV_HEADS
D_KV = KV_HEADS * HEAD_DIM
WINDOW = 128
D_PLE = 256
EPS = 1e-6

LANES = 128
assert D_KV == LANES and 2 * HEAD_DIM == LANES

OFF_U, OFF_ZS, OFF_Q, OFF_K, OFF_V, OFF_ZA, D_IN_WIDE = 0, 512, 1024, 1536, 1792, 2048, 2560

SLAB_CH = LANES
N_SLABS = D_SSM // SLAB_CH
SLAB_STATE = (SLAB_CH // SSM_GROUP_CH) * SSM_STATE
SCAN_COLS = 256

TOK_IN = 512
T_SCAN = 32
T_GROUP = 8
TOK_OUT = 256
PROJ_COLS = 256
VMEM_LIMIT = 48 * 1024 * 1024


def _silu(z):
    return z * jax.nn.sigmoid(z)


def _by_parity(step, body, even_bufs, odd_bufs):
    @pl.when(step % 2 == 0)
    def _():
        body(even_bufs, odd_bufs)

    @pl.when(step % 2 == 1)
    def _():
        body(odd_bufs, even_bufs)


def _in_proj_kernel(x_ref, g_ref, w_ref, u_ref, zs_ref, q_ref, k_ref, v_ref, za_ref):
    x = x_ref[...]
    ms = jnp.mean(x * x, axis=-1, keepdims=True)
    hn = (x * lax.rsqrt(ms + EPS) * g_ref[...]).astype(BF16)

    def proj(lo, hi):
        return jnp.dot(hn, w_ref[:, lo:hi], preferred_element_type=F32)

    u_ref[...] = proj(OFF_U, OFF_ZS).astype(BF16)
    zs_ref[...] = _silu(proj(OFF_ZS, OFF_Q)).astype(BF16)
    q_ref[...] = (proj(OFF_Q, OFF_K) * (1.0 / math.sqrt(HEAD_DIM))).astype(BF16)
    k_ref[...] = proj(OFF_K, OFF_V).astype(BF16)
    v_ref[...] = proj(OFF_V, OFF_ZA).astype(BF16)
    za_ref[...] = _silu(proj(OFF_ZA, D_IN_WIDE)).astype(BF16)


def _in_proj(x, g, w_wide):
    bsz, seq, _ = x.shape
    tok = lambda width: pl.BlockSpec((None, TOK_IN, width), lambda b, i: (b, i, 0))
    full = lambda shape: pl.BlockSpec(shape, lambda b, i: (0,) * len(shape))
    widths = (D_SSM, D_SSM, D_ATTN, 2 * D_KV, 2 * D_KV, D_ATTN)
    return pl.pallas_call(
        _in_proj_kernel,
        grid=(bsz, seq // TOK_IN),
        in_specs=[tok(D_MODEL), full((1, D_MODEL)), full((D_MODEL, D_IN_WIDE))],
        out_specs=tuple(tok(w) for w in widths),
        out_shape=tuple(jax.ShapeDtypeStruct((bsz, seq, w), BF16) for w in widths),
        compiler_params=pltpu.CompilerParams(
            dimension_semantics=("parallel", "parallel"), vmem_limit_bytes=VMEM_LIMIT),
        name="in_proj",
    )(x, g, w_wide)


def _ssm_kernel(u_ref, perm_ref, permt_ref, bre_ref, bim_ref, lre_ref, lim_ref, cre_ref, cim_ref,
                d_ref, wglu_ref, bglu_ref, o_ref, sre_ref, sim_ref, st_ref, y_ref,
                bu0_ref, ut0_ref, bu1_ref, ut1_ref, *, bsz):
    rows = bsz * T_SCAN
    step = pl.program_id(0)

    @pl.when(step == 0)
    def _():
        sre_ref[...] = jnp.zeros_like(sre_ref)
        sim_ref[...] = jnp.zeros_like(sim_ref)
        bu1_ref[...] = jnp.zeros_like(bu1_ref)
        ut1_ref[...] = jnp.zeros_like(ut1_ref)

    def body(write, read):
        bu_w, ut_w = write
        bu_r, ut_r = read

        u_bt = u_ref[...].reshape(rows, D_SSM)
        u_new = jnp.dot(perm_ref[...], u_bt, preferred_element_type=F32).astype(BF16)
        ut_w[...] = u_new
        for j in range(N_SLABS):
            ch = slice(j * SLAB_CH, (j + 1) * SLAB_CH)
            for c in range(SLAB_STATE // SCAN_COLS):
                lo = j * SLAB_STATE + c * SCAN_COLS
                cre = slice(c * SCAN_COLS, (c + 1) * SCAN_COLS)
                cim = slice(SLAB_STATE + c * SCAN_COLS, SLAB_STATE + (c + 1) * SCAN_COLS)
                lr = jnp.broadcast_to(lre_ref[:, lo:lo + SCAN_COLS], (bsz, SCAN_COLS))
                li = jnp.broadcast_to(lim_ref[:, lo:lo + SCAN_COLS], (bsz, SCAN_COLS))
                sr = sre_ref[:, lo:lo + SCAN_COLS]
                si = sim_ref[:, lo:lo + SCAN_COLS]
                for k in range(T_SCAN // T_GROUP):
                    rk = slice(k * T_GROUP * bsz, (k + 1) * T_GROUP * bsz)
                    uk = u_new[rk, ch]
                    bu_w[j, rk, cre] = jnp.dot(uk, bre_ref[j, :, cre], preferred_element_type=F32)
                    bu_w[j, rk, cre.start + SLAB_STATE:cre.stop + SLAB_STATE] = jnp.dot(
                        uk, bim_ref[j, :, cre], preferred_element_type=F32)
                    for t in range(k * T_GROUP, (k + 1) * T_GROUP):
                        r = slice(t * bsz, (t + 1) * bsz)
                        sr, si = (lr * sr - li * si + bu_r[j, r, cre],
                                  lr * si + li * sr + bu_r[j, r, cim])
                        st_ref[j, r, cre] = sr.astype(BF16)
                        st_ref[j, r, cim] = si.astype(BF16)
                    yk = (jnp.dot(st_ref[j, rk, cre], cre_ref[j, cre, :], preferred_element_type=F32)
                          + jnp.dot(st_ref[j, rk, cim], cim_ref[j, cre, :],
                                    preferred_element_type=F32))
                    if c == 0:
                        y_ref[rk, ch] = yk
                    else:
                        y_ref[rk, ch] += yk
                sre_ref[:, lo:lo + SCAN_COLS] = sr
                sim_ref[:, lo:lo + SCAN_COLS] = si

        y = y_ref[...] + d_ref[...] * ut_r[...].astype(F32)
        g = jax.nn.gelu(y)
        gate = jnp.dot(g.astype(BF16), wglu_ref[...], preferred_element_type=F32) + bglu_ref[...]
        out_tb = (g * jax.nn.sigmoid(gate)).astype(BF16)
        out_bt = jnp.dot(permt_ref[...], out_tb, preferred_element_type=F32).astype(BF16)
        o_ref[...] = out_bt.reshape(bsz, T_SCAN, D_SSM)

    _by_parity(step, body, (bu0_ref, ut0_ref), (bu1_ref, ut1_ref))


def _row_permutation(bsz):
    rows = bsz * T_SCAN
    perm = np.zeros((rows, rows), np.float32)
    t, b = np.meshgrid(np.arange(T_SCAN), np.arange(bsz), indexing="ij")
    perm[(t * bsz + b).ravel(), (b * T_SCAN + t).ravel()] = 1.0
    return perm


def _ssm(u, bre, bim, lre, lim, cre, cim_neg, d, wglu, bglu):
    bsz, seq, _ = u.shape
    rows = T_SCAN * bsz
    nblk = seq // T_SCAN
    perm = _row_permutation(bsz)
    full = lambda shape: pl.BlockSpec(shape, lambda f: (0,) * len(shape))
    bu_scratch = pltpu.VMEM((N_SLABS, rows, 2 * SLAB_STATE), F32)
    ut_scratch = pltpu.VMEM((rows, D_SSM), BF16)
    return pl.pallas_call(
        functools.partial(_ssm_kernel, bsz=bsz),
        grid=(nblk + 1,),
        in_specs=[
            pl.BlockSpec((bsz, T_SCAN, D_SSM), lambda f: (0, jnp.minimum(f, nblk - 1), 0)),
            full((rows, rows)), full((rows, rows)),
            full((N_SLABS, SLAB_CH, SLAB_STATE)), full((N_SLABS, SLAB_CH, SLAB_STATE)),
            full((1, N_STATE)), full((1, N_STATE)),
            full((N_SLABS, SLAB_STATE, SLAB_CH)), full((N_SLABS, SLAB_STATE, SLAB_CH)),
            full((1, D_SSM)), full((D_SSM, D_SSM)), full((1, D_SSM)),
        ],
        out_specs=pl.BlockSpec((bsz, T_SCAN, D_SSM), lambda f: (0, jnp.maximum(f - 1, 0), 0)),
        out_shape=jax.ShapeDtypeStruct((bsz, seq, D_SSM), BF16),
        scratch_shapes=[
            pltpu.VMEM((bsz, N_STATE), F32),
            pltpu.VMEM((bsz, N_STATE), F32),
            pltpu.VMEM((N_SLABS, rows, 2 * SLAB_STATE), BF16),
            pltpu.VMEM((rows, D_SSM), F32),
            bu_scratch, ut_scratch, bu_scratch, ut_scratch,
        ],
        compiler_params=pltpu.CompilerParams(
            dimension_semantics=("arbitrary",), vmem_limit_bytes=VMEM_LIMIT),
        name="s5_scan",
    )(u, jnp.asarray(perm, BF16), jnp.asarray(perm.T, BF16),
      bre, bim, lre, lim, cre, cim_neg, d, wglu, bglu)


def _attention_pieces(first, sink_ref, q_ref, k_ref, v_ref, kp_ref, vp_ref, za_ref, bias_ref,
                      half_ref, a_ref):
    low_q = lax.broadcasted_iota(jnp.int32, (WINDOW, LANES), 1) < HEAD_DIM
    kv_cache = {}

    def keys_values(jj, g):
        if (jj, g) not in kv_cache:
            keep_even, keep_odd = half_ref[0], half_ref[1]
            rows = slice(jj * WINDOW, (jj + 1) * WINDOW)
            if jj == 0:
                kp, vp = kp_ref[...], vp_ref[...]
            else:
                prev = slice((jj - 1) * WINDOW, jj * WINDOW)
                kp, vp = k_ref[prev, :], v_ref[prev, :]
            kk = jnp.concatenate([kp, k_ref[rows, :]], axis=0)
            vv = jnp.concatenate([vp, v_ref[rows, :]], axis=0)
            at_low = slice(g * D_KV, (g + 1) * D_KV)
            at_high = slice((1 - g) * D_KV, (2 - g) * D_KV)
            kcat = jnp.concatenate([kk[:, at_low] * keep_even,
                                    kk[:, at_high] * keep_odd], axis=0)
            vcat = jnp.concatenate(
                [jnp.concatenate([vv[:, at_low] * keep_even, keep_even], axis=1),
                 jnp.concatenate([vv[:, at_high] * keep_odd, keep_odd], axis=1)],
                axis=0)
            kv_cache[(jj, g)] = (kcat, vcat)
        return kv_cache[(jj, g)]

    items = [(jj, pair) for jj in range(TOK_OUT // WINDOW) for pair in range(N_HEADS // 2)]
    scores, probs = {}, {}

    def where(i):
        jj, pair = items[i]
        return (jj, pair, slice(jj * WINDOW, (jj + 1) * WINDOW),
                slice(pair * LANES, (pair + 1) * LANES))

    def stage_scores(i):
        jj, pair, rows, lanes = where(i)
        kcat, _ = keys_values(jj, pair // (Q_PER_KV // 2))
        s = lax.dot_general(q_ref[rows, lanes], kcat, (((1,), (1,)), ((), ())),
                            preferred_element_type=F32)
        scores[i] = s + (bias_ref[pair + (N_HEADS // 2) * first] if jj == 0 else bias_ref[pair])

    def stage_softmax(i):
        _, pair, _, _ = where(i)
        s = scores.pop(i)
        sink_e, sink_o = sink_ref[2 * pair], sink_ref[2 * pair + 1]
        m_e = jnp.maximum(jnp.max(s[:, :2 * WINDOW], axis=-1, keepdims=True), sink_e)
        m_o = jnp.maximum(jnp.max(s[:, 2 * WINDOW:], axis=-1, keepdims=True), sink_o)
        e = jnp.concatenate([jnp.exp(s[:, :2 * WINDOW] - m_e),
                             jnp.exp(s[:, 2 * WINDOW:] - m_o)], axis=1).astype(BF16)
        probs[i] = (e, jnp.where(low_q, jnp.exp(sink_e - m_e), jnp.exp(sink_o - m_o)))

    def stage_output(i):
        jj, pair, rows, lanes = where(i)
        _, vcat = keys_values(jj, pair // (Q_PER_KV // 2))
        e, sink_term = probs.pop(i)
        o = jnp.dot(e, vcat, preferred_element_type=F32)
        den = o[:, LANES:] + sink_term
        a_ref[rows, lanes] = (o[:, :LANES] / den * za_ref[rows, lanes].astype(F32)).astype(BF16)

    def slot(it):
        if it < len(items):
            stage_scores(it)
        if 0 <= it - 1 < len(items):
            stage_softmax(it - 1)
        if 0 <= it - 2 < len(items):
            stage_output(it - 2)

    return [functools.partial(slot, it) for it in range(len(items) + 2)]


def _attn_out_kernel(sink_ref, x_ref, p_ref, ssm_ref, zs_ref, q_ref, k_ref, v_ref, kp_ref, vp_ref,
                     za_ref, bias_ref, half_ref, wout_ref, pg_ref, wgate_ref, bgate_ref, wproj_ref,
                     o_ref, a0_ref, a1_ref, *, blocks_per_seq, n_blocks):
    step = pl.program_id(0)
    seq_block = jnp.minimum(step, n_blocks - 1) % blocks_per_seq
    first = (seq_block == 0).astype(jnp.int32)

    @pl.when(step == 0)
    def _():
        a1_ref[...] = jnp.zeros_like(a1_ref)

    def body(a_w, a_r):
        pieces = _attention_pieces(first, sink_ref, q_ref, k_ref, v_ref, kp_ref, vp_ref, za_ref,
                                   bias_ref, half_ref, a_w)
        n_chunks = D_MODEL // PROJ_COLS
        per_chunk = (len(pieces) - 2) // (2 * n_chunks)
        pieces = iter(pieces)

        def run_pieces(count=per_chunk):
            for piece in (list(pieces) if count is None else [p for _, p in zip(range(count), pieces)]):
                piece()

        run_pieces(2)

        s5 = (ssm_ref[...].astype(F32) * zs_ref[...].astype(F32)).astype(BF16)
        a_prev = a_r[...]
        mixed = []
        for n in range(n_chunks):
            cols = slice(n * PROJ_COLS, (n + 1) * PROJ_COLS)
            mixed.append(jnp.dot(s5, wout_ref[:D_SSM, cols], preferred_element_type=F32)
                         + jnp.dot(a_prev, wout_ref[D_SSM:, cols], preferred_element_type=F32))
            run_pieces()
        mixed = jnp.concatenate(mixed, axis=1)
        ms = jnp.mean(mixed * mixed, axis=-1, keepdims=True)
        h1 = x_ref[...] + mixed * lax.rsqrt(ms + EPS) * pg_ref[...]
        h1b = h1.astype(BF16)
        pb = p_ref[...].astype(BF16)
        for n in range(n_chunks):
            cols = slice(n * PROJ_COLS, (n + 1) * PROJ_COLS)
            gate = jax.nn.sigmoid(
                jnp.dot(h1b, wgate_ref[:, cols], preferred_element_type=F32) + bgate_ref[:, cols])
            ple = jnp.dot(pb, wproj_ref[:, cols], preferred_element_type=F32)
            o_ref[:, cols] = h1[:, cols] + gate * ple
            run_pieces()
        run_pieces(None)

    _by_parity(step, body, a0_ref, a1_ref)


def _attn_bias_table():
    q_idx = np.arange(WINDOW)[:, None]
    s_idx = np.arange(2 * WINDOW)[None, :]
    dist = q_idx + WINDOW - s_idx
    valid = (dist >= 0) & (dist < WINDOW)
    slopes = np.exp2(-8.0 * (np.arange(N_HEADS, dtype=np.float32) + 1.0) / N_HEADS).astype(np.float32)
    bias = -slopes[:, None, None] * dist.astype(np.float32)[None]
    normal = np.where(valid[None], bias, -np.inf)
    first = np.where((valid & (s_idx >= WINDOW))[None], bias, -np.inf)
    paired = lambda t: t.reshape(N_HEADS // 2, 2, WINDOW, 2 * WINDOW).transpose(0, 2, 1, 3).reshape(
        N_HEADS // 2, WINDOW, 4 * WINDOW)
    return np.concatenate([paired(normal), paired(first)], axis=0).astype(np.float32)


def _attn_out(sinks, x, p, ssm, zs, q, k, v, za, wout, pg, wgate, bgate, wproj):
    bsz, seq, _ = x.shape
    sub = TOK_OUT // WINDOW
    per_seq = seq // TOK_OUT
    n_blocks = bsz * per_seq

    def cur(f):
        f = jnp.minimum(f, n_blocks - 1)
        return f // per_seq, f % per_seq

    def prv(f):
        f = jnp.maximum(f - 1, 0)
        return f // per_seq, f % per_seq

    front = lambda width: pl.BlockSpec((None, TOK_OUT, width), lambda f: (*cur(f), 0))
    back = lambda width: pl.BlockSpec((None, TOK_OUT, width), lambda f: (*prv(f), 0))
    prev_kv = pl.BlockSpec(
        (None, WINDOW, 2 * D_KV),
        lambda f: (cur(f)[0], jnp.maximum(cur(f)[1] * sub - 1, 0), 0))
    full = lambda shape: pl.BlockSpec(shape, lambda f: (0,) * len(shape))
    bias = jnp.asarray(_attn_bias_table())
    lane_is_even = np.broadcast_to(np.arange(LANES) < HEAD_DIM, (2 * WINDOW, LANES))
    half = jnp.asarray(np.stack([lane_is_even, ~lane_is_even]), BF16)
    a_scratch = pltpu.VMEM((TOK_OUT, D_ATTN), BF16)
    return pl.pallas_call(
        functools.partial(_attn_out_kernel, blocks_per_seq=per_seq, n_blocks=n_blocks),
        grid=(n_blocks + 1,),
        in_specs=[
            pl.BlockSpec(memory_space=pltpu.SMEM),
            back(D_MODEL), back(D_PLE), back(D_SSM), back(D_SSM), front(D_ATTN),
            front(2 * D_KV), front(2 * D_KV), prev_kv, prev_kv, front(D_ATTN),
            full(bias.shape), full(half.shape),
            full((D_MODEL, D_MODEL)), full((1, D_MODEL)),
            full((D_MODEL, D_MODEL)), full((1, D_MODEL)), full((D_PLE, D_MODEL)),
        ],
        out_specs=back(D_MODEL),
        out_shape=jax.ShapeDtypeStruct((bsz, seq, D_MODEL), F32),
        scratch_shapes=[a_scratch, a_scratch],
        compiler_params=pltpu.CompilerParams(
            dimension_semantics=("arbitrary",), vmem_limit_bytes=VMEM_LIMIT),
        name="attn_out",
    )(sinks, x, p, ssm, zs, q, k, v, k, v, za, bias, half, wout, pg, wgate, bgate, wproj)


def _block_diag(blocks):
    s, n, a, b = blocks.shape
    eye = jnp.eye(n, dtype=blocks.dtype)
    return jnp.einsum('snab,nm->snamb', blocks, eye).reshape(s, n * a, n * b)


def _ssm_params(lam_re, lam_im, log_step, b_re, b_im, c_re, c_im):
    lr, li = lam_re.astype(F32), lam_im.astype(F32)
    step = jnp.exp(log_step.astype(F32))[:, None]
    mag = jnp.exp(lr * step)
    lbr, lbi = mag * jnp.cos(li * step), mag * jnp.sin(li * step)
    den = lr * lr + li * li
    fr = ((lbr - 1.0) * lr + lbi * li) / den
    fi = (lbi * lr - (lbr - 1.0) * li) / den
    br, bi = b_re.astype(F32), b_im.astype(F32)
    bbr = fr[..., None] * br - fi[..., None] * bi
    bbi = fr[..., None] * bi + fi[..., None] * br
    gps = SLAB_CH // SSM_GROUP_CH

    def in_w(part):
        blocks = part.reshape(N_SLABS, gps, SSM_STATE, SSM_GROUP_CH).transpose(0, 1, 3, 2)
        return _block_diag(blocks).astype(BF16)

    def out_w(part):
        blocks = part.reshape(N_SLABS, gps, SSM_GROUP_CH, SSM_STATE).transpose(0, 1, 3, 2)
        return _block_diag(blocks).astype(BF16)

    return (in_w(bbr), in_w(bbi), lbr.reshape(1, N_STATE), lbi.reshape(1, N_STATE),
            out_w(c_re.astype(F32)), out_w(-c_im.astype(F32)))


def _widen_w_in(w_in):
    w = w_in.astype(BF16)
    off_k = 2 * D_SSM + D_ATTN
    swap = lambda c: jnp.concatenate([c[:, HEAD_DIM:], c[:, :HEAD_DIM]], axis=1)
    wk, wv = w[:, off_k:off_k + D_KV], w[:, off_k + D_KV:off_k + 2 * D_KV]
    return jnp.concatenate([w[:, :off_k], wk, swap(wk), wv, swap(wv), w[:, off_k + 2 * D_KV:]], axis=1)


def _layer(h, p, pre_norm_g, w_in, ssm_lam_re, ssm_lam_im, ssm_log_step, ssm_b_re, ssm_b_im,
           ssm_c_re, ssm_c_im, ssm_d, ssm_w_glu, ssm_b_glu, attn_sinks, w_out, post_norm_g,
           pl_w_proj, pl_w_gate, pl_b_gate):
    row = lambda a: a.astype(F32).reshape(1, -1)
    u, zs, q, k, v, za = _in_proj(h, row(pre_norm_g), _widen_w_in(w_in))
    bre, bim, lre, lim, cre, cim_neg = _ssm_params(
        ssm_lam_re, ssm_lam_im, ssm_log_step, ssm_b_re, ssm_b_im, ssm_c_re, ssm_c_im)
    ssm = _ssm(u, bre, bim, lre, lim, cre, cim_neg,
               row(ssm_d), ssm_w_glu.astype(BF16), row(ssm_b_glu))
    return _attn_out(attn_sinks.astype(F32), h, p, ssm, zs, q, k, v, za,
                     w_out.astype(BF16), row(post_norm_g), pl_w_gate.astype(BF16), row(pl_b_gate),
                     pl_w_proj.astype(BF16))


def kernel(x, p, pre_norm_g, w_in, ssm_lam_re, ssm_lam_im, ssm_log_step, ssm_b_re, ssm_b_im,
           ssm_c_re, ssm_c_im, ssm_d, ssm_w_glu, ssm_b_glu, attn_sinks, w_out, post_norm_g,
           pl_w_proj, pl_w_gate, pl_b_gate):
    h = x
    for i in range(p.shape[0]):
        h = _layer(h, p[i], pre_norm_g[i], w_in[i], ssm_lam_re[i], ssm_lam_im[i], ssm_log_step[i],
                   ssm_b_re[i], ssm_b_im[i], ssm_c_re[i], ssm_c_im[i], ssm_d[i], ssm_w_glu[i],
                   ssm_b_glu[i], attn_sinks[i], w_out[i], post_norm_g[i], pl_w_proj[i],
                   pl_w_gate[i], pl_b_gate[i])
    return h
```

```python
import functools
import math

import numpy as np
import jax
import jax.numpy as jnp
from jax import lax
from jax.experimental import pallas as pl
from jax.experimental.pallas import tpu as pltpu

F32 = jnp.float32
BF16 = jnp.bfloat16

D_MODEL = 1024
D_SSM = 512
D_ATTN = 512
SSM_GROUP_CH = 16
SSM_GROUPS = D_SSM // SSM_GROUP_CH
SSM_STATE = 64
N_STATE = SSM_GROUPS * SSM_STATE
HEAD_DIM = 64
N_HEADS = D_ATTN // HEAD_DIM
KV_HEADS = 2
Q_PER_KV = N_HEADS // KV_HEADS
D_KV = KV_HEADS * HEAD_DIM
WINDOW = 128
D_PLE = 256
EPS = 1e-6

LANES = 128
assert D_KV == LANES and 2 * HEAD_DIM == LANES

OFF_U, OFF_ZS, OFF_Q, OFF_K, OFF_V, OFF_ZA, D_IN_WIDE = 0, 512, 1024, 1536, 1792, 2048, 2560

SLAB_CH = LANES
N_SLABS = D_SSM // SLAB_CH
SLAB_STATE = (SLAB_CH // SSM_GROUP_CH) * SSM_STATE
SCAN_COLS = 256

TOK_IN = 512
T_SCAN = 32
T_GROUP = 8
TOK_OUT = 512
PROJ_COLS = 256
VMEM_LIMIT = 48 * 1024 * 1024


def _silu(z):
    return z * jax.nn.sigmoid(z)


def _by_parity(step, body, even_bufs, odd_bufs):
    @pl.when(step % 2 == 0)
    def _():
        body(even_bufs, odd_bufs)

    @pl.when(step % 2 == 1)
    def _():
        body(odd_bufs, even_bufs)


def _in_proj_kernel(x_ref, g_ref, w_ref, u_ref, zs_ref, q_ref, k_ref, v_ref, za_ref):
    x = x_ref[...]
    ms = jnp.mean(x * x, axis=-1, keepdims=True)
    hn = (x * lax.rsqrt(ms + EPS) * g_ref[...]).astype(BF16)

    def proj(lo, hi):
        return jnp.dot(hn, w_ref[:, lo:hi], preferred_element_type=F32)

    u_ref[...] = proj(OFF_U, OFF_ZS).astype(BF16)
    zs_ref[...] = _silu(proj(OFF_ZS, OFF_Q)).astype(BF16)
    q_ref[...] = (proj(OFF_Q, OFF_K) * (1.0 / math.sqrt(HEAD_DIM))).astype(BF16)
    k_ref[...] = proj(OFF_K, OFF_V).astype(BF16)
    v_ref[...] = proj(OFF_V, OFF_ZA).astype(BF16)
    za_ref[...] = _silu(proj(OFF_ZA, D_IN_WIDE)).astype(BF16)


def _in_proj(x, g, w_wide):
    bsz, seq, _ = x.shape
    tok = lambda width: pl.BlockSpec((None, TOK_IN, width), lambda b, i: (b, i, 0))
    full = lambda shape: pl.BlockSpec(shape, lambda b, i: (0,) * len(shape))
    widths = (D_SSM, D_SSM, D_ATTN, 2 * D_KV, 2 * D_KV, D_ATTN)
    return pl.pallas_call(
        _in_proj_kernel,
        grid=(bsz, seq // TOK_IN),
        in_specs=[tok(D_MODEL), full((1, D_MODEL)), full((D_MODEL, D_IN_WIDE))],
        out_specs=tuple(tok(w) for w in widths),
        out_shape=tuple(jax.ShapeDtypeStruct((bsz, seq, w), BF16) for w in widths),
        compiler_params=pltpu.CompilerParams(
            dimension_semantics=("parallel", "parallel"), vmem_limit_bytes=VMEM_LIMIT),
        name="in_proj",
    )(x, g, w_wide)


def _ssm_kernel(u_ref, perm_ref, permt_ref, bre_ref, bim_ref, lre_ref, lim_ref, cre_ref, cim_ref,
                d_ref, wglu_ref, bglu_ref, o_ref, sre_ref, sim_ref, st_ref, y_ref,
                bu0_ref, ut0_ref, bu1_ref, ut1_ref, *, bsz):
    rows = bsz * T_SCAN
    step = pl.program_id(0)

    @pl.when(step == 0)
    def _():
        sre_ref[...] = jnp.zeros_like(sre_ref)
        sim_ref[...] = jnp.zeros_like(sim_ref)
        bu1_ref[...] = jnp.zeros_like(bu1_ref)
        ut1_ref[...] = jnp.zeros_like(ut1_ref)

    def body(write, read):
        bu_w, ut_w = write
        bu_r, ut_r = read

        u_bt = u_ref[...].reshape(rows, D_SSM)
        u_new = jnp.dot(perm_ref[...], u_bt, preferred_element_type=F32).astype(BF16)
        ut_w[...] = u_new
        for j in range(N_SLABS):
            ch = slice(j * SLAB_CH, (j + 1) * SLAB_CH)
            for c in range(SLAB_STATE // SCAN_COLS):
                lo = j * SLAB_STATE + c * SCAN_COLS
                cre = slice(c * SCAN_COLS, (c + 1) * SCAN_COLS)
                cim = slice(SLAB_STATE + c * SCAN_COLS, SLAB_STATE + (c + 1) * SCAN_COLS)
                lr = jnp.broadcast_to(lre_ref[:, lo:lo + SCAN_COLS], (bsz, SCAN_COLS))
                li = jnp.broadcast_to(lim_ref[:, lo:lo + SCAN_COLS], (bsz, SCAN_COLS))
                sr = sre_ref[:, lo:lo + SCAN_COLS]
                si = sim_ref[:, lo:lo + SCAN_COLS]
                for k in range(T_SCAN // T_GROUP):
                    rk = slice(k * T_GROUP * bsz, (k + 1) * T_GROUP * bsz)
                    uk = u_new[rk, ch]
                    bu_w[j, rk, cre] = jnp.dot(uk, bre_ref[j, :, cre], preferred_element_type=F32)
                    bu_w[j, rk, cre.start + SLAB_STATE:cre.stop + SLAB_STATE] = jnp.dot(
                        uk, bim_ref[j, :, cre], preferred_element_type=F32)
                    for t in range(k * T_GROUP, (k + 1) * T_GROUP):
                        r = slice(t * bsz, (t + 1) * bsz)
                        sr, si = (lr * sr - li * si + bu_r[j, r, cre],
                                  lr * si + li * sr + bu_r[j, r, cim])
                        st_ref[j, r, cre] = sr.astype(BF16)
                        st_ref[j, r, cim] = si.astype(BF16)
                    yk = (jnp.dot(st_ref[j, rk, cre], cre_ref[j, cre, :], preferred_element_type=F32)
                          + jnp.dot(st_ref[j, rk, cim], cim_ref[j, cre, :],
                                    preferred_element_type=F32))
                    if c == 0:
                        y_ref[rk, ch] = yk
                    else:
                        y_ref[rk, ch] += yk
                sre_ref[:, lo:lo + SCAN_COLS] = sr
                sim_ref[:, lo:lo + SCAN_COLS] = si

        y = y_ref[...] + d_ref[...] * ut_r[...].astype(F32)
        g = jax.nn.gelu(y)
        gate = jnp.dot(g.astype(BF16), wglu_ref[...], preferred_element_type=F32) + bglu_ref[...]
        out_tb = (g * jax.nn.sigmoid(gate)).astype(BF16)
        out_bt = jnp.dot(permt_ref[...], out_tb, preferred_element_type=F32).astype(BF16)
        o_ref[...] = out_bt.reshape(bsz, T_SCAN, D_SSM)

    _by_parity(step, body, (bu0_ref, ut0_ref), (bu1_ref, ut1_ref))


def _row_permutation(bsz):
    rows = bsz * T_SCAN
    perm = np.zeros((rows, rows), np.float32)
    t, b = np.meshgrid(np.arange(T_SCAN), np.arange(bsz), indexing="ij")
    perm[(t * bsz + b).ravel(), (b * T_SCAN + t).ravel()] = 1.0
    return perm


def _ssm(u, bre, bim, lre, lim, cre, cim_neg, d, wglu, bglu):
    bsz, seq, _ = u.shape
    rows = T_SCAN * bsz
    nblk = seq // T_SCAN
    perm = _row_permutation(bsz)
    full = lambda shape: pl.BlockSpec(shape, lambda f: (0,) * len(shape))
    bu_scratch = pltpu.VMEM((N_SLABS, rows, 2 * SLAB_STATE), F32)
    ut_scratch = pltpu.VMEM((rows, D_SSM), BF16)
    return pl.pallas_call(
        functools.partial(_ssm_kernel, bsz=bsz),
        grid=(nblk + 1,),
        in_specs=[
            pl.BlockSpec((bsz, T_SCAN, D_SSM), lambda f: (0, jnp.minimum(f, nblk - 1), 0)),
            full((rows, rows)), full((rows, rows)),
            full((N_SLABS, SLAB_CH, SLAB_STATE)), full((N_SLABS, SLAB_CH, SLAB_STATE)),
            full((1, N_STATE)), full((1, N_STATE)),
            full((N_SLABS, SLAB_STATE, SLAB_CH)), full((N_SLABS, SLAB_STATE, SLAB_CH)),
            full((1, D_SSM)), full((D_SSM, D_SSM)), full((1, D_SSM)),
        ],
        out_specs=pl.BlockSpec((bsz, T_SCAN, D_SSM), lambda f: (0, jnp.maximum(f - 1, 0), 0)),
        out_shape=jax.ShapeDtypeStruct((bsz, seq, D_SSM), BF16),
        scratch_shapes=[
            pltpu.VMEM((bsz, N_STATE), F32),
            pltpu.VMEM((bsz, N_STATE), F32),
            pltpu.VMEM((N_SLABS, rows, 2 * SLAB_STATE), BF16),
            pltpu.VMEM((rows, D_SSM), F32),
            bu_scratch, ut_scratch, bu_scratch, ut_scratch,
        ],
        compiler_params=pltpu.CompilerParams(
            dimension_semantics=("arbitrary",), vmem_limit_bytes=VMEM_LIMIT),
        name="s5_scan",
    )(u, jnp.asarray(perm, BF16), jnp.asarray(perm.T, BF16),
      bre, bim, lre, lim, cre, cim_neg, d, wglu, bglu)


def _attention_pieces(first, sink_ref, q_ref, k_ref, v_ref, kp_ref, vp_ref, za_ref, bias_ref,
                      half_ref, a_ref):
    low_q = lax.broadcasted_iota(jnp.int32, (WINDOW, LANES), 1) < HEAD_DIM
    kv_cache = {}

    def keys_values(jj, g):
        if (jj, g) not in kv_cache:
            keep_even, keep_odd = half_ref[0], half_ref[1]
            rows = slice(jj * WINDOW, (jj + 1) * WINDOW)
            if jj == 0:
                kp, vp = kp_ref[...], vp_ref[...]
            else:
                prev = slice((jj - 1) * WINDOW, jj * WINDOW)
                kp, vp = k_ref[prev, :], v_ref[prev, :]
            kk = jnp.concatenate([kp, k_ref[rows, :]], axis=0)
            vv = jnp.concatenate([vp, v_ref[rows, :]], axis=0)
            at_low = slice(g * D_KV, (g + 1) * D_KV)
            at_high = slice((1 - g) * D_KV, (2 - g) * D_KV)
            kcat = jnp.concatenate([kk[:, at_low] * keep_even,
                                    kk[:, at_high] * keep_odd], axis=0)
            vcat = jnp.concatenate(
                [jnp.concatenate([vv[:, at_low] * keep_even, keep_even], axis=1),
                 jnp.concatenate([vv[:, at_high] * keep_odd, keep_odd], axis=1)],
                axis=0)
            kv_cache[(jj, g)] = (kcat, vcat)
        return kv_cache[(jj, g)]

    items = [(jj, pair) for jj in range(TOK_OUT // WINDOW) for pair in range(N_HEADS // 2)]
    scores, probs = {}, {}

    def where(i):
        jj, pair = items[i]
        return (jj, pair, slice(jj * WINDOW, (jj + 1) * WINDOW),
                slice(pair * LANES, (pair + 1) * LANES))

    def stage_scores(i):
        jj, pair, rows, lanes = where(i)
        kcat, _ = keys_values(jj, pair // (Q_PER_KV // 2))
        s = lax.dot_general(q_ref[rows, lanes], kcat, (((1,), (1,)), ((), ())),
                            preferred_element_type=F32)
        scores[i] = s + (bias_ref[pair + (N_HEADS // 2) * first] if jj == 0 else bias_ref[pair])

    def stage_softmax(i):
        _, pair, _, _ = where(i)
        s = scores.pop(i)
        sink_e, sink_o = sink_ref[2 * pair], sink_ref[2 * pair + 1]
        m_e = jnp.maximum(jnp.max(s[:, :2 * WINDOW], axis=-1, keepdims=True), sink_e)
        m_o = jnp.maximum(jnp.max(s[:, 2 * WINDOW:], axis=-1, keepdims=True), sink_o)
        e = jnp.concatenate([jnp.exp(s[:, :2 * WINDOW] - m_e),
                             jnp.exp(s[:, 2 * WINDOW:] - m_o)], axis=1).astype(BF16)
        probs[i] = (e, jnp.where(low_q, jnp.exp(sink_e - m_e), jnp.exp(sink_o - m_o)))

    def stage_output(i):
        jj, pair, rows, lanes = where(i)
        _, vcat = keys_values(jj, pair // (Q_PER_KV // 2))
        e, sink_term = probs.pop(i)
        o = jnp.dot(e, vcat, preferred_element_type=F32)
        den = o[:, LANES:] + sink_term
        a_ref[rows, lanes] = (o[:, :LANES] / den * za_ref[rows, lanes].astype(F32)).astype(BF16)

    def slot(it):
        if it < len(items):
            stage_scores(it)
        if 0 <= it - 1 < len(items):
            stage_softmax(it - 1)
        if 0 <= it - 2 < len(items):
            stage_output(it - 2)

    return [functools.partial(slot, it) for it in range(len(items) + 2)]


def _attn_out_kernel(sink_ref, x_ref, p_ref, ssm_ref, zs_ref, q_ref, k_ref, v_ref, kp_ref, vp_ref,
                     za_ref, bias_ref, half_ref, wout_ref, pg_ref, wgate_ref, bgate_ref, wproj_ref,
                     o_ref, a0_ref, a1_ref, *, blocks_per_seq, n_blocks):
    step = pl.program_id(0)
    seq_block = jnp.minimum(step, n_blocks - 1) % blocks_per_seq
    first = (seq_block == 0).astype(jnp.int32)

    @pl.when(step == 0)
    def _():
        a1_ref[...] = jnp.zeros_like(a1_ref)

    def body(a_w, a_r):
        pieces = _attention_pieces(first, sink_ref, q_ref, k_ref, v_ref, kp_ref, vp_ref, za_ref,
                                   bias_ref, half_ref, a_w)
        n_chunks = D_MODEL // PROJ_COLS
        per_chunk = (len(pieces) - 2) // (2 * n_chunks)
        pieces = iter(pieces)

        def run_pieces(count=per_chunk):
            for piece in (list(pieces) if count is None else [p for _, p in zip(range(count), pieces)]):
                piece()

        run_pieces(2)

        s5 = (ssm_ref[...].astype(F32) * zs_ref[...].astype(F32)).astype(BF16)
        a_prev = a_r[...]
        mixed = []
        for n in range(n_chunks):
            cols = slice(n * PROJ_COLS, (n + 1) * PROJ_COLS)
            mixed.append(jnp.dot(s5, wout_ref[:D_SSM, cols], preferred_element_type=F32)
                         + jnp.dot(a_prev, wout_ref[D_SSM:, cols], preferred_element_type=F32))
            run_pieces()
        mixed = jnp.concatenate(mixed, axis=1)
        ms = jnp.mean(mixed * mixed, axis=-1, keepdims=True)
        h1 = x_ref[...] + mixed * lax.rsqrt(ms + EPS) * pg_ref[...]
        h1b = h1.astype(BF16)
        pb = p_ref[...].astype(BF16)
        for n in range(n_chunks):
            cols = slice(n * PROJ_COLS, (n + 1) * PROJ_COLS)
            gate = jax.nn.sigmoid(
                jnp.dot(h1b, wgate_ref[:, cols], preferred_element_type=F32) + bgate_ref[:, cols])
            ple = jnp.dot(pb, wproj_ref[:, cols], preferred_element_type=F32)
            o_ref[:, cols] = h1[:, cols] + gate * ple
            run_pieces()
        run_pieces(None)

    _by_parity(step, body, a0_ref, a1_ref)


def _attn_bias_table():
    q_idx = np.arange(WINDOW)[:, None]
    s_idx = np.arange(2 * WINDOW)[None, :]
    dist = q_idx + WINDOW - s_idx
    valid = (dist >= 0) & (dist < WINDOW)
    slopes = np.exp2(-8.0 * (np.arange(N_HEADS, dtype=np.float32) + 1.0) / N_HEADS).astype(np.float32)
    bias = -slopes[:, None, None] * dist.astype(np.float32)[None]
    normal = np.where(valid[None], bias, -np.inf)
    first = np.where((valid & (s_idx >= WINDOW))[None], bias, -np.inf)
    paired = lambda t: t.reshape(N_HEADS // 2, 2, WINDOW, 2 * WINDOW).transpose(0, 2, 1, 3).reshape(
        N_HEADS // 2, WINDOW, 4 * WINDOW)
    return np.concatenate([paired(normal), paired(first)], axis=0).astype(np.float32)


def _attn_out(sinks, x, p, ssm, zs, q, k, v, za, wout, pg, wgate, bgate, wproj):
    bsz, seq, _ = x.shape
    sub = TOK_OUT // WINDOW
    per_seq = seq // TOK_OUT
    n_blocks = bsz * per_seq

    def cur(f):
        f = jnp.minimum(f, n_blocks - 1)
        return f // per_seq, f % per_seq

    def prv(f):
        f = jnp.maximum(f - 1, 0)
        return f // per_seq, f % per_seq

    front = lambda width: pl.BlockSpec((None, TOK_OUT, width), lambda f: (*cur(f), 0))
    back = lambda width: pl.BlockSpec((None, TOK_OUT, width), lambda f: (*prv(f), 0))
    prev_kv = pl.BlockSpec(
        (None, WINDOW, 2 * D_KV),
        lambda f: (cur(f)[0], jnp.maximum(cur(f)[1] * sub - 1, 0), 0))
    full = lambda shape: pl.BlockSpec(shape, lambda f: (0,) * len(shape))
    bias = jnp.asarray(_attn_bias_table())
    lane_is_even = np.broadcast_to(np.arange(LANES) < HEAD_DIM, (2 * WINDOW, LANES))
    half = jnp.asarray(np.stack([lane_is_even, ~lane_is_even]), BF16)
    a_scratch = pltpu.VMEM((TOK_OUT, D_ATTN), BF16)
    return pl.pallas_call(
        functools.partial(_attn_out_kernel, blocks_per_seq=per_seq, n_blocks=n_blocks),
        grid=(n_blocks + 1,),
        in_specs=[
            pl.BlockSpec(memory_space=pltpu.SMEM),
            back(D_MODEL), back(D_PLE), back(D_SSM), back(D_SSM), front(D_ATTN),
            front(2 * D_KV), front(2 * D_KV), prev_kv, prev_kv, front(D_ATTN),
            full(bias.shape), full(half.shape),
            full((D_MODEL, D_MODEL)), full((1, D_MODEL)),
            full((D_MODEL, D_MODEL)), full((1, D_MODEL)), full((D_PLE, D_MODEL)),
        ],
        out_specs=back(D_MODEL),
        out_shape=jax.ShapeDtypeStruct((bsz, seq, D_MODEL), F32),
        scratch_shapes=[a_scratch, a_scratch],
        compiler_params=pltpu.CompilerParams(
            dimension_semantics=("arbitrary",), vmem_limit_bytes=VMEM_LIMIT),
        name="attn_out",
    )(sinks, x, p, ssm, zs, q, k, v, k, v, za, bias, half, wout, pg, wgate, bgate, wproj)


def _block_diag(blocks):
    s, n, a, b = blocks.shape
    eye = jnp.eye(n, dtype=blocks.dtype)
    return jnp.einsum('snab,nm->snamb', blocks, eye).reshape(s, n * a, n * b)


def _ssm_params(lam_re, lam_im, log_step, b_re, b_im, c_re, c_im):
    lr, li = lam_re.astype(F32), lam_im.astype(F32)
    step = jnp.exp(log_step.astype(F32))[:, None]
    mag = jnp.exp(lr * step)
    lbr, lbi = mag * jnp.cos(li * step), mag * jnp.sin(li * step)
    den = lr * lr + li * li
    fr = ((lbr - 1.0) * lr + lbi * li) / den
    fi = (lbi * lr - (lbr - 1.0) * li) / den
    br, bi = b_re.astype(F32), b_im.astype(F32)
    bbr = fr[..., None] * br - fi[..., None] * bi
    bbi = fr[..., None] * bi + fi[..., None] * br
    gps = SLAB_CH // SSM_GROUP_CH

    def in_w(part):
        blocks = part.reshape(N_SLABS, gps, SSM_STATE, SSM_GROUP_CH).transpose(0, 1, 3, 2)
        return _block_diag(blocks).astype(BF16)

    def out_w(part):
        blocks = part.reshape(N_SLABS, gps, SSM_GROUP_CH, SSM_STATE).transpose(0, 1, 3, 2)
        return _block_diag(blocks).astype(BF16)

    return (in_w(bbr), in_w(bbi), lbr.reshape(1, N_STATE), lbi.reshape(1, N_STATE),
            out_w(c_re.astype(F32)), out_w(-c_im.astype(F32)))


def _widen_w_in(w_in):
    w = w_in.astype(BF16)
    off_k = 2 * D_SSM + D_ATTN
    swap = lambda c: jnp.concatenate([c[:, HEAD_DIM:], c[:, :HEAD_DIM]], axis=1)
    wk, wv = w[:, off_k:off_k + D_KV], w[:, off_k + D_KV:off_k + 2 * D_KV]
    return jnp.concatenate([w[:, :off_k], wk, swap(wk), wv, swap(wv), w[:, off_k + 2 * D_KV:]], axis=1)


def _layer(h, p, pre_norm_g, w_in, ssm_lam_re, ssm_lam_im, ssm_log_step, ssm_b_re, ssm_b_im,
           ssm_c_re, ssm_c_im, ssm_d, ssm_w_glu, ssm_b_glu, attn_sinks, w_out, post_norm_g,
           pl_w_proj, pl_w_gate, pl_b_gate):
    row = lambda a: a.astype(F32).reshape(1, -1)
    u, zs, q, k, v, za = _in_proj(h, row(pre_norm_g), _widen_w_in(w_in))
    bre, bim, lre, lim, cre, cim_neg = _ssm_params(
        ssm_lam_re, ssm_lam_im, ssm_log_step, ssm_b_re, ssm_b_im, ssm_c_re, ssm_c_im)
    ssm = _ssm(u, bre, bim, lre, lim, cre, cim_neg,
               row(ssm_d), ssm_w_glu.astype(BF16), row(ssm_b_glu))
    return _attn_out(attn_sinks.astype(F32), h, p, ssm, zs, q, k, v, za,
                     w_out.astype(BF16), row(post_norm_g), pl_w_gate.astype(BF16), row(pl_b_gate),
                     pl_w_proj.astype(BF16))


def kernel(x, p, pre_norm_g, w_in, ssm_lam_re, ssm_lam_im, ssm_log_step, ssm_b_re, ssm_b_im,
           ssm_c_re, ssm_c_im, ssm_d, ssm_w_glu, ssm_b_glu, attn_sinks, w_out, post_norm_g,
           pl_w_proj, pl_w_gate, pl_b_gate):
    h = x
    for i in range(p.shape[0]):
        h = _layer(h, p[i], pre_norm_g[i], w_in[i], ssm_lam_re[i], ssm_lam_im[i], ssm_log_step[i],
                   ssm_b_re[i], ssm_b_im[i], ssm_c_re[i], ssm_c_im[i], ssm_d[i], ssm_w_glu[i],
                   ssm_b_glu[i], attn_sinks[i], w_out[i], post_norm_g[i], pl_w_proj[i],
                   pl_w_gate[i], pl_b_gate[i])
    return h
```

```python
import functools
import math

import numpy as np
import jax
import jax.numpy as jnp
from jax import lax
from jax.experimental import pallas as pl
from jax.experimental.pallas import tpu as pltpu

F32 = jnp.float32
BF16 = jnp.bfloat16

D_MODEL = 1024
D_SSM = 512
D_ATTN = 512
SSM_GROUP_CH = 16
SSM_GROUPS = D_SSM // SSM_GROUP_CH
SSM_STATE = 64
N_STATE = SSM_GROUPS * SSM_STATE
HEAD_DIM = 64
N_HEADS = D_ATTN // HEAD_DIM
KV_HEADS = 2
Q_PER_KV = N_HEADS // KV_HEADS
D_KV = KV_HEADS * HEAD_DIM
WINDOW = 128
D_PLE = 256
EPS = 1e-6

LANES = 128
assert D_KV == LANES and 2 * HEAD_DIM == LANES

OFF_U, OFF_ZS, OFF_Q, OFF_K, OFF_V, OFF_ZA, D_IN = 0, 512, 1024, 1536, 1664, 1792, 2304

SLAB_CH = LANES
N_SLABS = D_SSM // SLAB_CH
SLAB_STATE = (SLAB_CH // SSM_GROUP_CH) * SSM_STATE
SCAN_COLS = 256

TOK_IN = 1024
NORM_ROWS = 512
T_SCAN = 32
T_GROUP = 8
TOK_OUT = 512
PROJ_COLS = 256
VMEM_LIMIT = 48 * 1024 * 1024


def _silu(z):
    return z * jax.nn.sigmoid(z)


def _by_parity(step, body, even_bufs, odd_bufs):
    @pl.when(step % 2 == 0)
    def _():
        body(even_bufs, odd_bufs)

    @pl.when(step % 2 == 1)
    def _():
        body(odd_bufs, even_bufs)


def _in_proj_kernel(x_ref, g_ref, w_ref, u_ref, zs_ref, q_ref, k_ref, v_ref, za_ref, w_scr):
    @pl.when((pl.program_id(0) == 0) & (pl.program_id(1) == 0))
    def _():
        w_scr[...] = w_ref[...].astype(BF16)

    def emit(rows, tile, acc):
        lo = tile * PROJ_COLS
        if lo < OFF_ZS:
            u_ref[rows, lo - OFF_U:lo - OFF_U + PROJ_COLS] = acc.astype(BF16)
        elif lo < OFF_Q:
            zs_ref[rows, lo - OFF_ZS:lo - OFF_ZS + PROJ_COLS] = _silu(acc).astype(BF16)
        elif lo < OFF_K:
            q_ref[rows, lo - OFF_Q:lo - OFF_Q + PROJ_COLS] = (
                acc * (1.0 / math.sqrt(HEAD_DIM))).astype(BF16)
        elif lo < OFF_ZA:
            k_ref[rows, :] = acc[:, :D_KV].astype(BF16)
            v_ref[rows, :] = acc[:, D_KV:].astype(BF16)
        else:
            za_ref[rows, lo - OFF_ZA:lo - OFF_ZA + PROJ_COLS] = _silu(acc).astype(BF16)

    gain = g_ref[...]
    for half in range(TOK_IN // NORM_ROWS):
        rows = slice(half * NORM_ROWS, (half + 1) * NORM_ROWS)
        x = x_ref[rows, :]
        ms = jnp.mean(x * x, axis=-1, keepdims=True)
        hn = (x * lax.rsqrt(ms + EPS) * gain).astype(BF16)
        for tile in range(D_IN // PROJ_COLS):
            cols = slice(tile * PROJ_COLS, (tile + 1) * PROJ_COLS)
            emit(rows, tile, jnp.dot(hn, w_scr[:, cols], preferred_element_type=F32))


def _in_proj(x, g, w_in):
    bsz, seq, _ = x.shape
    tok = lambda width: pl.BlockSpec((None, TOK_IN, width), lambda b, i: (b, i, 0))
    widths = (D_SSM, D_SSM, D_ATTN, D_KV, D_KV, D_ATTN)
    return pl.pallas_call(
        _in_proj_kernel,
        grid=(bsz, seq // TOK_IN),
        in_specs=[
            tok(D_MODEL),
            pl.BlockSpec((1, D_MODEL), lambda b, i: (0, 0)),
            pl.BlockSpec((D_MODEL, D_IN), lambda b, i: (0, 0), pipeline_mode=pl.Buffered(1)),
        ],
        out_specs=tuple(tok(w) for w in widths),
        out_shape=tuple(jax.ShapeDtypeStruct((bsz, seq, w), BF16) for w in widths),
        scratch_shapes=[pltpu.VMEM((D_MODEL, D_IN), BF16)],
        compiler_params=pltpu.CompilerParams(
            dimension_semantics=("arbitrary", "arbitrary"), vmem_limit_bytes=VMEM_LIMIT),
        name="in_proj",
    )(x, g, w_in)


def _ssm_kernel(u_ref, perm_ref, permt_ref, bre_ref, bim_ref, lre_ref, lim_ref, cre_ref, cim_ref,
                d_ref, wglu_ref, bglu_ref, o_ref, sre_ref, sim_ref, st_ref, y_ref,
                bu0_ref, ut0_ref, bu1_ref, ut1_ref, *, bsz):
    rows = bsz * T_SCAN
    step = pl.program_id(0)

    @pl.when(step == 0)
    def _():
        sre_ref[...] = jnp.zeros_like(sre_ref)
        sim_ref[...] = jnp.zeros_like(sim_ref)
        bu1_ref[...] = jnp.zeros_like(bu1_ref)
        ut1_ref[...] = jnp.zeros_like(ut1_ref)

    def body(write, read):
        bu_w, ut_w = write
        bu_r, ut_r = read

        u_bt = u_ref[...].reshape(rows, D_SSM)
        u_new = jnp.dot(perm_ref[...], u_bt, preferred_element_type=F32).astype(BF16)
        ut_w[...] = u_new
        for j in range(N_SLABS):
            ch = slice(j * SLAB_CH, (j + 1) * SLAB_CH)
            for c in range(SLAB_STATE // SCAN_COLS):
                lo = j * SLAB_STATE + c * SCAN_COLS
                cre = slice(c * SCAN_COLS, (c + 1) * SCAN_COLS)
                cim = slice(SLAB_STATE + c * SCAN_COLS, SLAB_STATE + (c + 1) * SCAN_COLS)
                lr = jnp.broadcast_to(lre_ref[:, lo:lo + SCAN_COLS], (bsz, SCAN_COLS))
                li = jnp.broadcast_to(lim_ref[:, lo:lo + SCAN_COLS], (bsz, SCAN_COLS))
                sr = sre_ref[:, lo:lo + SCAN_COLS]
                si = sim_ref[:, lo:lo + SCAN_COLS]
                for k in range(T_SCAN // T_GROUP):
                    rk = slice(k * T_GROUP * bsz, (k + 1) * T_GROUP * bsz)
                    uk = u_new[rk, ch]
                    bu_w[j, rk, cre] = jnp.dot(uk, bre_ref[j, :, cre], preferred_element_type=F32)
                    bu_w[j, rk, cre.start + SLAB_STATE:cre.stop + SLAB_STATE] = jnp.dot(
                        uk, bim_ref[j, :, cre], preferred_element_type=F32)
                    for t in range(k * T_GROUP, (k + 1) * T_GROUP):
                        r = slice(t * bsz, (t + 1) * bsz)
                        sr, si = (lr * sr - li * si + bu_r[j, r, cre],
                                  lr * si + li * sr + bu_r[j, r, cim])
                        st_ref[j, r, cre] = sr.astype(BF16)
                        st_ref[j, r, cim] = si.astype(BF16)
                    yk = (jnp.dot(st_ref[j, rk, cre], cre_ref[j, cre, :], preferred_element_type=F32)
                          + jnp.dot(st_ref[j, rk, cim], cim_ref[j, cre, :],
                                    preferred_element_type=F32))
                    if c == 0:
                        y_ref[rk, ch] = yk
                    else:
                        y_ref[rk, ch] += yk
                sre_ref[:, lo:lo + SCAN_COLS] = sr
                sim_ref[:, lo:lo + SCAN_COLS] = si

        y = y_ref[...] + d_ref[...] * ut_r[...].astype(F32)
        g = jax.nn.gelu(y)
        gate = jnp.dot(g.astype(BF16), wglu_ref[...], preferred_element_type=F32) + bglu_ref[...]
        out_tb = (g * jax.nn.sigmoid(gate)).astype(BF16)
        out_bt = jnp.dot(permt_ref[...], out_tb, preferred_element_type=F32).astype(BF16)
        o_ref[...] = out_bt.reshape(bsz, T_SCAN, D_SSM)

    _by_parity(step, body, (bu0_ref, ut0_ref), (bu1_ref, ut1_ref))


def _row_permutation(bsz):
    rows = bsz * T_SCAN
    perm = np.zeros((rows, rows), np.float32)
    t, b = np.meshgrid(np.arange(T_SCAN), np.arange(bsz), indexing="ij")
    perm[(t * bsz + b).ravel(), (b * T_SCAN + t).ravel()] = 1.0
    return perm


def _ssm(u, bre, bim, lre, lim, cre, cim_neg, d, wglu, bglu):
    bsz, seq, _ = u.shape
    rows = T_SCAN * bsz
    nblk = seq // T_SCAN
    perm = _row_permutation(bsz)
    full = lambda shape: pl.BlockSpec(shape, lambda f: (0,) * len(shape))
    bu_scratch = pltpu.VMEM((N_SLABS, rows, 2 * SLAB_STATE), F32)
    ut_scratch = pltpu.VMEM((rows, D_SSM), BF16)
    return pl.pallas_call(
        functools.partial(_ssm_kernel, bsz=bsz),
        grid=(nblk + 1,),
        in_specs=[
            pl.BlockSpec((bsz, T_SCAN, D_SSM), lambda f: (0, jnp.minimum(f, nblk - 1), 0)),
            full((rows, rows)), full((rows, rows)),
            full((N_SLABS, SLAB_CH, SLAB_STATE)), full((N_SLABS, SLAB_CH, SLAB_STATE)),
            full((1, N_STATE)), full((1, N_STATE)),
            full((N_SLABS, SLAB_STATE, SLAB_CH)), full((N_SLABS, SLAB_STATE, SLAB_CH)),
            full((1, D_SSM)), full((D_SSM, D_SSM)), full((1, D_SSM)),
        ],
        out_specs=pl.BlockSpec((bsz, T_SCAN, D_SSM), lambda f: (0, jnp.maximum(f - 1, 0), 0)),
        out_shape=jax.ShapeDtypeStruct((bsz, seq, D_SSM), BF16),
        scratch_shapes=[
            pltpu.VMEM((bsz, N_STATE), F32),
            pltpu.VMEM((bsz, N_STATE), F32),
            pltpu.VMEM((N_SLABS, rows, 2 * SLAB_STATE), BF16),
            pltpu.VMEM((rows, D_SSM), F32),
            bu_scratch, ut_scratch, bu_scratch, ut_scratch,
        ],
        compiler_params=pltpu.CompilerParams(
            dimension_semantics=("arbitrary",), vmem_limit_bytes=VMEM_LIMIT),
        name="s5_scan",
    )(u, jnp.asarray(perm, BF16), jnp.asarray(perm.T, BF16),
      bre, bim, lre, lim, cre, cim_neg, d, wglu, bglu)


def _attention_pieces(first, sink_ref, q_ref, k_ref, v_ref, kp_ref, vp_ref, za_ref, bias_ref,
                      half_ref, a_ref):
    low_q = lax.broadcasted_iota(jnp.int32, (WINDOW, LANES), 1) < HEAD_DIM
    kv_cache = {}

    def swap_halves(t):
        packed = pltpu.bitcast(t, jnp.uint32)
        return pltpu.bitcast(pltpu.roll(packed, HEAD_DIM, axis=1), BF16)

    def keys_values(jj, g):
        if jj not in kv_cache:
            rows = slice(jj * WINDOW, (jj + 1) * WINDOW)
            if jj == 0:
                kp, vp = kp_ref[...], vp_ref[...]
            else:
                prev = slice((jj - 1) * WINDOW, jj * WINDOW)
                kp, vp = k_ref[prev, :], v_ref[prev, :]
            kk = jnp.concatenate([kp, k_ref[rows, :]], axis=0)
            vv = jnp.concatenate([vp, v_ref[rows, :]], axis=0)
            kv_cache[jj] = ((kk, swap_halves(kk)), (vv, swap_halves(vv)))
        if (jj, g) not in kv_cache:
            keep_even, keep_odd = half_ref[0], half_ref[1]
            k2, v2 = kv_cache[jj]
            kcat = jnp.concatenate([k2[g] * keep_even, k2[1 - g] * keep_odd], axis=0)
            vcat = jnp.concatenate(
                [jnp.concatenate([v2[g] * keep_even, keep_even], axis=1),
                 jnp.concatenate([v2[1 - g] * keep_odd, keep_odd], axis=1)], axis=0)
            kv_cache[(jj, g)] = (kcat, vcat)
        return kv_cache[(jj, g)]

    items = [(jj, pair) for jj in range(TOK_OUT // WINDOW) for pair in range(N_HEADS // 2)]
    scores, probs = {}, {}

    def where(i):
        jj, pair = items[i]
        return (jj, pair, slice(jj * WINDOW, (jj + 1) * WINDOW),
                slice(pair * LANES, (pair + 1) * LANES))

    def stage_scores(i):
        jj, pair, rows, lanes = where(i)
        kcat, _ = keys_values(jj, pair // (Q_PER_KV // 2))
        s = lax.dot_general(q_ref[rows, lanes], kcat, (((1,), (1,)), ((), ())),
                            preferred_element_type=F32)
        scores[i] = s + (bias_ref[pair + (N_HEADS // 2) * first] if jj == 0 else bias_ref[pair])

    def stage_softmax(i):
        _, pair, _, _ = where(i)
        s = scores.pop(i)
        sink_e, sink_o = sink_ref[2 * pair], sink_ref[2 * pair + 1]
        m_e = jnp.maximum(jnp.max(s[:, :2 * WINDOW], axis=-1, keepdims=True), sink_e)
        m_o = jnp.maximum(jnp.max(s[:, 2 * WINDOW:], axis=-1, keepdims=True), sink_o)
        e = jnp.concatenate([jnp.exp(s[:, :2 * WINDOW] - m_e),
                             jnp.exp(s[:, 2 * WINDOW:] - m_o)], axis=1).astype(BF16)
        probs[i] = (e, jnp.where(low_q, jnp.exp(sink_e - m_e), jnp.exp(sink_o - m_o)))

    def stage_output(i):
        jj, pair, rows, lanes = where(i)
        _, vcat = keys_values(jj, pair // (Q_PER_KV // 2))
        e, sink_term = probs.pop(i)
        o = jnp.dot(e, vcat, preferred_element_type=F32)
        den = o[:, LANES:] + sink_term
        a_ref[rows, lanes] = (o[:, :LANES] / den * za_ref[rows, lanes].astype(F32)).astype(BF16)

    def slot(it):
        if it < len(items):
            stage_scores(it)
        if 0 <= it - 1 < len(items):
            stage_softmax(it - 1)
        if 0 <= it - 2 < len(items):
            stage_output(it - 2)

    return [functools.partial(slot, it) for it in range(len(items) + 2)]


def _attn_out_kernel(sink_ref, x_ref, p_ref, ssm_ref, zs_ref, q_ref, k_ref, v_ref, kp_ref, vp_ref,
                     za_ref, bias_ref, half_ref, wout_f32, pg_ref, wgate_f32, bgate_ref, wproj_f32,
                     o_ref, wout_ref, wgate_ref, wproj_ref, a0_ref, a1_ref,
                     *, blocks_per_seq, n_blocks):
    step = pl.program_id(0)
    seq_block = jnp.minimum(step, n_blocks - 1) % blocks_per_seq
    first = (seq_block == 0).astype(jnp.int32)

    @pl.when(step == 0)
    def _():
        wout_ref[...] = wout_f32[...].astype(BF16)
        wgate_ref[...] = wgate_f32[...].astype(BF16)
        wproj_ref[...] = wproj_f32[...].astype(BF16)
        a1_ref[...] = jnp.zeros_like(a1_ref)

    def body(a_w, a_r):
        pieces = _attention_pieces(first, sink_ref, q_ref, k_ref, v_ref, kp_ref, vp_ref, za_ref,
                                   bias_ref, half_ref, a_w)
        n_chunks = D_MODEL // PROJ_COLS
        per_chunk = (len(pieces) - 2) // (2 * n_chunks)
        pieces = iter(pieces)

        def run_pieces(count=per_chunk):
            for piece in (list(pieces) if count is None else [p for _, p in zip(range(count), pieces)]):
                piece()

        run_pieces(2)

        s5 = (ssm_ref[...].astype(F32) * zs_ref[...].astype(F32)).astype(BF16)
        a_prev = a_r[...]
        mixed = []
        for n in range(n_chunks):
            cols = slice(n * PROJ_COLS, (n + 1) * PROJ_COLS)
            mixed.append(jnp.dot(s5, wout_ref[:D_SSM, cols], preferred_element_type=F32)
                         + jnp.dot(a_prev, wout_ref[D_SSM:, cols], preferred_element_type=F32))
            run_pieces()
        mixed = jnp.concatenate(mixed, axis=1)
        ms = jnp.mean(mixed * mixed, axis=-1, keepdims=True)
        h1 = x_ref[...] + mixed * lax.rsqrt(ms + EPS) * pg_ref[...]
        h1b = h1.astype(BF16)
        pb = p_ref[...].astype(BF16)
        for n in range(n_chunks):
            cols = slice(n * PROJ_COLS, (n + 1) * PROJ_COLS)
            gate = jax.nn.sigmoid(
                jnp.dot(h1b, wgate_ref[:, cols], preferred_element_type=F32) + bgate_ref[:, cols])
            ple = jnp.dot(pb, wproj_ref[:, cols], preferred_element_type=F32)
            o_ref[:, cols] = h1[:, cols] + gate * ple
            run_pieces()
        run_pieces(None)

    _by_parity(step, body, a0_ref, a1_ref)


def _attn_bias_table():
    q_idx = np.arange(WINDOW)[:, None]
    s_idx = np.arange(2 * WINDOW)[None, :]
    dist = q_idx + WINDOW - s_idx
    valid = (dist >= 0) & (dist < WINDOW)
    slopes = np.exp2(-8.0 * (np.arange(N_HEADS, dtype=np.float32) + 1.0) / N_HEADS).astype(np.float32)
    bias = -slopes[:, None, None] * dist.astype(np.float32)[None]
    normal = np.where(valid[None], bias, -np.inf)
    first = np.where((valid & (s_idx >= WINDOW))[None], bias, -np.inf)
    paired = lambda t: t.reshape(N_HEADS // 2, 2, WINDOW, 2 * WINDOW).transpose(0, 2, 1, 3).reshape(
        N_HEADS // 2, WINDOW, 4 * WINDOW)
    return np.concatenate([paired(normal), paired(first)], axis=0).astype(np.float32)


def _attn_out(sinks, x, p, ssm, zs, q, k, v, za, wout, pg, wgate, bgate, wproj):
    bsz, seq, _ = x.shape
    sub = TOK_OUT // WINDOW
    per_seq = seq // TOK_OUT
    n_blocks = bsz * per_seq

    def cur(f):
        f = jnp.minimum(f, n_blocks - 1)
        return f // per_seq, f % per_seq

    def prv(f):
        f = jnp.maximum(f - 1, 0)
        return f // per_seq, f % per_seq

    front = lambda width: pl.BlockSpec((None, TOK_OUT, width), lambda f: (*cur(f), 0))
    back = lambda width: pl.BlockSpec((None, TOK_OUT, width), lambda f: (*prv(f), 0))
    prev_kv = pl.BlockSpec(
        (None, WINDOW, D_KV),
        lambda f: (cur(f)[0], jnp.maximum(cur(f)[1] * sub - 1, 0), 0))
    full = lambda shape: pl.BlockSpec(shape, lambda f: (0,) * len(shape))
    once = lambda shape: pl.BlockSpec(shape, lambda f: (0,) * len(shape),
                                      pipeline_mode=pl.Buffered(1))
    bias = jnp.asarray(_attn_bias_table())
    lane_is_even = np.broadcast_to(np.arange(LANES) < HEAD_DIM, (2 * WINDOW, LANES))
    half = jnp.asarray(np.stack([lane_is_even, ~lane_is_even]), BF16)
    a_scratch = pltpu.VMEM((TOK_OUT, D_ATTN), BF16)
    return pl.pallas_call(
        functools.partial(_attn_out_kernel, blocks_per_seq=per_seq, n_blocks=n_blocks),
        grid=(n_blocks + 1,),
        in_specs=[
            pl.BlockSpec(memory_space=pltpu.SMEM),
            back(D_MODEL), back(D_PLE), back(D_SSM), back(D_SSM), front(D_ATTN),
            front(D_KV), front(D_KV), prev_kv, prev_kv, front(D_ATTN),
            once(bias.shape), once(half.shape),
            once((D_MODEL, D_MODEL)), full((1, D_MODEL)),
            once((D_MODEL, D_MODEL)), full((1, D_MODEL)), once((D_PLE, D_MODEL)),
        ],
        out_specs=back(D_MODEL),
        out_shape=jax.ShapeDtypeStruct((bsz, seq, D_MODEL), F32),
        scratch_shapes=[pltpu.VMEM((D_MODEL, D_MODEL), BF16), pltpu.VMEM((D_MODEL, D_MODEL), BF16),
                        pltpu.VMEM((D_PLE, D_MODEL), BF16), a_scratch, a_scratch],
        compiler_params=pltpu.CompilerParams(
            dimension_semantics=("arbitrary",), vmem_limit_bytes=VMEM_LIMIT),
        name="attn_out",
    )(sinks, x, p, ssm, zs, q, k, v, k, v, za, bias, half, wout, pg, wgate, bgate, wproj)


def _block_diag(blocks):
    s, n, a, b = blocks.shape
    eye = jnp.eye(n, dtype=blocks.dtype)
    return jnp.einsum('snab,nm->snamb', blocks, eye).reshape(s, n * a, n * b)


def _ssm_params(lam_re, lam_im, log_step, b_re, b_im, c_re, c_im):
    lr, li = lam_re.astype(F32), lam_im.astype(F32)
    step = jnp.exp(log_step.astype(F32))[:, None]
    mag = jnp.exp(lr * step)
    lbr, lbi = mag * jnp.cos(li * step), mag * jnp.sin(li * step)
    den = lr * lr + li * li
    fr = ((lbr - 1.0) * lr + lbi * li) / den
    fi = (lbi * lr - (lbr - 1.0) * li) / den
    br, bi = b_re.astype(F32), b_im.astype(F32)
    bbr = fr[..., None] * br - fi[..., None] * bi
    bbi = fr[..., None] * bi + fi[..., None] * br
    gps = SLAB_CH // SSM_GROUP_CH

    def in_w(part):
        blocks = part.reshape(N_SLABS, gps, SSM_STATE, SSM_GROUP_CH).transpose(0, 1, 3, 2)
        return _block_diag(blocks).astype(BF16)

    def out_w(part):
        blocks = part.reshape(N_SLABS, gps, SSM_GROUP_CH, SSM_STATE).transpose(0, 1, 3, 2)
        return _block_diag(blocks).astype(BF16)

    return (in_w(bbr), in_w(bbi), lbr.reshape(1, N_STATE), lbi.reshape(1, N_STATE),
            out_w(c_re.astype(F32)), out_w(-c_im.astype(F32)))


def _layer(h, p, pre_norm_g, w_in, ssm_lam_re, ssm_lam_im, ssm_log_step, ssm_b_re, ssm_b_im,
           ssm_c_re, ssm_c_im, ssm_d, ssm_w_glu, ssm_b_glu, attn_sinks, w_out, post_norm_g,
           pl_w_proj, pl_w_gate, pl_b_gate):
    row = lambda a: a.astype(F32).reshape(1, -1)
    u, zs, q, k, v, za = _in_proj(h, row(pre_norm_g), w_in.astype(F32))
    bre, bim, lre, lim, cre, cim_neg = _ssm_params(
        ssm_lam_re, ssm_lam_im, ssm_log_step, ssm_b_re, ssm_b_im, ssm_c_re, ssm_c_im)
    ssm = _ssm(u, bre, bim, lre, lim, cre, cim_neg,
               row(ssm_d), ssm_w_glu.astype(BF16), row(ssm_b_glu))
    return _attn_out(attn_sinks.astype(F32), h, p, ssm, zs, q, k, v, za,
                     w_out.astype(F32), row(post_norm_g), pl_w_gate.astype(F32), row(pl_b_gate),
                     pl_w_proj.astype(F32))


def kernel(x, p, pre_norm_g, w_in, ssm_lam_re, ssm_lam_im, ssm_log_step, ssm_b_re, ssm_b_im,
           ssm_c_re, ssm_c_im, ssm_d, ssm_w_glu, ssm_b_glu, attn_sinks, w_out, post_norm_g,
           pl_w_proj, pl_w_gate, pl_b_gate):
    h = x
    for i in range(p.shape[0]):
        h = _layer(h, p[i], pre_norm_g[i], w_in[i], ssm_lam_re[i], ssm_lam_im[i], ssm_log_step[i],
                   ssm_b_re[i], ssm_b_im[i], ssm_c_re[i], ssm_c_im[i], ssm_d[i], ssm_w_glu[i],
                   ssm_b_glu[i], attn_sinks[i], w_out[i], post_norm_g[i], pl_w_proj[i],
                   pl_w_gate[i], pl_b_gate[i])
    return h
```

```python
import functools
import math

import numpy as np
import jax
import jax.numpy as jnp
from jax import lax
from jax.experimental import pallas as pl
from jax.experimental.pallas import tpu as pltpu

F32 = jnp.float32
BF16 = jnp.bfloat16

D_MODEL = 1024
D_SSM = 512
D_ATTN = 512
SSM_GROUP_CH = 16
SSM_GROUPS = D_SSM // SSM_GROUP_CH
SSM_STATE = 64
N_STATE = SSM_GROUPS * SSM_STATE
HEAD_DIM = 64
N_HEADS = D_ATTN // HEAD_DIM
KV_HEADS = 2
Q_PER_KV = N_HEADS // KV_HEADS
D_KV = KV_HEADS * HEAD_DIM
WINDOW = 128
D_PLE = 256
EPS = 1e-6

LANES = 128
assert D_KV == LANES and 2 * HEAD_DIM == LANES

OFF_U, OFF_ZS, OFF_Q, OFF_K, OFF_V, OFF_ZA, D_IN = 0, 512, 1024, 1536, 1664, 1792, 2304

SLAB_CH = LANES
N_SLABS = D_SSM // SLAB_CH
SLAB_STATE = (SLAB_CH // SSM_GROUP_CH) * SSM_STATE
SCAN_COLS = 256

TOK_IN = 1024
NORM_ROWS = 512
T_SCAN = 32
PIECE_ROWS = 128
TOK_OUT = 512
PROJ_COLS = 256
VMEM_LIMIT = 48 * 1024 * 1024


def _silu(z):
    return z * jax.nn.sigmoid(z)


def _by_parity(step, body, even_bufs, odd_bufs):
    @pl.when(step % 2 == 0)
    def _():
        body(even_bufs, odd_bufs)

    @pl.when(step % 2 == 1)
    def _():
        body(odd_bufs, even_bufs)


def _in_proj_kernel(x_ref, g_ref, w_ref, u_ref, zs_ref, q_ref, k_ref, v_ref, za_ref, w_scr):
    @pl.when((pl.program_id(0) == 0) & (pl.program_id(1) == 0))
    def _():
        w_scr[...] = w_ref[...].astype(BF16)

    def emit(rows, tile, acc):
        lo = tile * PROJ_COLS
        if lo < OFF_ZS:
            u_ref[rows, lo - OFF_U:lo - OFF_U + PROJ_COLS] = acc.astype(BF16)
        elif lo < OFF_Q:
            zs_ref[rows, lo - OFF_ZS:lo - OFF_ZS + PROJ_COLS] = _silu(acc).astype(BF16)
        elif lo < OFF_K:
            q_ref[rows, lo - OFF_Q:lo - OFF_Q + PROJ_COLS] = (
                acc * (1.0 / math.sqrt(HEAD_DIM))).astype(BF16)
        elif lo < OFF_ZA:
            k_ref[rows, :] = acc[:, :D_KV].astype(BF16)
            v_ref[rows, :] = acc[:, D_KV:].astype(BF16)
        else:
            za_ref[rows, lo - OFF_ZA:lo - OFF_ZA + PROJ_COLS] = _silu(acc).astype(BF16)

    gain = g_ref[...]
    for half in range(TOK_IN // NORM_ROWS):
        rows = slice(half * NORM_ROWS, (half + 1) * NORM_ROWS)
        x = x_ref[rows, :]
        ms = jnp.mean(x * x, axis=-1, keepdims=True)
        hn = (x * lax.rsqrt(ms + EPS) * gain).astype(BF16)
        for tile in range(D_IN // PROJ_COLS):
            cols = slice(tile * PROJ_COLS, (tile + 1) * PROJ_COLS)
            emit(rows, tile, jnp.dot(hn, w_scr[:, cols], preferred_element_type=F32))


def _in_proj(x, g, w_in):
    bsz, seq, _ = x.shape
    tok = lambda width: pl.BlockSpec((None, TOK_IN, width), lambda b, i: (b, i, 0))
    widths = (D_SSM, D_SSM, D_ATTN, D_KV, D_KV, D_ATTN)
    return pl.pallas_call(
        _in_proj_kernel,
        grid=(bsz, seq // TOK_IN),
        in_specs=[
            tok(D_MODEL),
            pl.BlockSpec((1, D_MODEL), lambda b, i: (0, 0)),
            pl.BlockSpec((D_MODEL, D_IN), lambda b, i: (0, 0), pipeline_mode=pl.Buffered(1)),
        ],
        out_specs=tuple(tok(w) for w in widths),
        out_shape=tuple(jax.ShapeDtypeStruct((bsz, seq, w), BF16) for w in widths),
        scratch_shapes=[pltpu.VMEM((D_MODEL, D_IN), BF16)],
        compiler_params=pltpu.CompilerParams(
            dimension_semantics=("arbitrary", "arbitrary"), vmem_limit_bytes=VMEM_LIMIT),
        name="in_proj",
    )(x, g, w_in)


def _ssm_kernel(u_ref, perm_ref, permt_ref, win_re_ref, win_im_ref, l2re_ref, l2im_ref,
                wout_re_ref, wout_im_ref, wdir_ref, d_ref, wglu_ref, bglu_ref, o_ref,
                sre_ref, sim_ref, uprev_ref, st_ref, y_ref,
                g0_ref, ut0_ref, g1_ref, ut1_ref, *, bsz):
    rows = bsz * T_SCAN
    half = rows // 2
    step = pl.program_id(0)

    @pl.when(step == 0)
    def _():
        sre_ref[...] = jnp.zeros_like(sre_ref)
        sim_ref[...] = jnp.zeros_like(sim_ref)
        uprev_ref[...] = jnp.zeros_like(uprev_ref)
        g1_ref[...] = jnp.zeros_like(g1_ref)
        ut1_ref[...] = jnp.zeros_like(ut1_ref)

    def body(write, read):
        g_w, ut_w = write
        g_r, ut_r = read

        u_bt = u_ref[...].reshape(rows, D_SSM)
        u_new = jnp.dot(perm_ref[...], u_bt, preferred_element_type=F32).astype(BF16)
        ut_w[...] = u_new
        u_before = jnp.concatenate([uprev_ref[...], u_new[half:rows - bsz]], axis=0)
        uprev_ref[...] = u_new[rows - bsz:]
        for j in range(N_SLABS):
            ch = slice(j * SLAB_CH, (j + 1) * SLAB_CH)
            drive_in = jnp.concatenate([u_before[:, ch], u_new[:half, ch]], axis=1)
            for c in range(SLAB_STATE // SCAN_COLS):
                lo = j * SLAB_STATE + c * SCAN_COLS
                cre = slice(c * SCAN_COLS, (c + 1) * SCAN_COLS)
                cim = slice(SLAB_STATE + c * SCAN_COLS, SLAB_STATE + (c + 1) * SCAN_COLS)
                l2r = jnp.broadcast_to(l2re_ref[:, lo:lo + SCAN_COLS], (bsz, SCAN_COLS))
                l2i = jnp.broadcast_to(l2im_ref[:, lo:lo + SCAN_COLS], (bsz, SCAN_COLS))
                sr = sre_ref[:, lo:lo + SCAN_COLS]
                si = sim_ref[:, lo:lo + SCAN_COLS]
                for piece in range(half // PIECE_ROWS):
                    rk = slice(piece * PIECE_ROWS, (piece + 1) * PIECE_ROWS)
                    g_w[j, rk, cre] = jnp.dot(drive_in[rk], win_re_ref[j, :, cre],
                                              preferred_element_type=F32)
                    g_w[j, rk, cim] = jnp.dot(drive_in[rk], win_im_ref[j, :, cre],
                                              preferred_element_type=F32)
                    for k in range(rk.start // bsz, rk.stop // bsz):
                        r = slice(k * bsz, (k + 1) * bsz)
                        sr, si = (l2r * sr - l2i * si + g_r[j, r, cre],
                                  l2r * si + l2i * sr + g_r[j, r, cim])
                        st_ref[j, r, cre] = sr.astype(BF16)
                        st_ref[j, r, cim] = si.astype(BF16)
                    yk = (jnp.dot(st_ref[j, rk, cre], wout_re_ref[j, cre, :],
                                  preferred_element_type=F32)
                          + jnp.dot(st_ref[j, rk, cim], wout_im_ref[j, cre, :],
                                    preferred_element_type=F32))
                    odd_rows = slice(half + rk.start, half + rk.stop)
                    if c == 0:
                        y_ref[rk, ch] = yk[:, :SLAB_CH]
                        y_ref[odd_rows, ch] = yk[:, SLAB_CH:] + jnp.dot(
                            ut_r[odd_rows, ch], wdir_ref[j], preferred_element_type=F32)
                    else:
                        y_ref[rk, ch] += yk[:, :SLAB_CH]
                        y_ref[odd_rows, ch] += yk[:, SLAB_CH:]
                sre_ref[:, lo:lo + SCAN_COLS] = sr
                sim_ref[:, lo:lo + SCAN_COLS] = si

        y = y_ref[...] + d_ref[...] * ut_r[...].astype(F32)
        g = jax.nn.gelu(y)
        gate = jnp.dot(g.astype(BF16), wglu_ref[...], preferred_element_type=F32) + bglu_ref[...]
        out_tb = (g * jax.nn.sigmoid(gate)).astype(BF16)
        out_bt = jnp.dot(permt_ref[...], out_tb, preferred_element_type=F32).astype(BF16)
        o_ref[...] = out_bt.reshape(bsz, T_SCAN, D_SSM)

    _by_parity(step, body, (g0_ref, ut0_ref), (g1_ref, ut1_ref))


def _row_permutation(bsz):
    rows = bsz * T_SCAN
    perm = np.zeros((rows, rows), np.float32)
    t, b = np.meshgrid(np.arange(T_SCAN), np.arange(bsz), indexing="ij")
    dst = (t % 2) * (rows // 2) + (t // 2) * bsz + b
    perm[dst.ravel(), (b * T_SCAN + t).ravel()] = 1.0
    return perm


def _ssm(u, params, d, wglu, bglu):
    bsz, seq, _ = u.shape
    rows = T_SCAN * bsz
    half = rows // 2
    nblk = seq // T_SCAN
    perm = _row_permutation(bsz)
    full = lambda shape: pl.BlockSpec(shape, lambda f: (0,) * len(shape))
    g_scratch = pltpu.VMEM((N_SLABS, half, 2 * SLAB_STATE), F32)
    ut_scratch = pltpu.VMEM((rows, D_SSM), BF16)
    return pl.pallas_call(
        functools.partial(_ssm_kernel, bsz=bsz),
        grid=(nblk + 1,),
        in_specs=[
            pl.BlockSpec((bsz, T_SCAN, D_SSM), lambda f: (0, jnp.minimum(f, nblk - 1), 0)),
            full((rows, rows)), full((rows, rows)),
            full((N_SLABS, 2 * SLAB_CH, SLAB_STATE)), full((N_SLABS, 2 * SLAB_CH, SLAB_STATE)),
            full((1, N_STATE)), full((1, N_STATE)),
            full((N_SLABS, SLAB_STATE, 2 * SLAB_CH)), full((N_SLABS, SLAB_STATE, 2 * SLAB_CH)),
            full((N_SLABS, SLAB_CH, SLAB_CH)),
            full((1, D_SSM)), full((D_SSM, D_SSM)), full((1, D_SSM)),
        ],
        out_specs=pl.BlockSpec((bsz, T_SCAN, D_SSM), lambda f: (0, jnp.maximum(f - 1, 0), 0)),
        out_shape=jax.ShapeDtypeStruct((bsz, seq, D_SSM), BF16),
        scratch_shapes=[
            pltpu.VMEM((bsz, N_STATE), F32),
            pltpu.VMEM((bsz, N_STATE), F32),
            pltpu.VMEM((bsz, D_SSM), BF16),
            pltpu.VMEM((N_SLABS, half, 2 * SLAB_STATE), BF16),
            pltpu.VMEM((rows, D_SSM), F32),
            g_scratch, ut_scratch, g_scratch, ut_scratch,
        ],
        compiler_params=pltpu.CompilerParams(
            dimension_semantics=("arbitrary",), vmem_limit_bytes=VMEM_LIMIT),
        name="s5_scan",
    )(u, jnp.asarray(perm, BF16), jnp.asarray(perm.T, BF16), *params, d, wglu, bglu)


def _attention_pieces(first, sink_ref, q_ref, k_ref, v_ref, kp_ref, vp_ref, za_ref, bias_ref,
                      half_ref, a_ref):
    low_q = lax.broadcasted_iota(jnp.int32, (WINDOW, LANES), 1) < HEAD_DIM
    kv_cache = {}

    def swap_halves(t):
        packed = pltpu.bitcast(t, jnp.uint32)
        return pltpu.bitcast(pltpu.roll(packed, HEAD_DIM, axis=1), BF16)

    def keys_values(jj, g):
        if jj not in kv_cache:
            rows = slice(jj * WINDOW, (jj + 1) * WINDOW)
            if jj == 0:
                kp, vp = kp_ref[...], vp_ref[...]
            else:
                prev = slice((jj - 1) * WINDOW, jj * WINDOW)
                kp, vp = k_ref[prev, :], v_ref[prev, :]
            kk = jnp.concatenate([kp, k_ref[rows, :]], axis=0)
            vv = jnp.concatenate([vp, v_ref[rows, :]], axis=0)
            kv_cache[jj] = ((kk, swap_halves(kk)), (vv, swap_halves(vv)))
        if (jj, g) not in kv_cache:
            keep_even, keep_odd = half_ref[0], half_ref[1]
            k2, v2 = kv_cache[jj]
            kcat = jnp.concatenate([k2[g] * keep_even, k2[1 - g] * keep_odd], axis=0)
            vcat = jnp.concatenate(
                [jnp.concatenate([v2[g] * keep_even, keep_even], axis=1),
                 jnp.concatenate([v2[1 - g] * keep_odd, keep_odd], axis=1)], axis=0)
            kv_cache[(jj, g)] = (kcat, vcat)
        return kv_cache[(jj, g)]

    items = [(jj, pair) for jj in range(TOK_OUT // WINDOW) for pair in range(N_HEADS // 2)]
    scores, probs = {}, {}

    def where(i):
        jj, pair = items[i]
        return (jj, pair, slice(jj * WINDOW, (jj + 1) * WINDOW),
                slice(pair * LANES, (pair + 1) * LANES))

    def stage_scores(i):
        jj, pair, rows, lanes = where(i)
        kcat, _ = keys_values(jj, pair // (Q_PER_KV // 2))
        s = lax.dot_general(q_ref[rows, lanes], kcat, (((1,), (1,)), ((), ())),
                            preferred_element_type=F32)
        scores[i] = s + (bias_ref[pair + (N_HEADS // 2) * first] if jj == 0 else bias_ref[pair])

    def stage_softmax(i):
        _, pair, _, _ = where(i)
        s = scores.pop(i)
        sink_e, sink_o = sink_ref[2 * pair], sink_ref[2 * pair + 1]
        m_e = jnp.maximum(jnp.max(s[:, :2 * WINDOW], axis=-1, keepdims=True), sink_e)
        m_o = jnp.maximum(jnp.max(s[:, 2 * WINDOW:], axis=-1, keepdims=True), sink_o)
        e = jnp.concatenate([jnp.exp(s[:, :2 * WINDOW] - m_e),
                             jnp.exp(s[:, 2 * WINDOW:] - m_o)], axis=1).astype(BF16)
        probs[i] = (e, jnp.where(low_q, jnp.exp(sink_e - m_e), jnp.exp(sink_o - m_o)))

    def stage_output(i):
        jj, pair, rows, lanes = where(i)
        _, vcat = keys_values(jj, pair // (Q_PER_KV // 2))
        e, sink_term = probs.pop(i)
        o = jnp.dot(e, vcat, preferred_element_type=F32)
        den = o[:, LANES:] + sink_term
        a_ref[rows, lanes] = (o[:, :LANES] / den * za_ref[rows, lanes].astype(F32)).astype(BF16)

    def slot(it):
        if it < len(items):
            stage_scores(it)
        if 0 <= it - 1 < len(items):
            stage_softmax(it - 1)
        if 0 <= it - 2 < len(items):
            stage_output(it - 2)

    return [functools.partial(slot, it) for it in range(len(items) + 2)]


def _attn_out_kernel(sink_ref, x_ref, p_ref, ssm_ref, zs_ref, q_ref, k_ref, v_ref, kp_ref, vp_ref,
                     za_ref, bias_ref, half_ref, wout_f32, pg_ref, wgate_f32, bgate_ref, wproj_f32,
                     o_ref, wout_ref, wgate_ref, wproj_ref, a0_ref, a1_ref,
                     *, blocks_per_seq, n_blocks):
    step = pl.program_id(0)
    seq_block = jnp.minimum(step, n_blocks - 1) % blocks_per_seq
    first = (seq_block == 0).astype(jnp.int32)

    @pl.when(step == 0)
    def _():
        wout_ref[...] = wout_f32[...].astype(BF16)
        wgate_ref[...] = wgate_f32[...].astype(BF16)
        wproj_ref[...] = wproj_f32[...].astype(BF16)
        a1_ref[...] = jnp.zeros_like(a1_ref)

    def body(a_w, a_r):
        pieces = _attention_pieces(first, sink_ref, q_ref, k_ref, v_ref, kp_ref, vp_ref, za_ref,
                                   bias_ref, half_ref, a_w)
        n_chunks = D_MODEL // PROJ_COLS
        per_chunk = (len(pieces) - 2) // (2 * n_chunks)
        pieces = iter(pieces)

        def run_pieces(count=per_chunk):
            for piece in (list(pieces) if count is None else [p for _, p in zip(range(count), pieces)]):
                piece()

        run_pieces(2)

        s5 = (ssm_ref[...].astype(F32) * zs_ref[...].astype(F32)).astype(BF16)
        a_prev = a_r[...]
        mixed = []
        for n in range(n_chunks):
            cols = slice(n * PROJ_COLS, (n + 1) * PROJ_COLS)
            mixed.append(jnp.dot(s5, wout_ref[:D_SSM, cols], preferred_element_type=F32)
                         + jnp.dot(a_prev, wout_ref[D_SSM:, cols], preferred_element_type=F32))
            run_pieces()
        mixed = jnp.concatenate(mixed, axis=1)
        ms = jnp.mean(mixed * mixed, axis=-1, keepdims=True)
        h1 = x_ref[...] + mixed * lax.rsqrt(ms + EPS) * pg_ref[...]
        h1b = h1.astype(BF16)
        pb = p_ref[...].astype(BF16)
        for n in range(n_chunks):
            cols = slice(n * PROJ_COLS, (n + 1) * PROJ_COLS)
            gate = jax.nn.sigmoid(
                jnp.dot(h1b, wgate_ref[:, cols], preferred_element_type=F32) + bgate_ref[:, cols])
            ple = jnp.dot(pb, wproj_ref[:, cols], preferred_element_type=F32)
            o_ref[:, cols] = h1[:, cols] + gate * ple
            run_pieces()
        run_pieces(None)

    _by_parity(step, body, a0_ref, a1_ref)


def _attn_bias_table():
    q_idx = np.arange(WINDOW)[:, None]
    s_idx = np.arange(2 * WINDOW)[None, :]
    dist = q_idx + WINDOW - s_idx
    valid = (dist >= 0) & (dist < WINDOW)
    slopes = np.exp2(-8.0 * (np.arange(N_HEADS, dtype=np.float32) + 1.0) / N_HEADS).astype(np.float32)
    bias = -slopes[:, None, None] * dist.astype(np.float32)[None]
    normal = np.where(valid[None], bias, -np.inf)
    first = np.where((valid & (s_idx >= WINDOW))[None], bias, -np.inf)
    paired = lambda t: t.reshape(N_HEADS // 2, 2, WINDOW, 2 * WINDOW).transpose(0, 2, 1, 3).reshape(
        N_HEADS // 2, WINDOW, 4 * WINDOW)
    return np.concatenate([paired(normal), paired(first)], axis=0).astype(np.float32)


def _attn_out(sinks, x, p, ssm, zs, q, k, v, za, wout, pg, wgate, bgate, wproj):
    bsz, seq, _ = x.shape
    sub = TOK_OUT // WINDOW
    per_seq = seq // TOK_OUT
    n_blocks = bsz * per_seq

    def cur(f):
        f = jnp.minimum(f, n_blocks - 1)
        return f // per_seq, f % per_seq

    def prv(f):
        f = jnp.maximum(f - 1, 0)
        return f // per_seq, f % per_seq

    front = lambda width: pl.BlockSpec((None, TOK_OUT, width), lambda f: (*cur(f), 0))
    back = lambda width: pl.BlockSpec((None, TOK_OUT, width), lambda f: (*prv(f), 0))
    prev_kv = pl.BlockSpec(
        (None, WINDOW, D_KV),
        lambda f: (cur(f)[0], jnp.maximum(cur(f)[1] * sub - 1, 0), 0))
    full = lambda shape: pl.BlockSpec(shape, lambda f: (0,) * len(shape))
    once = lambda shape: pl.BlockSpec(shape, lambda f: (0,) * len(shape),
                                      pipeline_mode=pl.Buffered(1))
    bias = jnp.asarray(_attn_bias_table())
    lane_is_even = np.broadcast_to(np.arange(LANES) < HEAD_DIM, (2 * WINDOW, LANES))
    half = jnp.asarray(np.stack([lane_is_even, ~lane_is_even]), BF16)
    a_scratch = pltpu.VMEM((TOK_OUT, D_ATTN), BF16)
    return pl.pallas_call(
        functools.partial(_attn_out_kernel, blocks_per_seq=per_seq, n_blocks=n_blocks),
        grid=(n_blocks + 1,),
        in_specs=[
            pl.BlockSpec(memory_space=pltpu.SMEM),
            back(D_MODEL), back(D_PLE), back(D_SSM), back(D_SSM), front(D_ATTN),
            front(D_KV), front(D_KV), prev_kv, prev_kv, front(D_ATTN),
            once(bias.shape), once(half.shape),
            once((D_MODEL, D_MODEL)), full((1, D_MODEL)),
            once((D_MODEL, D_MODEL)), full((1, D_MODEL)), once((D_PLE, D_MODEL)),
        ],
        out_specs=back(D_MODEL),
        out_shape=jax.ShapeDtypeStruct((bsz, seq, D_MODEL), F32),
        scratch_shapes=[pltpu.VMEM((D_MODEL, D_MODEL), BF16), pltpu.VMEM((D_MODEL, D_MODEL), BF16),
                        pltpu.VMEM((D_PLE, D_MODEL), BF16), a_scratch, a_scratch],
        compiler_params=pltpu.CompilerParams(
            dimension_semantics=("arbitrary",), vmem_limit_bytes=VMEM_LIMIT),
        name="attn_out",
    )(sinks, x, p, ssm, zs, q, k, v, k, v, za, bias, half, wout, pg, wgate, bgate, wproj)


def _block_diag(blocks):
    s, n, a, b = blocks.shape
    eye = jnp.eye(n, dtype=blocks.dtype)
    return jnp.einsum('snab,nm->snamb', blocks, eye).reshape(s, n * a, n * b)


def _ssm_params(lam_re, lam_im, log_step, b_re, b_im, c_re, c_im):
    lr, li = lam_re.astype(F32), lam_im.astype(F32)
    step = jnp.exp(log_step.astype(F32))[:, None]
    mag = jnp.exp(lr * step)
    lbr, lbi = mag * jnp.cos(li * step), mag * jnp.sin(li * step)
    den = lr * lr + li * li
    fr = ((lbr - 1.0) * lr + lbi * li) / den
    fi = (lbi * lr - (lbr - 1.0) * li) / den
    br, bi = b_re.astype(F32), b_im.astype(F32)
    bbr = fr[..., None] * br - fi[..., None] * bi
    bbi = fr[..., None] * bi + fi[..., None] * br
    cr, ci = c_re.astype(F32), c_im.astype(F32)
    l2r, l2i = lbr * lbr - lbi * lbi, 2.0 * lbr * lbi
    lbbr = lbr[..., None] * bbr - lbi[..., None] * bbi
    lbbi = lbr[..., None] * bbi + lbi[..., None] * bbr
    clr = cr * lbr[:, None, :] - ci * lbi[:, None, :]
    cli = cr * lbi[:, None, :] + ci * lbr[:, None, :]
    cb = jnp.sum(cr[:, :, :, None] * bbr[:, None, :, :] - ci[:, :, :, None] * bbi[:, None, :, :],
                 axis=2)
    gps = SLAB_CH // SSM_GROUP_CH

    def in_w(part):
        blocks = part.reshape(N_SLABS, gps, SSM_STATE, SSM_GROUP_CH).transpose(0, 1, 3, 2)
        return _block_diag(blocks)

    def out_w(part):
        blocks = part.reshape(N_SLABS, gps, SSM_GROUP_CH, SSM_STATE).transpose(0, 1, 3, 2)
        return _block_diag(blocks)

    cat = lambda parts, axis: jnp.concatenate(parts, axis=axis).astype(BF16)
    win_re = cat([in_w(lbbr), in_w(bbr)], 1)
    win_im = cat([in_w(lbbi), in_w(bbi)], 1)
    wout_re = cat([out_w(cr), out_w(clr)], 2)
    wout_im = cat([out_w(-ci), out_w(-cli)], 2)
    wdir = _block_diag(cb.reshape(N_SLABS, gps, SSM_GROUP_CH, SSM_GROUP_CH)
                       .transpose(0, 1, 3, 2)).astype(BF16)
    return (win_re, win_im, l2r.reshape(1, N_STATE), l2i.reshape(1, N_STATE),
            wout_re, wout_im, wdir)


def _layer(h, p, pre_norm_g, w_in, ssm_lam_re, ssm_lam_im, ssm_log_step, ssm_b_re, ssm_b_im,
           ssm_c_re, ssm_c_im, ssm_d, ssm_w_glu, ssm_b_glu, attn_sinks, w_out, post_norm_g,
           pl_w_proj, pl_w_gate, pl_b_gate):
    row = lambda a: a.astype(F32).reshape(1, -1)
    u, zs, q, k, v, za = _in_proj(h, row(pre_norm_g), w_in.astype(F32))
    params = _ssm_params(
        ssm_lam_re, ssm_lam_im, ssm_log_step, ssm_b_re, ssm_b_im, ssm_c_re, ssm_c_im)
    ssm = _ssm(u, params, row(ssm_d), ssm_w_glu.astype(BF16), row(ssm_b_glu))
    return _attn_out(attn_sinks.astype(F32), h, p, ssm, zs, q, k, v, za,
                     w_out.astype(F32), row(post_norm_g), pl_w_gate.astype(F32), row(pl_b_gate),
                     pl_w_proj.astype(F32))


def kernel(x, p, pre_norm_g, w_in, ssm_lam_re, ssm_lam_im, ssm_log_step, ssm_b_re, ssm_b_im,
           ssm_c_re, ssm_c_im, ssm_d, ssm_w_glu, ssm_b_glu, attn_sinks, w_out, post_norm_g,
           pl_w_proj, pl_w_gate, pl_b_gate):
    h = x
    for i in range(p.shape[0]):
        h = _layer(h, p[i], pre_norm_g[i], w_in[i], ssm_lam_re[i], ssm_lam_im[i], ssm_log_step[i],
                   ssm_b_re[i], ssm_b_im[i], ssm_c_re[i], ssm_c_im[i], ssm_d[i], ssm_w_glu[i],
                   ssm_b_glu[i], attn_sinks[i], w_out[i], post_norm_g[i], pl_w_proj[i],
                   pl_w_gate[i], pl_b_gate[i])
    return h
```

```python
import functools
import math

import numpy as np
import jax
import jax.numpy as jnp
from jax import lax
from jax.experimental import pallas as pl
from jax.experimental.pallas import tpu as pltpu

F32 = jnp.float32
BF16 = jnp.bfloat16

D_MODEL = 1024
D_SSM = 512
D_ATTN = 512
SSM_GROUP_CH = 16
SSM_GROUPS = D_SSM // SSM_GROUP_CH
SSM_STATE = 64
N_STATE = SSM_GROUPS * SSM_STATE
HEAD_DIM = 64
N_HEADS = D_ATTN // HEAD_DIM
KV_HEADS = 2
Q_PER_KV = N_HEADS // KV_HEADS
D_KV = KV_HEADS * HEAD_DIM
WINDOW = 128
D_PLE = 256
EPS = 1e-6

LANES = 128
assert D_KV == LANES and 2 * HEAD_DIM == LANES

OFF_U, OFF_ZS, OFF_Q, OFF_K, OFF_V, OFF_ZA, D_IN = 0, 512, 1024, 1536, 1664, 1792, 2304

SLAB_CH = LANES
N_SLABS = D_SSM // SLAB_CH
SLAB_STATE = (SLAB_CH // SSM_GROUP_CH) * SSM_STATE
SCAN_COLS = 256

TOK_IN = 1024
NORM_ROWS = 512
T_SCAN = 32
PIECE_ROWS = 128
TOK_OUT = 512
PROJ_COLS = 256
VMEM_LIMIT = 48 * 1024 * 1024


def _silu(z):
    return z * jax.nn.sigmoid(z)


def _by_parity(step, body, even_bufs, odd_bufs):
    @pl.when(step % 2 == 0)
    def _():
        body(even_bufs, odd_bufs)

    @pl.when(step % 2 == 1)
    def _():
        body(odd_bufs, even_bufs)


def _in_proj_kernel(x_ref, g_ref, w_ref, u_ref, zs_ref, q_ref, k_ref, v_ref, za_ref, w_scr):
    @pl.when((pl.program_id(0) == 0) & (pl.program_id(1) == 0))
    def _():
        w_scr[...] = w_ref[...].astype(BF16)

    def emit(rows, tile, acc):
        lo = tile * PROJ_COLS
        if lo < OFF_ZS:
            u_ref[rows, lo - OFF_U:lo - OFF_U + PROJ_COLS] = acc.astype(BF16)
        elif lo < OFF_Q:
            zs_ref[rows, lo - OFF_ZS:lo - OFF_ZS + PROJ_COLS] = _silu(acc).astype(BF16)
        elif lo < OFF_K:
            q_ref[rows, lo - OFF_Q:lo - OFF_Q + PROJ_COLS] = (
                acc * (1.0 / math.sqrt(HEAD_DIM))).astype(BF16)
        elif lo < OFF_ZA:
            k_ref[rows, :] = acc[:, :D_KV].astype(BF16)
            v_ref[rows, :] = acc[:, D_KV:].astype(BF16)
        else:
            za_ref[rows, lo - OFF_ZA:lo - OFF_ZA + PROJ_COLS] = _silu(acc).astype(BF16)

    gain = g_ref[...]
    for half in range(TOK_IN // NORM_ROWS):
        rows = slice(half * NORM_ROWS, (half + 1) * NORM_ROWS)
        x = x_ref[rows, :]
        ms = jnp.mean(x * x, axis=-1, keepdims=True)
        hn = (x * lax.rsqrt(ms + EPS) * gain).astype(BF16)
        for tile in range(D_IN // PROJ_COLS):
            cols = slice(tile * PROJ_COLS, (tile + 1) * PROJ_COLS)
            emit(rows, tile, jnp.dot(hn, w_scr[:, cols], preferred_element_type=F32))


def _in_proj(x, g, w_in):
    bsz, seq, _ = x.shape
    tok = lambda width: pl.BlockSpec((None, TOK_IN, width), lambda b, i: (b, i, 0))
    widths = (D_SSM, D_SSM, D_ATTN, D_KV, D_KV, D_ATTN)
    return pl.pallas_call(
        _in_proj_kernel,
        grid=(bsz, seq // TOK_IN),
        in_specs=[
            tok(D_MODEL),
            pl.BlockSpec((1, D_MODEL), lambda b, i: (0, 0)),
            pl.BlockSpec((D_MODEL, D_IN), lambda b, i: (0, 0), pipeline_mode=pl.Buffered(1)),
        ],
        out_specs=tuple(tok(w) for w in widths),
        out_shape=tuple(jax.ShapeDtypeStruct((bsz, seq, w), BF16) for w in widths),
        scratch_shapes=[pltpu.VMEM((D_MODEL, D_IN), BF16)],
        compiler_params=pltpu.CompilerParams(
            dimension_semantics=("arbitrary", "arbitrary"), vmem_limit_bytes=VMEM_LIMIT),
        name="in_proj",
    )(x, g, w_in)


def _ssm_kernel(u_ref, perm_ref, permt_ref, win_re_ref, win_im_ref, l2re_ref, l2im_ref,
                wout_re_ref, wout_im_ref, wdir_ref, d_ref, wglu_ref, bglu_ref, o_ref,
                sre_ref, sim_ref, st_ref, y_ref,
                g0_ref, ut0_ref, g1_ref, ut1_ref, *, bsz):
    rows = bsz * T_SCAN
    half = rows // 2
    step = pl.program_id(0)

    @pl.when(step == 0)
    def _():
        sre_ref[...] = jnp.zeros_like(sre_ref)
        sim_ref[...] = jnp.zeros_like(sim_ref)
        g1_ref[...] = jnp.zeros_like(g1_ref)
        ut1_ref[...] = jnp.zeros_like(ut1_ref)

    def body(write, read):
        g_w, ut_w = write
        g_r, ut_r = read

        u_bt = u_ref[...].reshape(rows, D_SSM)
        u_new = jnp.dot(perm_ref[...], u_bt, preferred_element_type=F32).astype(BF16)
        ut_w[...] = u_new
        u_before = jnp.concatenate([ut_r[rows - bsz:, :], u_new[half:rows - bsz]], axis=0)
        for j in range(N_SLABS):
            ch = slice(j * SLAB_CH, (j + 1) * SLAB_CH)
            drive_in = jnp.concatenate([u_before[:, ch], u_new[:half, ch]], axis=1)
            for c in range(SLAB_STATE // SCAN_COLS):
                lo = j * SLAB_STATE + c * SCAN_COLS
                cre = slice(c * SCAN_COLS, (c + 1) * SCAN_COLS)
                cim = slice(SLAB_STATE + c * SCAN_COLS, SLAB_STATE + (c + 1) * SCAN_COLS)
                l2r = jnp.broadcast_to(l2re_ref[:, lo:lo + SCAN_COLS], (bsz, SCAN_COLS))
                l2i = jnp.broadcast_to(l2im_ref[:, lo:lo + SCAN_COLS], (bsz, SCAN_COLS))
                sr = sre_ref[:, lo:lo + SCAN_COLS]
                si = sim_ref[:, lo:lo + SCAN_COLS]
                for piece in range(half // PIECE_ROWS):
                    rk = slice(piece * PIECE_ROWS, (piece + 1) * PIECE_ROWS)
                    g_w[j, rk, cre] = jnp.dot(drive_in[rk], win_re_ref[j, :, cre],
                                              preferred_element_type=F32)
                    g_w[j, rk, cim] = jnp.dot(drive_in[rk], win_im_ref[j, :, cre],
                                              preferred_element_type=F32)
                    for k in range(rk.start // bsz, rk.stop // bsz):
                        r = slice(k * bsz, (k + 1) * bsz)
                        sr, si = (l2r * sr - l2i * si + g_r[j, r, cre],
                                  l2r * si + l2i * sr + g_r[j, r, cim])
                        st_ref[j, r, cre] = sr.astype(BF16)
                        st_ref[j, r, cim] = si.astype(BF16)
                    yk = (jnp.dot(st_ref[j, rk, cre], wout_re_ref[j, cre, :],
                                  preferred_element_type=F32)
                          + jnp.dot(st_ref[j, rk, cim], wout_im_ref[j, cre, :],
                                    preferred_element_type=F32))
                    odd_rows = slice(half + rk.start, half + rk.stop)
                    if c == 0:
                        y_ref[rk, ch] = yk[:, :SLAB_CH]
                        y_ref[odd_rows, ch] = yk[:, SLAB_CH:] + jnp.dot(
                            ut_r[odd_rows, ch], wdir_ref[j], preferred_element_type=F32)
                    else:
                        y_ref[rk, ch] += yk[:, :SLAB_CH]
                        y_ref[odd_rows, ch] += yk[:, SLAB_CH:]
                sre_ref[:, lo:lo + SCAN_COLS] = sr
                sim_ref[:, lo:lo + SCAN_COLS] = si

        y = y_ref[...] + d_ref[...] * ut_r[...].astype(F32)
        g = jax.nn.gelu(y)
        gate = jnp.dot(g.astype(BF16), wglu_ref[...], preferred_element_type=F32) + bglu_ref[...]
        out_tb = (g * jax.nn.sigmoid(gate)).astype(BF16)
        out_bt = jnp.dot(permt_ref[...], out_tb, preferred_element_type=F32).astype(BF16)
        o_ref[...] = out_bt.reshape(bsz, T_SCAN, D_SSM)

    _by_parity(step, body, (g0_ref, ut0_ref), (g1_ref, ut1_ref))


def _row_permutation(bsz):
    rows = bsz * T_SCAN
    perm = np.zeros((rows, rows), np.float32)
    t, b = np.meshgrid(np.arange(T_SCAN), np.arange(bsz), indexing="ij")
    dst = (t % 2) * (rows // 2) + (t // 2) * bsz + b
    perm[dst.ravel(), (b * T_SCAN + t).ravel()] = 1.0
    return perm


def _ssm(u, params, d, wglu, bglu):
    bsz, seq, _ = u.shape
    rows = T_SCAN * bsz
    half = rows // 2
    nblk = seq // T_SCAN
    perm = _row_permutation(bsz)
    full = lambda shape: pl.BlockSpec(shape, lambda f: (0,) * len(shape))
    g_scratch = pltpu.VMEM((N_SLABS, half, 2 * SLAB_STATE), F32)
    ut_scratch = pltpu.VMEM((rows, D_SSM), BF16)
    return pl.pallas_call(
        functools.partial(_ssm_kernel, bsz=bsz),
        grid=(nblk + 1,),
        in_specs=[
            pl.BlockSpec((bsz, T_SCAN, D_SSM), lambda f: (0, jnp.minimum(f, nblk - 1), 0)),
            full((rows, rows)), full((rows, rows)),
            full((N_SLABS, 2 * SLAB_CH, SLAB_STATE)), full((N_SLABS, 2 * SLAB_CH, SLAB_STATE)),
            full((1, N_STATE)), full((1, N_STATE)),
            full((N_SLABS, SLAB_STATE, 2 * SLAB_CH)), full((N_SLABS, SLAB_STATE, 2 * SLAB_CH)),
            full((N_SLABS, SLAB_CH, SLAB_CH)),
            full((1, D_SSM)), full((D_SSM, D_SSM)), full((1, D_SSM)),
        ],
        out_specs=pl.BlockSpec((bsz, T_SCAN, D_SSM), lambda f: (0, jnp.maximum(f - 1, 0), 0)),
        out_shape=jax.ShapeDtypeStruct((bsz, seq, D_SSM), BF16),
        scratch_shapes=[
            pltpu.VMEM((bsz, N_STATE), F32),
            pltpu.VMEM((bsz, N_STATE), F32),
            pltpu.VMEM((N_SLABS, half, 2 * SLAB_STATE), BF16),
            pltpu.VMEM((rows, D_SSM), F32),
            g_scratch, ut_scratch, g_scratch, ut_scratch,
        ],
        compiler_params=pltpu.CompilerParams(
            dimension_semantics=("arbitrary",), vmem_limit_bytes=VMEM_LIMIT),
        name="s5_scan",
    )(u, jnp.asarray(perm, BF16), jnp.asarray(perm.T, BF16), *params, d, wglu, bglu)


def _attention_pieces(first, sink_ref, q_ref, k_ref, v_ref, kp_ref, vp_ref, za_ref, bias_ref,
                      half_ref, a_ref):
    low_q = lax.broadcasted_iota(jnp.int32, (WINDOW, LANES), 1) < HEAD_DIM
    kv_cache = {}

    def swap_halves(t):
        packed = pltpu.bitcast(t, jnp.uint32)
        return pltpu.bitcast(pltpu.roll(packed, HEAD_DIM, axis=1), BF16)

    def keys_values(jj, g):
        if jj not in kv_cache:
            rows = slice(jj * WINDOW, (jj + 1) * WINDOW)
            if jj == 0:
                kp, vp = kp_ref[...], vp_ref[...]
            else:
                prev = slice((jj - 1) * WINDOW, jj * WINDOW)
                kp, vp = k_ref[prev, :], v_ref[prev, :]
            kk = jnp.concatenate([kp, k_ref[rows, :]], axis=0)
            vv = jnp.concatenate([vp, v_ref[rows, :]], axis=0)
            kv_cache[jj] = ((kk, swap_halves(kk)), (vv, swap_halves(vv)))
        if (jj, g) not in kv_cache:
            keep_even, keep_odd = half_ref[0], half_ref[1]
            k2, v2 = kv_cache[jj]
            kcat = jnp.concatenate([k2[g] * keep_even, k2[1 - g] * keep_odd], axis=0)
            vcat = jnp.concatenate(
                [jnp.concatenate([v2[g] * keep_even, keep_even], axis=1),
                 jnp.concatenate([v2[1 - g] * keep_odd, keep_odd], axis=1)], axis=0)
            kv_cache[(jj, g)] = (kcat, vcat)
        return kv_cache[(jj, g)]

    items = [(jj, pair) for jj in range(TOK_OUT // WINDOW) for pair in range(N_HEADS // 2)]
    scores, probs = {}, {}

    def where(i):
        jj, pair = items[i]
        return (jj, pair, slice(jj * WINDOW, (jj + 1) * WINDOW),
                slice(pair * LANES, (pair + 1) * LANES))

    def stage_scores(i):
        jj, pair, rows, lanes = where(i)
        kcat, _ = keys_values(jj, pair // (Q_PER_KV // 2))
        s = lax.dot_general(q_ref[rows, lanes], kcat, (((1,), (1,)), ((), ())),
                            preferred_element_type=F32)
        scores[i] = s + (bias_ref[pair + (N_HEADS // 2) * first] if jj == 0 else bias_ref[pair])

    def stage_softmax(i):
        _, pair, _, _ = where(i)
        s = scores.pop(i)
        sink_e, sink_o = sink_ref[2 * pair], sink_ref[2 * pair + 1]
        m_e = jnp.maximum(jnp.max(s[:, :2 * WINDOW], axis=-1, keepdims=True), sink_e)
        m_o = jnp.maximum(jnp.max(s[:, 2 * WINDOW:], axis=-1, keepdims=True), sink_o)
        e = jnp.concatenate([jnp.exp(s[:, :2 * WINDOW] - m_e),
                             jnp.exp(s[:, 2 * WINDOW:] - m_o)], axis=1).astype(BF16)
        probs[i] = (e, jnp.where(low_q, jnp.exp(sink_e - m_e), jnp.exp(sink_o - m_o)))

    def stage_output(i):
        jj, pair, rows, lanes = where(i)
        _, vcat = keys_values(jj, pair // (Q_PER_KV // 2))
        e, sink_term = probs.pop(i)
        o = jnp.dot(e, vcat, preferred_element_type=F32)
        den = o[:, LANES:] + sink_term
        a_ref[rows, lanes] = (o[:, :LANES] / den * za_ref[rows, lanes].astype(F32)).astype(BF16)

    def slot(it):
        if it < len(items):
            stage_scores(it)
        if 0 <= it - 1 < len(items):
            stage_softmax(it - 1)
        if 0 <= it - 2 < len(items):
            stage_output(it - 2)

    return [functools.partial(slot, it) for it in range(len(items) + 2)]


def _attn_out_kernel(sink_ref, x_ref, p_ref, ssm_ref, zs_ref, q_ref, k_ref, v_ref, kp_ref, vp_ref,
                     za_ref, bias_ref, half_ref, wout_f32, pg_ref, wgate_f32, bgate_ref, wproj_f32,
                     o_ref, wout_ref, wgate_ref, wproj_ref, a0_ref, a1_ref,
                     *, blocks_per_seq, n_blocks):
    step = pl.program_id(0)
    seq_block = jnp.minimum(step, n_blocks - 1) % blocks_per_seq
    first = (seq_block == 0).astype(jnp.int32)

    @pl.when(step == 0)
    def _():
        wout_ref[...] = wout_f32[...].astype(BF16)
        wgate_ref[...] = wgate_f32[...].astype(BF16)
        wproj_ref[...] = wproj_f32[...].astype(BF16)
        a1_ref[...] = jnp.zeros_like(a1_ref)

    def body(a_w, a_r):
        pieces = _attention_pieces(first, sink_ref, q_ref, k_ref, v_ref, kp_ref, vp_ref, za_ref,
                                   bias_ref, half_ref, a_w)
        n_chunks = D_MODEL // PROJ_COLS
        per_chunk = (len(pieces) - 2) // (2 * n_chunks)
        pieces = iter(pieces)

        def run_pieces(count=per_chunk):
            for piece in (list(pieces) if count is None else [p for _, p in zip(range(count), pieces)]):
                piece()

        run_pieces(2)

        s5 = (ssm_ref[...].astype(F32) * zs_ref[...].astype(F32)).astype(BF16)
        a_prev = a_r[...]
        mixed = []
        for n in range(n_chunks):
            cols = slice(n * PROJ_COLS, (n + 1) * PROJ_COLS)
            mixed.append(jnp.dot(s5, wout_ref[:D_SSM, cols], preferred_element_type=F32)
                         + jnp.dot(a_prev, wout_ref[D_SSM:, cols], preferred_element_type=F32))
            run_pieces()
        mixed = jnp.concatenate(mixed, axis=1)
        ms = jnp.mean(mixed * mixed, axis=-1, keepdims=True)
        h1 = x_ref[...] + mixed * lax.rsqrt(ms + EPS) * pg_ref[...]
        h1b = h1.astype(BF16)
        pb = p_ref[...].astype(BF16)
        for n in range(n_chunks):
            cols = slice(n * PROJ_COLS, (n + 1) * PROJ_COLS)
            gate = jax.nn.sigmoid(
                jnp.dot(h1b, wgate_ref[:, cols], preferred_element_type=F32) + bgate_ref[:, cols])
            ple = jnp.dot(pb, wproj_ref[:, cols], preferred_element_type=F32)
            o_ref[:, cols] = h1[:, cols] + gate * ple
            run_pieces()
        run_pieces(None)

    _by_parity(step, body, a0_ref, a1_ref)


def _attn_bias_table():
    q_idx = np.arange(WINDOW)[:, None]
    s_idx = np.arange(2 * WINDOW)[None, :]
    dist = q_idx + WINDOW - s_idx
    valid = (dist >= 0) & (dist < WINDOW)
    slopes = np.exp2(-8.0 * (np.arange(N_HEADS, dtype=np.float32) + 1.0) / N_HEADS).astype(np.float32)
    bias = -slopes[:, None, None] * dist.astype(np.float32)[None]
    normal = np.where(valid[None], bias, -np.inf)
    first = np.where((valid & (s_idx >= WINDOW))[None], bias, -np.inf)
    paired = lambda t: t.reshape(N_HEADS // 2, 2, WINDOW, 2 * WINDOW).transpose(0, 2, 1, 3).reshape(
        N_HEADS // 2, WINDOW, 4 * WINDOW)
    return np.concatenate([paired(normal), paired(first)], axis=0).astype(np.float32)


def _attn_out(sinks, x, p, ssm, zs, q, k, v, za, wout, pg, wgate, bgate, wproj):
    bsz, seq, _ = x.shape
    sub = TOK_OUT // WINDOW
    per_seq = seq // TOK_OUT
    n_blocks = bsz * per_seq

    def cur(f):
        f = jnp.minimum(f, n_blocks - 1)
        return f // per_seq, f % per_seq

    def prv(f):
        f = jnp.maximum(f - 1, 0)
        return f // per_seq, f % per_seq

    front = lambda width: pl.BlockSpec((None, TOK_OUT, width), lambda f: (*cur(f), 0))
    back = lambda width: pl.BlockSpec((None, TOK_OUT, width), lambda f: (*prv(f), 0))
    prev_kv = pl.BlockSpec(
        (None, WINDOW, D_KV),
        lambda f: (cur(f)[0], jnp.maximum(cur(f)[1] * sub - 1, 0), 0))
    full = lambda shape: pl.BlockSpec(shape, lambda f: (0,) * len(shape))
    once = lambda shape: pl.BlockSpec(shape, lambda f: (0,) * len(shape),
                                      pipeline_mode=pl.Buffered(1))
    bias = jnp.asarray(_attn_bias_table())
    lane_is_even = np.broadcast_to(np.arange(LANES) < HEAD_DIM, (2 * WINDOW, LANES))
    half = jnp.asarray(np.stack([lane_is_even, ~lane_is_even]), BF16)
    a_scratch = pltpu.VMEM((TOK_OUT, D_ATTN), BF16)
    return pl.pallas_call(
        functools.partial(_attn_out_kernel, blocks_per_seq=per_seq, n_blocks=n_blocks),
        grid=(n_blocks + 1,),
        in_specs=[
            pl.BlockSpec(memory_space=pltpu.SMEM),
            back(D_MODEL), back(D_PLE), back(D_SSM), back(D_SSM), front(D_ATTN),
            front(D_KV), front(D_KV), prev_kv, prev_kv, front(D_ATTN),
            once(bias.shape), once(half.shape),
            once((D_MODEL, D_MODEL)), full((1, D_MODEL)),
            once((D_MODEL, D_MODEL)), full((1, D_MODEL)), once((D_PLE, D_MODEL)),
        ],
        out_specs=back(D_MODEL),
        out_shape=jax.ShapeDtypeStruct((bsz, seq, D_MODEL), F32),
        scratch_shapes=[pltpu.VMEM((D_MODEL, D_MODEL), BF16), pltpu.VMEM((D_MODEL, D_MODEL), BF16),
                        pltpu.VMEM((D_PLE, D_MODEL), BF16), a_scratch, a_scratch],
        compiler_params=pltpu.CompilerParams(
            dimension_semantics=("arbitrary",), vmem_limit_bytes=VMEM_LIMIT),
        name="attn_out",
    )(sinks, x, p, ssm, zs, q, k, v, k, v, za, bias, half, wout, pg, wgate, bgate, wproj)


def _block_diag(blocks):
    s, n, a, b = blocks.shape
    eye = jnp.eye(n, dtype=blocks.dtype)
    return jnp.einsum('snab,nm->snamb', blocks, eye).reshape(s, n * a, n * b)


def _ssm_params(lam_re, lam_im, log_step, b_re, b_im, c_re, c_im):
    lr, li = lam_re.astype(F32), lam_im.astype(F32)
    step = jnp.exp(log_step.astype(F32))[:, None]
    mag = jnp.exp(lr * step)
    lbr, lbi = mag * jnp.cos(li * step), mag * jnp.sin(li * step)
    den = lr * lr + li * li
    fr = ((lbr - 1.0) * lr + lbi * li) / den
    fi = (lbi * lr - (lbr - 1.0) * li) / den
    br, bi = b_re.astype(F32), b_im.astype(F32)
    bbr = fr[..., None] * br - fi[..., None] * bi
    bbi = fr[..., None] * bi + fi[..., None] * br
    cr, ci = c_re.astype(F32), c_im.astype(F32)
    l2r, l2i = lbr * lbr - lbi * lbi, 2.0 * lbr * lbi
    lbbr = lbr[..., None] * bbr - lbi[..., None] * bbi
    lbbi = lbr[..., None] * bbi + lbi[..., None] * bbr
    clr = cr * lbr[:, None, :] - ci * lbi[:, None, :]
    cli = cr * lbi[:, None, :] + ci * lbr[:, None, :]
    cb = jnp.sum(cr[:, :, :, None] * bbr[:, None, :, :] - ci[:, :, :, None] * bbi[:, None, :, :],
                 axis=2)
    gps = SLAB_CH // SSM_GROUP_CH

    def in_w(part):
        blocks = part.reshape(N_SLABS, gps, SSM_STATE, SSM_GROUP_CH).transpose(0, 1, 3, 2)
        return _block_diag(blocks)

    def out_w(part):
        blocks = part.reshape(N_SLABS, gps, SSM_GROUP_CH, SSM_STATE).transpose(0, 1, 3, 2)
        return _block_diag(blocks)

    cat = lambda parts, axis: jnp.concatenate(parts, axis=axis).astype(BF16)
    win_re = cat([in_w(lbbr), in_w(bbr)], 1)
    win_im = cat([in_w(lbbi), in_w(bbi)], 1)
    wout_re = cat([out_w(cr), out_w(clr)], 2)
    wout_im = cat([out_w(-ci), out_w(-cli)], 2)
    wdir = _block_diag(cb.reshape(N_SLABS, gps, SSM_GROUP_CH, SSM_GROUP_CH)
                       .transpose(0, 1, 3, 2)).astype(BF16)
    return (win_re, win_im, l2r.reshape(1, N_STATE), l2i.reshape(1, N_STATE),
            wout_re, wout_im, wdir)


def _layer(h, p, pre_norm_g, w_in, ssm_lam_re, ssm_lam_im, ssm_log_step, ssm_b_re, ssm_b_im,
           ssm_c_re, ssm_c_im, ssm_d, ssm_w_glu, ssm_b_glu, attn_sinks, w_out, post_norm_g,
           pl_w_proj, pl_w_gate, pl_b_gate):
    row = lambda a: a.astype(F32).reshape(1, -1)
    u, zs, q, k, v, za = _in_proj(h, row(pre_norm_g), w_in.astype(F32))
    params = _ssm_params(
        ssm_lam_re, ssm_lam_im, ssm_log_step, ssm_b_re, ssm_b_im, ssm_c_re, ssm_c_im)
    ssm = _ssm(u, params, row(ssm_d), ssm_w_glu.astype(BF16), row(ssm_b_glu))
    return _attn_out(attn_sinks.astype(F32), h, p, ssm, zs, q, k, v, za,
                     w_out.astype(F32), row(post_norm_g), pl_w_gate.astype(F32), row(pl_b_gate),
                     pl_w_proj.astype(F32))


def kernel(x, p, pre_norm_g, w_in, ssm_lam_re, ssm_lam_im, ssm_log_step, ssm_b_re, ssm_b_im,
           ssm_c_re, ssm_c_im, ssm_d, ssm_w_glu, ssm_b_glu, attn_sinks, w_out, post_norm_g,
           pl_w_proj, pl_w_gate, pl_b_gate):
    h = x
    for i in range(p.shape[0]):
        h = _layer(h, p[i], pre_norm_g[i], w_in[i], ssm_lam_re[i], ssm_lam_im[i], ssm_log_step[i],
                   ssm_b_re[i], ssm_b_im[i], ssm_c_re[i], ssm_c_im[i], ssm_d[i], ssm_w_glu[i],
                   ssm_b_glu[i], attn_sinks[i], w_out[i], post_norm_g[i], pl_w_proj[i],
                   pl_w_gate[i], pl_b_gate[i])
    return h
```

```python
import functools
import math

import numpy as np
import jax
import jax.numpy as jnp
from jax import lax
from jax.experimental import pallas as pl
from jax.experimental.pallas import tpu as pltpu

F32 = jnp.float32
BF16 = jnp.bfloat16

D_MODEL = 1024
D_SSM = 512
D_ATTN = 512
SSM_GROUP_CH = 16
SSM_GROUPS = D_SSM // SSM_GROUP_CH
SSM_STATE = 64
N_STATE = SSM_GROUPS * SSM_STATE
HEAD_DIM = 64
N_HEADS = D_ATTN // HEAD_DIM
KV_HEADS = 2
Q_PER_KV = N_HEADS // KV_HEADS
D_KV = KV_HEADS * HEAD_DIM
WINDOW = 128
D_PLE = 256
EPS = 1e-6

LANES = 128
assert D_KV == LANES and 2 * HEAD_DIM == LANES

OFF_U, OFF_ZS, OFF_Q, OFF_K, OFF_V, OFF_ZA, D_IN = 0, 512, 1024, 1536, 1664, 1792, 2304

SLAB_CH = LANES
N_SLABS = D_SSM // SLAB_CH
SLAB_STATE = (SLAB_CH // SSM_GROUP_CH) * SSM_STATE
SCAN_COLS = 256

TOK_IN = 1024
NORM_ROWS = 512
T_SCAN = 32
PIECE_ROWS = 128
TOK_OUT = 512
PROJ_COLS = 256
OUT_COLS = 256
VMEM_LIMIT = 48 * 1024 * 1024


def _silu(z):
    return z * jax.nn.sigmoid(z)


def _by_parity(step, body, even_bufs, odd_bufs):
    @pl.when(step % 2 == 0)
    def _():
        body(even_bufs, odd_bufs)

    @pl.when(step % 2 == 1)
    def _():
        body(odd_bufs, even_bufs)


def _in_proj_kernel(x_ref, g_ref, w_ref, u_ref, zs_ref, q_ref, k_ref, v_ref, za_ref, w_scr):
    @pl.when((pl.program_id(0) == 0) & (pl.program_id(1) == 0))
    def _():
        w_scr[...] = w_ref[...].astype(BF16)

    def emit(rows, tile, acc):
        lo = tile * PROJ_COLS
        if lo < OFF_ZS:
            u_ref[rows, lo - OFF_U:lo - OFF_U + PROJ_COLS] = acc.astype(BF16)
        elif lo < OFF_Q:
            zs_ref[rows, lo - OFF_ZS:lo - OFF_ZS + PROJ_COLS] = _silu(acc).astype(BF16)
        elif lo < OFF_K:
            q_ref[rows, lo - OFF_Q:lo - OFF_Q + PROJ_COLS] = (
                acc * (1.0 / math.sqrt(HEAD_DIM))).astype(BF16)
        elif lo < OFF_ZA:
            k_ref[rows, :] = acc[:, :D_KV].astype(BF16)
            v_ref[rows, :] = acc[:, D_KV:].astype(BF16)
        else:
            za_ref[rows, lo - OFF_ZA:lo - OFF_ZA + PROJ_COLS] = _silu(acc).astype(BF16)

    gain = g_ref[...]
    for half in range(TOK_IN // NORM_ROWS):
        rows = slice(half * NORM_ROWS, (half + 1) * NORM_ROWS)
        x = x_ref[rows, :]
        ms = jnp.mean(x * x, axis=-1, keepdims=True)
        hn = (x * lax.rsqrt(ms + EPS) * gain).astype(BF16)
        for tile in range(D_IN // PROJ_COLS):
            cols = slice(tile * PROJ_COLS, (tile + 1) * PROJ_COLS)
            emit(rows, tile, jnp.dot(hn, w_scr[:, cols], preferred_element_type=F32))


def _in_proj(x, g, w_in):
    bsz, seq, _ = x.shape
    tok = lambda width: pl.BlockSpec((None, TOK_IN, width), lambda b, i: (b, i, 0))
    widths = (D_SSM, D_SSM, D_ATTN, D_KV, D_KV, D_ATTN)
    return pl.pallas_call(
        _in_proj_kernel,
        grid=(bsz, seq // TOK_IN),
        in_specs=[
            tok(D_MODEL),
            pl.BlockSpec((1, D_MODEL), lambda b, i: (0, 0)),
            pl.BlockSpec((D_MODEL, D_IN), lambda b, i: (0, 0), pipeline_mode=pl.Buffered(1)),
        ],
        out_specs=tuple(tok(w) for w in widths),
        out_shape=tuple(jax.ShapeDtypeStruct((bsz, seq, w), BF16) for w in widths),
        scratch_shapes=[pltpu.VMEM((D_MODEL, D_IN), BF16)],
        compiler_params=pltpu.CompilerParams(
            dimension_semantics=("arbitrary", "arbitrary"), vmem_limit_bytes=VMEM_LIMIT),
        name="in_proj",
    )(x, g, w_in)


def _ssm_kernel(u_ref, perm_ref, permt_ref, win_re_ref, win_im_ref, l2re_ref, l2im_ref,
                wout_re_ref, wout_im_ref, wdir_ref, d_ref, wglu_ref, bglu_ref, o_ref,
                sre_ref, sim_ref, st_ref, y_ref,
                g0_ref, ut0_ref, g1_ref, ut1_ref, *, bsz):
    rows = bsz * T_SCAN
    half = rows // 2
    step = pl.program_id(0)

    @pl.when(step == 0)
    def _():
        sre_ref[...] = jnp.zeros_like(sre_ref)
        sim_ref[...] = jnp.zeros_like(sim_ref)
        g1_ref[...] = jnp.zeros_like(g1_ref)
        ut1_ref[...] = jnp.zeros_like(ut1_ref)

    def body(write, read):
        g_w, ut_w = write
        g_r, ut_r = read

        u_bt = u_ref[...].reshape(rows, D_SSM)
        u_new = jnp.dot(perm_ref[...], u_bt, preferred_element_type=F32).astype(BF16)
        ut_w[...] = u_new
        u_before = jnp.concatenate([ut_r[rows - bsz:, :], u_new[half:rows - bsz]], axis=0)
        for j in range(N_SLABS):
            ch = slice(j * SLAB_CH, (j + 1) * SLAB_CH)
            drive_in = jnp.concatenate([u_before[:, ch], u_new[:half, ch]], axis=1)
            for c in range(SLAB_STATE // SCAN_COLS):
                lo = j * SLAB_STATE + c * SCAN_COLS
                cre = slice(c * SCAN_COLS, (c + 1) * SCAN_COLS)
                cim = slice(SLAB_STATE + c * SCAN_COLS, SLAB_STATE + (c + 1) * SCAN_COLS)
                l2r = jnp.broadcast_to(l2re_ref[:, lo:lo + SCAN_COLS], (bsz, SCAN_COLS))
                l2i = jnp.broadcast_to(l2im_ref[:, lo:lo + SCAN_COLS], (bsz, SCAN_COLS))
                sr = sre_ref[:, lo:lo + SCAN_COLS]
                si = sim_ref[:, lo:lo + SCAN_COLS]
                for piece in range(half // PIECE_ROWS):
                    rk = slice(piece * PIECE_ROWS, (piece + 1) * PIECE_ROWS)
                    g_w[j, rk, cre] = jnp.dot(drive_in[rk], win_re_ref[j, :, cre],
                                              preferred_element_type=F32)
                    g_w[j, rk, cim] = jnp.dot(drive_in[rk], win_im_ref[j, :, cre],
                                              preferred_element_type=F32)
                    for k in range(rk.start // bsz, rk.stop // bsz):
                        r = slice(k * bsz, (k + 1) * bsz)
                        sr, si = (l2r * sr - l2i * si + g_r[j, r, cre],
                                  l2r * si + l2i * sr + g_r[j, r, cim])
                        st_ref[j, r, cre] = sr.astype(BF16)
                        st_ref[j, r, cim] = si.astype(BF16)
                    yk = (jnp.dot(st_ref[j, rk, cre], wout_re_ref[j, cre, :],
                                  preferred_element_type=F32)
                          + jnp.dot(st_ref[j, rk, cim], wout_im_ref[j, cre, :],
                                    preferred_element_type=F32))
                    odd_rows = slice(half + rk.start, half + rk.stop)
                    if c == 0:
                        y_ref[rk, ch] = yk[:, :SLAB_CH]
                        y_ref[odd_rows, ch] = yk[:, SLAB_CH:] + jnp.dot(
                            ut_r[odd_rows, ch], wdir_ref[j], preferred_element_type=F32)
                    else:
                        y_ref[rk, ch] += yk[:, :SLAB_CH]
                        y_ref[odd_rows, ch] += yk[:, SLAB_CH:]
                sre_ref[:, lo:lo + SCAN_COLS] = sr
                sim_ref[:, lo:lo + SCAN_COLS] = si

        y = y_ref[...] + d_ref[...] * ut_r[...].astype(F32)
        g = jax.nn.gelu(y)
        gate = jnp.dot(g.astype(BF16), wglu_ref[...], preferred_element_type=F32) + bglu_ref[...]
        out_tb = (g * jax.nn.sigmoid(gate)).astype(BF16)
        out_bt = jnp.dot(permt_ref[...], out_tb, preferred_element_type=F32).astype(BF16)
        o_ref[...] = out_bt.reshape(bsz, T_SCAN, D_SSM)

    _by_parity(step, body, (g0_ref, ut0_ref), (g1_ref, ut1_ref))


def _row_permutation(bsz):
    rows = bsz * T_SCAN
    perm = np.zeros((rows, rows), np.float32)
    t, b = np.meshgrid(np.arange(T_SCAN), np.arange(bsz), indexing="ij")
    dst = (t % 2) * (rows // 2) + (t // 2) * bsz + b
    perm[dst.ravel(), (b * T_SCAN + t).ravel()] = 1.0
    return perm


def _ssm(u, params, d, wglu, bglu):
    bsz, seq, _ = u.shape
    rows = T_SCAN * bsz
    half = rows // 2
    nblk = seq // T_SCAN
    perm = _row_permutation(bsz)
    full = lambda shape: pl.BlockSpec(shape, lambda f: (0,) * len(shape))
    g_scratch = pltpu.VMEM((N_SLABS, half, 2 * SLAB_STATE), F32)
    ut_scratch = pltpu.VMEM((rows, D_SSM), BF16)
    return pl.pallas_call(
        functools.partial(_ssm_kernel, bsz=bsz),
        grid=(nblk + 1,),
        in_specs=[
            pl.BlockSpec((bsz, T_SCAN, D_SSM), lambda f: (0, jnp.minimum(f, nblk - 1), 0)),
            full((rows, rows)), full((rows, rows)),
            full((N_SLABS, 2 * SLAB_CH, SLAB_STATE)), full((N_SLABS, 2 * SLAB_CH, SLAB_STATE)),
            full((1, N_STATE)), full((1, N_STATE)),
            full((N_SLABS, SLAB_STATE, 2 * SLAB_CH)), full((N_SLABS, SLAB_STATE, 2 * SLAB_CH)),
            full((N_SLABS, SLAB_CH, SLAB_CH)),
            full((1, D_SSM)), full((D_SSM, D_SSM)), full((1, D_SSM)),
        ],
        out_specs=pl.BlockSpec((bsz, T_SCAN, D_SSM), lambda f: (0, jnp.maximum(f - 1, 0), 0)),
        out_shape=jax.ShapeDtypeStruct((bsz, seq, D_SSM), BF16),
        scratch_shapes=[
            pltpu.VMEM((bsz, N_STATE), F32),
            pltpu.VMEM((bsz, N_STATE), F32),
            pltpu.VMEM((N_SLABS, half, 2 * SLAB_STATE), BF16),
            pltpu.VMEM((rows, D_SSM), F32),
            g_scratch, ut_scratch, g_scratch, ut_scratch,
        ],
        compiler_params=pltpu.CompilerParams(
            dimension_semantics=("arbitrary",), vmem_limit_bytes=VMEM_LIMIT),
        name="s5_scan",
    )(u, jnp.asarray(perm, BF16), jnp.asarray(perm.T, BF16), *params, d, wglu, bglu)


def _attention_pieces(first, sink_ref, q_ref, k_ref, v_ref, kp_ref, vp_ref, za_ref, bias_ref,
                      half_ref, a_ref):
    low_q = lax.broadcasted_iota(jnp.int32, (WINDOW, LANES), 1) < HEAD_DIM
    kv_cache = {}

    def swap_halves(t):
        packed = pltpu.bitcast(t, jnp.uint32)
        return pltpu.bitcast(pltpu.roll(packed, HEAD_DIM, axis=1), BF16)

    def keys_values(jj, g):
        if jj not in kv_cache:
            rows = slice(jj * WINDOW, (jj + 1) * WINDOW)
            if jj == 0:
                kp, vp = kp_ref[...], vp_ref[...]
            else:
                prev = slice((jj - 1) * WINDOW, jj * WINDOW)
                kp, vp = k_ref[prev, :], v_ref[prev, :]
            kk = jnp.concatenate([kp, k_ref[rows, :]], axis=0)
            vv = jnp.concatenate([vp, v_ref[rows, :]], axis=0)
            kv_cache[jj] = ((kk, swap_halves(kk)), (vv, swap_halves(vv)))
        if (jj, g) not in kv_cache:
            keep_even, keep_odd = half_ref[0], half_ref[1]
            k2, v2 = kv_cache[jj]
            kcat = jnp.concatenate([k2[g] * keep_even, k2[1 - g] * keep_odd], axis=0)
            vcat = jnp.concatenate(
                [jnp.concatenate([v2[g] * keep_even, keep_even], axis=1),
                 jnp.concatenate([v2[1 - g] * keep_odd, keep_odd], axis=1)], axis=0)
            kv_cache[(jj, g)] = (kcat, vcat)
        return kv_cache[(jj, g)]

    items = [(jj, pair) for jj in range(TOK_OUT // WINDOW) for pair in range(N_HEADS // 2)]
    scores, probs = {}, {}

    def where(i):
        jj, pair = items[i]
        return (jj, pair, slice(jj * WINDOW, (jj + 1) * WINDOW),
                slice(pair * LANES, (pair + 1) * LANES))

    def stage_scores(i):
        jj, pair, rows, lanes = where(i)
        kcat, _ = keys_values(jj, pair // (Q_PER_KV // 2))
        s = lax.dot_general(q_ref[rows, lanes], kcat, (((1,), (1,)), ((), ())),
                            preferred_element_type=F32)
        scores[i] = s + (bias_ref[pair + (N_HEADS // 2) * first] if jj == 0 else bias_ref[pair])

    def stage_softmax(i):
        _, pair, _, _ = where(i)
        s = scores.pop(i)
        sink_e, sink_o = sink_ref[2 * pair], sink_ref[2 * pair + 1]
        m_e = jnp.maximum(jnp.max(s[:, :2 * WINDOW], axis=-1, keepdims=True), sink_e)
        m_o = jnp.maximum(jnp.max(s[:, 2 * WINDOW:], axis=-1, keepdims=True), sink_o)
        e = jnp.concatenate([jnp.exp(s[:, :2 * WINDOW] - m_e),
                             jnp.exp(s[:, 2 * WINDOW:] - m_o)], axis=1).astype(BF16)
        probs[i] = (e, jnp.where(low_q, jnp.exp(sink_e - m_e), jnp.exp(sink_o - m_o)))

    def stage_output(i):
        jj, pair, rows, lanes = where(i)
        _, vcat = keys_values(jj, pair // (Q_PER_KV // 2))
        e, sink_term = probs.pop(i)
        o = jnp.dot(e, vcat, preferred_element_type=F32)
        den = o[:, LANES:] + sink_term
        a_ref[rows, lanes] = (o[:, :LANES] / den * za_ref[rows, lanes].astype(F32)).astype(BF16)

    def slot(it):
        if it < len(items):
            stage_scores(it)
        if 0 <= it - 1 < len(items):
            stage_softmax(it - 1)
        if 0 <= it - 2 < len(items):
            stage_output(it - 2)

    return [functools.partial(slot, it) for it in range(len(items) + 2)]


def _attn_out_kernel(sink_ref, x_ref, p_ref, ssm_ref, zs_ref, q_ref, k_ref, v_ref, kp_ref, vp_ref,
                     za_ref, bias_ref, half_ref, wout_f32, pg_ref, wgate_f32, bgate_ref, wproj_f32,
                     o_ref, wout_ref, wgate_ref, wproj_ref, a0_ref, a1_ref,
                     *, blocks_per_seq, n_blocks):
    step = pl.program_id(0)
    seq_block = jnp.minimum(step, n_blocks - 1) % blocks_per_seq
    first = (seq_block == 0).astype(jnp.int32)

    @pl.when(step == 0)
    def _():
        wout_ref[...] = wout_f32[...].astype(BF16)
        wgate_ref[...] = wgate_f32[...].astype(BF16)
        wproj_ref[...] = wproj_f32[...].astype(BF16)
        a1_ref[...] = jnp.zeros_like(a1_ref)

    def body(a_w, a_r):
        pieces = _attention_pieces(first, sink_ref, q_ref, k_ref, v_ref, kp_ref, vp_ref, za_ref,
                                   bias_ref, half_ref, a_w)
        n_chunks = D_MODEL // OUT_COLS
        per_chunk = (len(pieces) - 2) // (2 * n_chunks)
        pieces = iter(pieces)

        def run_pieces(count=per_chunk):
            for piece in (list(pieces) if count is None else [p for _, p in zip(range(count), pieces)]):
                piece()

        run_pieces(2)

        s5 = (ssm_ref[...].astype(F32) * zs_ref[...].astype(F32)).astype(BF16)
        a_prev = a_r[...]
        mixed = []
        for n in range(n_chunks):
            cols = slice(n * OUT_COLS, (n + 1) * OUT_COLS)
            mixed.append(jnp.dot(a_prev, wout_ref[D_SSM:, cols], preferred_element_type=F32)
                         + jnp.dot(s5, wout_ref[:D_SSM, cols], preferred_element_type=F32))
            run_pieces()
        mixed = jnp.concatenate(mixed, axis=1)
        ms = jnp.mean(mixed * mixed, axis=-1, keepdims=True)
        h1 = x_ref[...] + mixed * lax.rsqrt(ms + EPS) * pg_ref[...]
        h1b = h1.astype(BF16)
        pb = p_ref[...].astype(BF16)
        for n in range(n_chunks):
            cols = slice(n * OUT_COLS, (n + 1) * OUT_COLS)
            gate = jax.nn.sigmoid(
                jnp.dot(h1b, wgate_ref[:, cols], preferred_element_type=F32) + bgate_ref[:, cols])
            ple = jnp.dot(pb, wproj_ref[:, cols], preferred_element_type=F32)
            o_ref[:, cols] = h1[:, cols] + gate * ple
            run_pieces()
        run_pieces(None)

    _by_parity(step, body, a0_ref, a1_ref)


def _attn_bias_table():
    q_idx = np.arange(WINDOW)[:, None]
    s_idx = np.arange(2 * WINDOW)[None, :]
    dist = q_idx + WINDOW - s_idx
    valid = (dist >= 0) & (dist < WINDOW)
    slopes = np.exp2(-8.0 * (np.arange(N_HEADS, dtype=np.float32) + 1.0) / N_HEADS).astype(np.float32)
    bias = -slopes[:, None, None] * dist.astype(np.float32)[None]
    normal = np.where(valid[None], bias, -np.inf)
    first = np.where((valid & (s_idx >= WINDOW))[None], bias, -np.inf)
    paired = lambda t: t.reshape(N_HEADS // 2, 2, WINDOW, 2 * WINDOW).transpose(0, 2, 1, 3).reshape(
        N_HEADS // 2, WINDOW, 4 * WINDOW)
    return np.concatenate([paired(normal), paired(first)], axis=0).astype(np.float32)


def _attn_out(sinks, x, p, ssm, zs, q, k, v, za, wout, pg, wgate, bgate, wproj):
    bsz, seq, _ = x.shape
    sub = TOK_OUT // WINDOW
    per_seq = seq // TOK_OUT
    n_blocks = bsz * per_seq

    def cur(f):
        f = jnp.minimum(f, n_blocks - 1)
        return f // per_seq, f % per_seq

    def prv(f):
        f = jnp.maximum(f - 1, 0)
        return f // per_seq, f % per_seq

    front = lambda width: pl.BlockSpec((None, TOK_OUT, width), lambda f: (*cur(f), 0))
    back = lambda width: pl.BlockSpec((None, TOK_OUT, width), lambda f: (*prv(f), 0))
    prev_kv = pl.BlockSpec(
        (None, WINDOW, D_KV),
        lambda f: (cur(f)[0], jnp.maximum(cur(f)[1] * sub - 1, 0), 0))
    full = lambda shape: pl.BlockSpec(shape, lambda f: (0,) * len(shape))
    once = lambda shape: pl.BlockSpec(shape, lambda f: (0,) * len(shape),
                                      pipeline_mode=pl.Buffered(1))
    bias = jnp.asarray(_attn_bias_table())
    lane_is_even = np.broadcast_to(np.arange(LANES) < HEAD_DIM, (2 * WINDOW, LANES))
    half = jnp.asarray(np.stack([lane_is_even, ~lane_is_even]), BF16)
    a_scratch = pltpu.VMEM((TOK_OUT, D_ATTN), BF16)
    return pl.pallas_call(
        functools.partial(_attn_out_kernel, blocks_per_seq=per_seq, n_blocks=n_blocks),
        grid=(n_blocks + 1,),
        in_specs=[
            pl.BlockSpec(memory_space=pltpu.SMEM),
            back(D_MODEL), back(D_PLE), back(D_SSM), back(D_SSM), front(D_ATTN),
            front(D_KV), front(D_KV), prev_kv, prev_kv, front(D_ATTN),
            once(bias.shape), once(half.shape),
            once((D_MODEL, D_MODEL)), full((1, D_MODEL)),
            once((D_MODEL, D_MODEL)), full((1, D_MODEL)), once((D_PLE, D_MODEL)),
        ],
        out_specs=back(D_MODEL),
        out_shape=jax.ShapeDtypeStruct((bsz, seq, D_MODEL), F32),
        scratch_shapes=[pltpu.VMEM((D_MODEL, D_MODEL), BF16), pltpu.VMEM((D_MODEL, D_MODEL), BF16),
                        pltpu.VMEM((D_PLE, D_MODEL), BF16), a_scratch, a_scratch],
        compiler_params=pltpu.CompilerParams(
            dimension_semantics=("arbitrary",), vmem_limit_bytes=VMEM_LIMIT),
        name="attn_out",
    )(sinks, x, p, ssm, zs, q, k, v, k, v, za, bias, half, wout, pg, wgate, bgate, wproj)


def _block_diag(blocks):
    s, n, a, b = blocks.shape
    eye = jnp.eye(n, dtype=blocks.dtype)[None, :, None, :, None]
    return (blocks[:, :, :, None, :] * eye).reshape(s, n * a, n * b)


def _ssm_params(lam_re, lam_im, log_step, b_re, b_im, c_re, c_im):
    lr, li = lam_re.astype(F32), lam_im.astype(F32)
    step = jnp.exp(log_step.astype(F32))[:, None]
    mag = jnp.exp(lr * step)
    lbr, lbi = mag * jnp.cos(li * step), mag * jnp.sin(li * step)
    den = lr * lr + li * li
    fr = ((lbr - 1.0) * lr + lbi * li) / den
    fi = (lbi * lr - (lbr - 1.0) * li) / den
    br, bi = b_re.astype(F32), b_im.astype(F32)
    bbr = fr[..., None] * br - fi[..., None] * bi
    bbi = fr[..., None] * bi + fi[..., None] * br
    cr, ci = c_re.astype(F32), c_im.astype(F32)
    l2r, l2i = lbr * lbr - lbi * lbi, 2.0 * lbr * lbi
    lbbr = lbr[..., None] * bbr - lbi[..., None] * bbi
    lbbi = lbr[..., None] * bbi + lbi[..., None] * bbr
    clr = cr * lbr[:, None, :] - ci * lbi[:, None, :]
    cli = cr * lbi[:, None, :] + ci * lbr[:, None, :]
    cb = jnp.sum(cr[:, :, :, None] * bbr[:, None, :, :] - ci[:, :, :, None] * bbi[:, None, :, :],
                 axis=2)
    gps = SLAB_CH // SSM_GROUP_CH

    def in_w(part):
        blocks = part.reshape(N_SLABS, gps, SSM_STATE, SSM_GROUP_CH).transpose(0, 1, 3, 2)
        return _block_diag(blocks)

    def out_w(part):
        blocks = part.reshape(N_SLABS, gps, SSM_GROUP_CH, SSM_STATE).transpose(0, 1, 3, 2)
        return _block_diag(blocks)

    cat = lambda parts, axis: jnp.concatenate(parts, axis=axis).astype(BF16)
    win_re = cat([in_w(lbbr), in_w(bbr)], 1)
    win_im = cat([in_w(lbbi), in_w(bbi)], 1)
    wout_re = cat([out_w(cr), out_w(clr)], 2)
    wout_im = cat([out_w(-ci), out_w(-cli)], 2)
    wdir = _block_diag(cb.reshape(N_SLABS, gps, SSM_GROUP_CH, SSM_GROUP_CH)
                       .transpose(0, 1, 3, 2)).astype(BF16)
    return (win_re, win_im, l2r.reshape(1, N_STATE), l2i.reshape(1, N_STATE),
            wout_re, wout_im, wdir)


def _layer(h, p, pre_norm_g, w_in, ssm_lam_re, ssm_lam_im, ssm_log_step, ssm_b_re, ssm_b_im,
           ssm_c_re, ssm_c_im, ssm_d, ssm_w_glu, ssm_b_glu, attn_sinks, w_out, post_norm_g,
           pl_w_proj, pl_w_gate, pl_b_gate):
    row = lambda a: a.astype(F32).reshape(1, -1)
    u, zs, q, k, v, za = _in_proj(h, row(pre_norm_g), w_in.astype(F32))
    params = _ssm_params(
        ssm_lam_re, ssm_lam_im, ssm_log_step, ssm_b_re, ssm_b_im, ssm_c_re, ssm_c_im)
    ssm = _ssm(u, params, row(ssm_d), ssm_w_glu.astype(BF16), row(ssm_b_glu))
    return _attn_out(attn_sinks.astype(F32), h, p, ssm, zs, q, k, v, za,
                     w_out.astype(F32), row(post_norm_g), pl_w_gate.astype(F32), row(pl_b_gate),
                     pl_w_proj.astype(F32))


def kernel(x, p, pre_norm_g, w_in, ssm_lam_re, ssm_lam_im, ssm_log_step, ssm_b_re, ssm_b_im,
           ssm_c_re, ssm_c_im, ssm_d, ssm_w_glu, ssm_b_glu, attn_sinks, w_out, post_norm_g,
           pl_w_proj, pl_w_gate, pl_b_gate):
    h = x
    for i in range(p.shape[0]):
        h = _layer(h, p[i], pre_norm_g[i], w_in[i], ssm_lam_re[i], ssm_lam_im[i], ssm_log_step[i],
                   ssm_b_re[i], ssm_b_im[i], ssm_c_re[i], ssm_c_im[i], ssm_d[i], ssm_w_glu[i],
                   ssm_b_glu[i], attn_sinks[i], w_out[i], post_norm_g[i], pl_w_proj[i],
                   pl_w_gate[i], pl_b_gate[i])
    return h
```

```python
import functools
import math

import numpy as np
import jax
import jax.numpy as jnp
from jax import lax
from jax.experimental import pallas as pl
from jax.experimental.pallas import tpu as pltpu

F32 = jnp.float32
BF16 = jnp.bfloat16

D_MODEL = 1024
D_SSM = 512
D_ATTN = 512
SSM_GROUP_CH = 16
SSM_GROUPS = D_SSM // SSM_GROUP_CH
SSM_STATE = 64
N_STATE = SSM_GROUPS * SSM_STATE
HEAD_DIM = 64
N_HEADS = D_ATTN // HEAD_DIM
KV_HEADS = 2
Q_PER_KV = N_HEADS // KV_HEADS
D_KV = KV_HEADS * HEAD_DIM
WINDOW = 128
D_PLE = 256
EPS = 1e-6

LANES = 128
assert D_KV == LANES and 2 * HEAD_DIM == LANES

OFF_U, OFF_ZS, OFF_Q, OFF_K, OFF_V, OFF_ZA, D_IN = 0, 512, 1024, 1536, 1664, 1792, 2304

SLAB_CH = LANES
N_SLABS = D_SSM // SLAB_CH
SLAB_STATE = (SLAB_CH // SSM_GROUP_CH) * SSM_STATE
SCAN_COLS = 256

TOK_IN = 1024
NORM_ROWS = 512
T_SCAN = 32
PIECE_ROWS = 128
PERM_BATCH = 8
TOK_OUT = 512
PROJ_COLS = 256
OUT_COLS = 256
VMEM_LIMIT = 48 * 1024 * 1024


def _silu(z):
    return z * jax.nn.sigmoid(z)


def _by_parity(step, body, even_bufs, odd_bufs):
    @pl.when(step % 2 == 0)
    def _():
        body(even_bufs, odd_bufs)

    @pl.when(step % 2 == 1)
    def _():
        body(odd_bufs, even_bufs)


def _in_proj_kernel(x_ref, g_ref, w_ref, u_ref, zs_ref, q_ref, k_ref, v_ref, za_ref, w_scr):
    @pl.when((pl.program_id(0) == 0) & (pl.program_id(1) == 0))
    def _():
        w_scr[...] = w_ref[...].astype(BF16)

    def emit(rows, tile, acc):
        lo = tile * PROJ_COLS
        if lo < OFF_ZS:
            u_ref[rows, lo - OFF_U:lo - OFF_U + PROJ_COLS] = acc.astype(BF16)
        elif lo < OFF_Q:
            zs_ref[rows, lo - OFF_ZS:lo - OFF_ZS + PROJ_COLS] = _silu(acc).astype(BF16)
        elif lo < OFF_K:
            q_ref[rows, lo - OFF_Q:lo - OFF_Q + PROJ_COLS] = (
                acc * (1.0 / math.sqrt(HEAD_DIM))).astype(BF16)
        elif lo < OFF_ZA:
            k_ref[rows, :] = acc[:, :D_KV].astype(BF16)
            v_ref[rows, :] = acc[:, D_KV:].astype(BF16)
        else:
            za_ref[rows, lo - OFF_ZA:lo - OFF_ZA + PROJ_COLS] = _silu(acc).astype(BF16)

    gain = g_ref[...]
    for half in range(TOK_IN // NORM_ROWS):
        rows = slice(half * NORM_ROWS, (half + 1) * NORM_ROWS)
        x = x_ref[rows, :]
        ms = jnp.mean(x * x, axis=-1, keepdims=True)
        hn = (x * lax.rsqrt(ms + EPS) * gain).astype(BF16)
        for tile in range(D_IN // PROJ_COLS):
            cols = slice(tile * PROJ_COLS, (tile + 1) * PROJ_COLS)
            emit(rows, tile, jnp.dot(hn, w_scr[:, cols], preferred_element_type=F32))


def _in_proj(x, g, w_in):
    bsz, seq, _ = x.shape
    tok = lambda width: pl.BlockSpec((None, TOK_IN, width), lambda b, i: (b, i, 0))
    widths = (D_SSM, D_SSM, D_ATTN, D_KV, D_KV, D_ATTN)
    return pl.pallas_call(
        _in_proj_kernel,
        grid=(bsz, seq // TOK_IN),
        in_specs=[
            tok(D_MODEL),
            pl.BlockSpec((1, D_MODEL), lambda b, i: (0, 0)),
            pl.BlockSpec((D_MODEL, D_IN), lambda b, i: (0, 0), pipeline_mode=pl.Buffered(1)),
        ],
        out_specs=tuple(tok(w) for w in widths),
        out_shape=tuple(jax.ShapeDtypeStruct((bsz, seq, w), BF16) for w in widths),
        scratch_shapes=[pltpu.VMEM((D_MODEL, D_IN), BF16)],
        compiler_params=pltpu.CompilerParams(
            dimension_semantics=("arbitrary", "arbitrary"), vmem_limit_bytes=VMEM_LIMIT),
        name="in_proj",
    )(x, g, w_in)


def _ssm_kernel(u_ref, perm_ref, permt_ref, win_re_ref, win_im_ref, l2re_ref, l2im_ref,
                wout_re_ref, wout_im_ref, wdir_ref, d_ref, wglu_ref, bglu_ref, o_ref,
                sre_ref, sim_ref, st_ref, y_ref,
                g0_ref, ut0_ref, g1_ref, ut1_ref, *, bsz):
    rows = bsz * T_SCAN
    half = rows // 2
    step = pl.program_id(0)

    @pl.when(step == 0)
    def _():
        sre_ref[...] = jnp.zeros_like(sre_ref)
        sim_ref[...] = jnp.zeros_like(sim_ref)
        g1_ref[...] = jnp.zeros_like(g1_ref)
        ut1_ref[...] = jnp.zeros_like(ut1_ref)

    def body(write, read):
        g_w, ut_w = write
        g_r, ut_r = read

        groups = bsz // PERM_BATCH
        grows = PERM_BATCH * T_SCAN
        permuted = [
            jnp.dot(perm_ref[...], u_ref[g * PERM_BATCH:(g + 1) * PERM_BATCH].reshape(grows, D_SSM),
                    preferred_element_type=F32).reshape(T_SCAN, 1, PERM_BATCH, D_SSM)
            for g in range(groups)]
        u_new = jnp.concatenate(permuted, axis=1).reshape(rows, D_SSM).astype(BF16)
        ut_w[...] = u_new
        u_before = jnp.concatenate([ut_r[rows - bsz:, :], u_new[half:rows - bsz]], axis=0)
        for j in range(N_SLABS):
            ch = slice(j * SLAB_CH, (j + 1) * SLAB_CH)
            drive_in = jnp.concatenate([u_before[:, ch], u_new[:half, ch]], axis=1)
            for c in range(SLAB_STATE // SCAN_COLS):
                lo = j * SLAB_STATE + c * SCAN_COLS
                cre = slice(c * SCAN_COLS, (c + 1) * SCAN_COLS)
                cim = slice(SLAB_STATE + c * SCAN_COLS, SLAB_STATE + (c + 1) * SCAN_COLS)
                l2r = jnp.broadcast_to(l2re_ref[:, lo:lo + SCAN_COLS], (bsz, SCAN_COLS))
                l2i = jnp.broadcast_to(l2im_ref[:, lo:lo + SCAN_COLS], (bsz, SCAN_COLS))
                sr = sre_ref[:, lo:lo + SCAN_COLS]
                si = sim_ref[:, lo:lo + SCAN_COLS]
                for piece in range(half // PIECE_ROWS):
                    rk = slice(piece * PIECE_ROWS, (piece + 1) * PIECE_ROWS)
                    g_w[j, rk, cre] = jnp.dot(drive_in[rk], win_re_ref[j, :, cre],
                                              preferred_element_type=F32)
                    g_w[j, rk, cim] = jnp.dot(drive_in[rk], win_im_ref[j, :, cre],
                                              preferred_element_type=F32)
                    for k in range(rk.start // bsz, rk.stop // bsz):
                        r = slice(k * bsz, (k + 1) * bsz)
                        sr, si = (l2r * sr - l2i * si + g_r[j, r, cre],
                                  l2r * si + l2i * sr + g_r[j, r, cim])
                        st_ref[j, r, cre] = sr.astype(BF16)
                        st_ref[j, r, cim] = si.astype(BF16)
                    yk = (jnp.dot(st_ref[j, rk, cre], wout_re_ref[j, cre, :],
                                  preferred_element_type=F32)
                          + jnp.dot(st_ref[j, rk, cim], wout_im_ref[j, cre, :],
                                    preferred_element_type=F32))
                    odd_rows = slice(half + rk.start, half + rk.stop)
                    if c == 0:
                        y_ref[rk, ch] = yk[:, :SLAB_CH]
                        y_ref[odd_rows, ch] = yk[:, SLAB_CH:] + jnp.dot(
                            ut_r[odd_rows, ch], wdir_ref[j], preferred_element_type=F32)
                    else:
                        y_ref[rk, ch] += yk[:, :SLAB_CH]
                        y_ref[odd_rows, ch] += yk[:, SLAB_CH:]
                sre_ref[:, lo:lo + SCAN_COLS] = sr
                sim_ref[:, lo:lo + SCAN_COLS] = si

        y = y_ref[...] + d_ref[...] * ut_r[...].astype(F32)
        g = jax.nn.gelu(y)
        gate = jnp.dot(g.astype(BF16), wglu_ref[...], preferred_element_type=F32) + bglu_ref[...]
        out_tb = (g * jax.nn.sigmoid(gate)).reshape(T_SCAN, groups, PERM_BATCH, D_SSM)
        for grp in range(groups):
            out_g = out_tb[:, grp].reshape(grows, D_SSM).astype(BF16)
            out_bt = jnp.dot(permt_ref[...], out_g, preferred_element_type=F32).astype(BF16)
            o_ref[grp * PERM_BATCH:(grp + 1) * PERM_BATCH] = out_bt.reshape(
                PERM_BATCH, T_SCAN, D_SSM)

    _by_parity(step, body, (g0_ref, ut0_ref), (g1_ref, ut1_ref))


def _row_permutation():
    rows = PERM_BATCH * T_SCAN
    perm = np.zeros((rows, rows), np.float32)
    t, b = np.meshgrid(np.arange(T_SCAN), np.arange(PERM_BATCH), indexing="ij")
    dst = (t % 2) * (rows // 2) + (t // 2) * PERM_BATCH + b
    perm[dst.ravel(), (b * T_SCAN + t).ravel()] = 1.0
    return perm


def _ssm(u, params, d, wglu, bglu):
    bsz, seq, _ = u.shape
    rows = T_SCAN * bsz
    half = rows // 2
    nblk = seq // T_SCAN
    assert bsz % PERM_BATCH == 0
    perm = _row_permutation()
    full = lambda shape: pl.BlockSpec(shape, lambda f: (0,) * len(shape))
    g_scratch = pltpu.VMEM((N_SLABS, half, 2 * SLAB_STATE), F32)
    ut_scratch = pltpu.VMEM((rows, D_SSM), BF16)
    return pl.pallas_call(
        functools.partial(_ssm_kernel, bsz=bsz),
        grid=(nblk + 1,),
        in_specs=[
            pl.BlockSpec((bsz, T_SCAN, D_SSM), lambda f: (0, jnp.minimum(f, nblk - 1), 0)),
            full(perm.shape), full(perm.shape),
            full((N_SLABS, 2 * SLAB_CH, SLAB_STATE)), full((N_SLABS, 2 * SLAB_CH, SLAB_STATE)),
            full((1, N_STATE)), full((1, N_STATE)),
            full((N_SLABS, SLAB_STATE, 2 * SLAB_CH)), full((N_SLABS, SLAB_STATE, 2 * SLAB_CH)),
            full((N_SLABS, SLAB_CH, SLAB_CH)),
            full((1, D_SSM)), full((D_SSM, D_SSM)), full((1, D_SSM)),
        ],
        out_specs=pl.BlockSpec((bsz, T_SCAN, D_SSM), lambda f: (0, jnp.maximum(f - 1, 0), 0)),
        out_shape=jax.ShapeDtypeStruct((bsz, seq, D_SSM), BF16),
        scratch_shapes=[
            pltpu.VMEM((bsz, N_STATE), F32),
            pltpu.VMEM((bsz, N_STATE), F32),
            pltpu.VMEM((N_SLABS, half, 2 * SLAB_STATE), BF16),
            pltpu.VMEM((rows, D_SSM), F32),
            g_scratch, ut_scratch, g_scratch, ut_scratch,
        ],
        compiler_params=pltpu.CompilerParams(
            dimension_semantics=("arbitrary",), vmem_limit_bytes=VMEM_LIMIT),
        name="s5_scan",
    )(u, jnp.asarray(perm, BF16), jnp.asarray(perm.T, BF16), *params, d, wglu, bglu)


def _attention_pieces(first, sink_ref, q_ref, k_ref, v_ref, kp_ref, vp_ref, za_ref, bias_ref,
                      half_ref, a_ref):
    low_q = lax.broadcasted_iota(jnp.int32, (WINDOW, LANES), 1) < HEAD_DIM
    kv_cache = {}

    def swap_halves(t):
        packed = pltpu.bitcast(t, jnp.uint32)
        return pltpu.bitcast(pltpu.roll(packed, HEAD_DIM, axis=1), BF16)

    def keys_values(jj, g):
        if jj not in kv_cache:
            rows = slice(jj * WINDOW, (jj + 1) * WINDOW)
            if jj == 0:
                kp, vp = kp_ref[...], vp_ref[...]
            else:
                prev = slice((jj - 1) * WINDOW, jj * WINDOW)
                kp, vp = k_ref[prev, :], v_ref[prev, :]
            kk = jnp.concatenate([kp, k_ref[rows, :]], axis=0)
            vv = jnp.concatenate([vp, v_ref[rows, :]], axis=0)
            kv_cache[jj] = ((kk, swap_halves(kk)), (vv, swap_halves(vv)))
        if (jj, g) not in kv_cache:
            keep_even, keep_odd = half_ref[0], half_ref[1]
            k2, v2 = kv_cache[jj]
            kcat = jnp.concatenate([k2[g] * keep_even, k2[1 - g] * keep_odd], axis=0)
            vcat = jnp.concatenate(
                [jnp.concatenate([v2[g] * keep_even, keep_even], axis=1),
                 jnp.concatenate([v2[1 - g] * keep_odd, keep_odd], axis=1)], axis=0)
            kv_cache[(jj, g)] = (kcat, vcat)
        return kv_cache[(jj, g)]

    items = [(jj, pair) for jj in range(TOK_OUT // WINDOW) for pair in range(N_HEADS // 2)]
    scores, probs = {}, {}

    def where(i):
        jj, pair = items[i]
        return (jj, pair, slice(jj * WINDOW, (jj + 1) * WINDOW),
                slice(pair * LANES, (pair + 1) * LANES))

    def stage_scores(i):
        jj, pair, rows, lanes = where(i)
        kcat, _ = keys_values(jj, pair // (Q_PER_KV // 2))
        s = lax.dot_general(q_ref[rows, lanes], kcat, (((1,), (1,)), ((), ())),
                            preferred_element_type=F32)
        scores[i] = s + (bias_ref[pair + (N_HEADS // 2) * first] if jj == 0 else bias_ref[pair])

    def stage_softmax(i):
        _, pair, _, _ = where(i)
        s = scores.pop(i)
        sink_e, sink_o = sink_ref[2 * pair], sink_ref[2 * pair + 1]
        m_e = jnp.maximum(jnp.max(s[:, :2 * WINDOW], axis=-1, keepdims=True), sink_e)
        m_o = jnp.maximum(jnp.max(s[:, 2 * WINDOW:], axis=-1, keepdims=True), sink_o)
        e = jnp.concatenate([jnp.exp(s[:, :2 * WINDOW] - m_e),
                             jnp.exp(s[:, 2 * WINDOW:] - m_o)], axis=1).astype(BF16)
        probs[i] = (e, jnp.where(low_q, jnp.exp(sink_e - m_e), jnp.exp(sink_o - m_o)))

    def stage_output(i):
        jj, pair, rows, lanes = where(i)
        _, vcat = keys_values(jj, pair // (Q_PER_KV // 2))
        e, sink_term = probs.pop(i)
        o = jnp.dot(e, vcat, preferred_element_type=F32)
        den = o[:, LANES:] + sink_term
        a_ref[rows, lanes] = (o[:, :LANES] / den * za_ref[rows, lanes].astype(F32)).astype(BF16)

    def slot(it):
        if it < len(items):
            stage_scores(it)
        if 0 <= it - 1 < len(items):
            stage_softmax(it - 1)
        if 0 <= it - 2 < len(items):
            stage_output(it - 2)

    return [functools.partial(slot, it) for it in range(len(items) + 2)]


def _attn_out_kernel(sink_ref, x_ref, p_ref, ssm_ref, zs_ref, q_ref, k_ref, v_ref, kp_ref, vp_ref,
                     za_ref, bias_ref, half_ref, wout_f32, pg_ref, wgate_f32, bgate_ref, wproj_f32,
                     o_ref, wout_ref, wgate_ref, wproj_ref, a0_ref, a1_ref,
                     *, blocks_per_seq, n_blocks):
    step = pl.program_id(0)
    seq_block = jnp.minimum(step, n_blocks - 1) % blocks_per_seq
    first = (seq_block == 0).astype(jnp.int32)

    @pl.when(step == 0)
    def _():
        wout_ref[...] = wout_f32[...].astype(BF16)
        wgate_ref[...] = wgate_f32[...].astype(BF16)
        wproj_ref[...] = wproj_f32[...].astype(BF16)
        a1_ref[...] = jnp.zeros_like(a1_ref)

    def body(a_w, a_r):
        pieces = _attention_pieces(first, sink_ref, q_ref, k_ref, v_ref, kp_ref, vp_ref, za_ref,
                                   bias_ref, half_ref, a_w)
        n_chunks = D_MODEL // OUT_COLS
        per_chunk = (len(pieces) - 2) // (2 * n_chunks)
        pieces = iter(pieces)

        def run_pieces(count=per_chunk):
            for piece in (list(pieces) if count is None else [p for _, p in zip(range(count), pieces)]):
                piece()

        run_pieces(2)

        s5 = (ssm_ref[...].astype(F32) * zs_ref[...].astype(F32)).astype(BF16)
        a_prev = a_r[...]
        mixed = []
        for n in range(n_chunks):
            cols = slice(n * OUT_COLS, (n + 1) * OUT_COLS)
            mixed.append(jnp.dot(s5, wout_ref[:D_SSM, cols], preferred_element_type=F32)
                         + jnp.dot(a_prev, wout_ref[D_SSM:, cols], preferred_element_type=F32))
            run_pieces()
        mixed = jnp.concatenate(mixed, axis=1)
        ms = jnp.mean(mixed * mixed, axis=-1, keepdims=True)
        h1 = x_ref[...] + mixed * lax.rsqrt(ms + EPS) * pg_ref[...]
        h1b = h1.astype(BF16)
        pb = p_ref[...].astype(BF16)
        for n in range(n_chunks):
            cols = slice(n * OUT_COLS, (n + 1) * OUT_COLS)
            gate = jax.nn.sigmoid(
                jnp.dot(h1b, wgate_ref[:, cols], preferred_element_type=F32) + bgate_ref[:, cols])
            ple = jnp.dot(pb, wproj_ref[:, cols], preferred_element_type=F32)
            o_ref[:, cols] = h1[:, cols] + gate * ple
            run_pieces()
        run_pieces(None)

    _by_parity(step, body, a0_ref, a1_ref)


def _attn_bias_table():
    q_idx = np.arange(WINDOW)[:, None]
    s_idx = np.arange(2 * WINDOW)[None, :]
    dist = q_idx + WINDOW - s_idx
    valid = (dist >= 0) & (dist < WINDOW)
    slopes = np.exp2(-8.0 * (np.arange(N_HEADS, dtype=np.float32) + 1.0) / N_HEADS).astype(np.float32)
    bias = -slopes[:, None, None] * dist.astype(np.float32)[None]
    normal = np.where(valid[None], bias, -np.inf)
    first = np.where((valid & (s_idx >= WINDOW))[None], bias, -np.inf)
    paired = lambda t: t.reshape(N_HEADS // 2, 2, WINDOW, 2 * WINDOW).transpose(0, 2, 1, 3).reshape(
        N_HEADS // 2, WINDOW, 4 * WINDOW)
    return np.concatenate([paired(normal), paired(first)], axis=0).astype(np.float32)


def _attn_out(sinks, x, p, ssm, zs, q, k, v, za, wout, pg, wgate, bgate, wproj):
    bsz, seq, _ = x.shape
    sub = TOK_OUT // WINDOW
    per_seq = seq // TOK_OUT
    n_blocks = bsz * per_seq

    def cur(f):
        f = jnp.minimum(f, n_blocks - 1)
        return f // per_seq, f % per_seq

    def prv(f):
        f = jnp.maximum(f - 1, 0)
        return f // per_seq, f % per_seq

    front = lambda width: pl.BlockSpec((None, TOK_OUT, width), lambda f: (*cur(f), 0))
    back = lambda width: pl.BlockSpec((None, TOK_OUT, width), lambda f: (*prv(f), 0))
    prev_kv = pl.BlockSpec(
        (None, WINDOW, D_KV),
        lambda f: (cur(f)[0], jnp.maximum(cur(f)[1] * sub - 1, 0), 0))
    full = lambda shape: pl.BlockSpec(shape, lambda f: (0,) * len(shape))
    once = lambda shape: pl.BlockSpec(shape, lambda f: (0,) * len(shape),
                                      pipeline_mode=pl.Buffered(1))
    bias = jnp.asarray(_attn_bias_table())
    lane_is_even = np.broadcast_to(np.arange(LANES) < HEAD_DIM, (2 * WINDOW, LANES))
    half = jnp.asarray(np.stack([lane_is_even, ~lane_is_even]), BF16)
    a_scratch = pltpu.VMEM((TOK_OUT, D_ATTN), BF16)
    return pl.pallas_call(
        functools.partial(_attn_out_kernel, blocks_per_seq=per_seq, n_blocks=n_blocks),
        grid=(n_blocks + 1,),
        in_specs=[
            pl.BlockSpec(memory_space=pltpu.SMEM),
            back(D_MODEL), back(D_PLE), back(D_SSM), back(D_SSM), front(D_ATTN),
            front(D_KV), front(D_KV), prev_kv, prev_kv, front(D_ATTN),
            once(bias.shape), once(half.shape),
            once((D_MODEL, D_MODEL)), full((1, D_MODEL)),
            once((D_MODEL, D_MODEL)), full((1, D_MODEL)), once((D_PLE, D_MODEL)),
        ],
        out_specs=back(D_MODEL),
        out_shape=jax.ShapeDtypeStruct((bsz, seq, D_MODEL), F32),
        scratch_shapes=[pltpu.VMEM((D_MODEL, D_MODEL), BF16), pltpu.VMEM((D_MODEL, D_MODEL), BF16),
                        pltpu.VMEM((D_PLE, D_MODEL), BF16), a_scratch, a_scratch],
        compiler_params=pltpu.CompilerParams(
            dimension_semantics=("arbitrary",), vmem_limit_bytes=VMEM_LIMIT),
        name="attn_out",
    )(sinks, x, p, ssm, zs, q, k, v, k, v, za, bias, half, wout, pg, wgate, bgate, wproj)


def _block_diag(blocks):
    s, n, a, b = blocks.shape
    eye = jnp.eye(n, dtype=blocks.dtype)
    return jnp.einsum('snab,nm->snamb', blocks, eye).reshape(s, n * a, n * b)


def _ssm_params(lam_re, lam_im, log_step, b_re, b_im, c_re, c_im):
    lr, li = lam_re.astype(F32), lam_im.astype(F32)
    step = jnp.exp(log_step.astype(F32))[:, None]
    mag = jnp.exp(lr * step)
    lbr, lbi = mag * jnp.cos(li * step), mag * jnp.sin(li * step)
    den = lr * lr + li * li
    fr = ((lbr - 1.0) * lr + lbi * li) / den
    fi = (lbi * lr - (lbr - 1.0) * li) / den
    br, bi = b_re.astype(F32), b_im.astype(F32)
    bbr = fr[..., None] * br - fi[..., None] * bi
    bbi = fr[..., None] * bi + fi[..., None] * br
    cr, ci = c_re.astype(F32), c_im.astype(F32)
    l2r, l2i = lbr * lbr - lbi * lbi, 2.0 * lbr * lbi
    lbbr = lbr[..., None] * bbr - lbi[..., None] * bbi
    lbbi = lbr[..., None] * bbi + lbi[..., None] * bbr
    clr = cr * lbr[:, None, :] - ci * lbi[:, None, :]
    cli = cr * lbi[:, None, :] + ci * lbr[:, None, :]
    cb = jnp.sum(cr[:, :, :, None] * bbr[:, None, :, :] - ci[:, :, :, None] * bbi[:, None, :, :],
                 axis=2)
    gps = SLAB_CH // SSM_GROUP_CH

    def in_w(part):
        blocks = part.reshape(N_SLABS, gps, SSM_STATE, SSM_GROUP_CH).transpose(0, 1, 3, 2)
        return _block_diag(blocks)

    def out_w(part):
        blocks = part.reshape(N_SLABS, gps, SSM_GROUP_CH, SSM_STATE).transpose(0, 1, 3, 2)
        return _block_diag(blocks)

    cat = lambda parts, axis: jnp.concatenate(parts, axis=axis).astype(BF16)
    win_re = cat([in_w(lbbr), in_w(bbr)], 1)
    win_im = cat([in_w(lbbi), in_w(bbi)], 1)
    wout_re = cat([out_w(cr), out_w(clr)], 2)
    wout_im = cat([out_w(-ci), out_w(-cli)], 2)
    wdir = _block_diag(cb.reshape(N_SLABS, gps, SSM_GROUP_CH, SSM_GROUP_CH)
                       .transpose(0, 1, 3, 2)).astype(BF16)
    return (win_re, win_im, l2r.reshape(1, N_STATE), l2i.reshape(1, N_STATE),
            wout_re, wout_im, wdir)


def _layer(h, p, pre_norm_g, w_in, ssm_lam_re, ssm_lam_im, ssm_log_step, ssm_b_re, ssm_b_im,
           ssm_c_re, ssm_c_im, ssm_d, ssm_w_glu, ssm_b_glu, attn_sinks, w_out, post_norm_g,
           pl_w_proj, pl_w_gate, pl_b_gate):
    row = lambda a: a.astype(F32).reshape(1, -1)
    u, zs, q, k, v, za = _in_proj(h, row(pre_norm_g), w_in.astype(F32))
    params = _ssm_params(
        ssm_lam_re, ssm_lam_im, ssm_log_step, ssm_b_re, ssm_b_im, ssm_c_re, ssm_c_im)
    ssm = _ssm(u, params, row(ssm_d), ssm_w_glu.astype(BF16), row(ssm_b_glu))
    return _attn_out(attn_sinks.astype(F32), h, p, ssm, zs, q, k, v, za,
                     w_out.astype(F32), row(post_norm_g), pl_w_gate.astype(F32), row(pl_b_gate),
                     pl_w_proj.astype(F32))


def kernel(x, p, pre_norm_g, w_in, ssm_lam_re, ssm_lam_im, ssm_log_step, ssm_b_re, ssm_b_im,
           ssm_c_re, ssm_c_im, ssm_d, ssm_w_glu, ssm_b_glu, attn_sinks, w_out, post_norm_g,
           pl_w_proj, pl_w_gate, pl_b_gate):
    h = x
    for i in range(p.shape[0]):
        h = _layer(h, p[i], pre_norm_g[i], w_in[i], ssm_lam_re[i], ssm_lam_im[i], ssm_log_step[i],
                   ssm_b_re[i], ssm_b_im[i], ssm_c_re[i], ssm_c_im[i], ssm_d[i], ssm_w_glu[i],
                   ssm_b_glu[i], attn_sinks[i], w_out[i], post_norm_g[i], pl_w_proj[i],
                   pl_w_gate[i], pl_b_gate[i])
    return h
```

```python
import functools
import math

import numpy as np
import jax
import jax.numpy as jnp
from jax import lax
from jax.experimental import pallas as pl
from jax.experimental.pallas import tpu as pltpu

F32 = jnp.float32
BF16 = jnp.bfloat16

D_MODEL = 1024
D_SSM = 512
D_ATTN = 512
SSM_GROUP_CH = 16
SSM_GROUPS = D_SSM // SSM_GROUP_CH
SSM_STATE = 64
N_STATE = SSM_GROUPS * SSM_STATE
HEAD_DIM = 64
N_HEADS = D_ATTN // HEAD_DIM
KV_HEADS = 2
Q_PER_KV = N_HEADS // KV_HEADS
D_KV = KV_HEADS * HEAD_DIM
WINDOW = 128
D_PLE = 256
EPS = 1e-6

LANES = 128
assert D_KV == LANES and 2 * HEAD_DIM == LANES

OFF_U, OFF_ZS, OFF_Q, OFF_K, OFF_V, OFF_ZA, D_IN = 0, 512, 1024, 1536, 1664, 1792, 2304

SLAB_CH = LANES
N_SLABS = D_SSM // SLAB_CH
SLAB_STATE = (SLAB_CH // SSM_GROUP_CH) * SSM_STATE
SCAN_COLS = 256

TOK_IN = 1024
NORM_ROWS = 512
T_SCAN = 32
PIECE_ROWS = 128
PERM_BATCH = 8
TOK_OUT = 512
PROJ_COLS = 256
OUT_COLS = 256
VMEM_LIMIT = 48 * 1024 * 1024


def _silu(z):
    return z * jax.nn.sigmoid(z)


def _pipelined(step, n_blocks, body, even_bufs, odd_bufs):
    @pl.when(step == 0)
    def _():
        body(even_bufs, odd_bufs, True, False)

    @pl.when((step > 0) & (step < n_blocks) & (step % 2 == 0))
    def _():
        body(even_bufs, odd_bufs, True, True)

    @pl.when((step < n_blocks) & (step % 2 == 1))
    def _():
        body(odd_bufs, even_bufs, True, True)

    @pl.when(step == n_blocks)
    def _():
        last_read = odd_bufs if n_blocks % 2 == 0 else even_bufs
        body(None, last_read, False, True)


def _in_proj_kernel(x_ref, g_ref, w_ref, u_ref, zs_ref, q_ref, k_ref, v_ref, za_ref, w_scr):
    @pl.when((pl.program_id(0) == 0) & (pl.program_id(1) == 0))
    def _():
        w_scr[...] = w_ref[...].astype(BF16)

    def emit(rows, tile, acc):
        lo = tile * PROJ_COLS
        if lo < OFF_ZS:
            u_ref[rows, lo - OFF_U:lo - OFF_U + PROJ_COLS] = acc.astype(BF16)
        elif lo < OFF_Q:
            zs_ref[rows, lo - OFF_ZS:lo - OFF_ZS + PROJ_COLS] = _silu(acc).astype(BF16)
        elif lo < OFF_K:
            q_ref[rows, lo - OFF_Q:lo - OFF_Q + PROJ_COLS] = (
                acc * (1.0 / math.sqrt(HEAD_DIM))).astype(BF16)
        elif lo < OFF_ZA:
            k_ref[rows, :] = acc[:, :D_KV].astype(BF16)
            v_ref[rows, :] = acc[:, D_KV:].astype(BF16)
        else:
            za_ref[rows, lo - OFF_ZA:lo - OFF_ZA + PROJ_COLS] = _silu(acc).astype(BF16)

    gain = g_ref[...]
    for half in range(TOK_IN // NORM_ROWS):
        rows = slice(half * NORM_ROWS, (half + 1) * NORM_ROWS)
        x = x_ref[rows, :]
        ms = jnp.mean(x * x, axis=-1, keepdims=True)
        hn = (x * lax.rsqrt(ms + EPS) * gain).astype(BF16)
        for tile in range(D_IN // PROJ_COLS):
            cols = slice(tile * PROJ_COLS, (tile + 1) * PROJ_COLS)
            emit(rows, tile, jnp.dot(hn, w_scr[:, cols], preferred_element_type=F32))


def _in_proj(x, g, w_in):
    bsz, seq, _ = x.shape
    tok = lambda width: pl.BlockSpec((None, TOK_IN, width), lambda b, i: (b, i, 0))
    widths = (D_SSM, D_SSM, D_ATTN, D_KV, D_KV, D_ATTN)
    return pl.pallas_call(
        _in_proj_kernel,
        grid=(bsz, seq // TOK_IN),
        in_specs=[
            tok(D_MODEL),
            pl.BlockSpec((1, D_MODEL), lambda b, i: (0, 0)),
            pl.BlockSpec((D_MODEL, D_IN), lambda b, i: (0, 0), pipeline_mode=pl.Buffered(1)),
        ],
        out_specs=tuple(tok(w) for w in widths),
        out_shape=tuple(jax.ShapeDtypeStruct((bsz, seq, w), BF16) for w in widths),
        scratch_shapes=[pltpu.VMEM((D_MODEL, D_IN), BF16)],
        compiler_params=pltpu.CompilerParams(
            dimension_semantics=("arbitrary", "arbitrary"), vmem_limit_bytes=VMEM_LIMIT),
        name="in_proj",
    )(x, g, w_in)


def _ssm_kernel(u_ref, perm_ref, permt_ref, win_re_ref, win_im_ref, l2re_ref, l2im_ref,
                wout_re_ref, wout_im_ref, wdir_ref, d_ref, wglu_ref, bglu_ref, o_ref,
                sre_ref, sim_ref, st_ref, y_ref,
                g0_ref, ut0_ref, g1_ref, ut1_ref, *, bsz, n_blocks):
    rows = bsz * T_SCAN
    half = rows // 2
    step = pl.program_id(0)

    @pl.when(step == 0)
    def _():
        sre_ref[...] = jnp.zeros_like(sre_ref)
        sim_ref[...] = jnp.zeros_like(sim_ref)

    def body(write, read, front, back):
        g_w, ut_w = write if front else (None, None)
        g_r, ut_r = read if back else (None, None)
        groups = bsz // PERM_BATCH
        grows = PERM_BATCH * T_SCAN
        if front:
            permuted = [
                jnp.dot(perm_ref[...],
                        u_ref[g * PERM_BATCH:(g + 1) * PERM_BATCH].reshape(grows, D_SSM),
                        preferred_element_type=F32).reshape(T_SCAN, 1, PERM_BATCH, D_SSM)
                for g in range(groups)]
            u_new = jnp.concatenate(permuted, axis=1).reshape(rows, D_SSM).astype(BF16)
            ut_w[...] = u_new
            u_last = ut_r[rows - bsz:, :] if back else jnp.zeros((bsz, D_SSM), BF16)
            u_before = jnp.concatenate([u_last, u_new[half:rows - bsz]], axis=0)
        for j in range(N_SLABS):
            ch = slice(j * SLAB_CH, (j + 1) * SLAB_CH)
            if front:
                drive_in = jnp.concatenate([u_before[:, ch], u_new[:half, ch]], axis=1)
            for c in range(SLAB_STATE // SCAN_COLS):
                lo = j * SLAB_STATE + c * SCAN_COLS
                cre = slice(c * SCAN_COLS, (c + 1) * SCAN_COLS)
                cim = slice(SLAB_STATE + c * SCAN_COLS, SLAB_STATE + (c + 1) * SCAN_COLS)
                if back:
                    l2r = jnp.broadcast_to(l2re_ref[:, lo:lo + SCAN_COLS], (bsz, SCAN_COLS))
                    l2i = jnp.broadcast_to(l2im_ref[:, lo:lo + SCAN_COLS], (bsz, SCAN_COLS))
                    sr = sre_ref[:, lo:lo + SCAN_COLS]
                    si = sim_ref[:, lo:lo + SCAN_COLS]
                for piece in range(half // PIECE_ROWS):
                    rk = slice(piece * PIECE_ROWS, (piece + 1) * PIECE_ROWS)
                    if front:
                        g_w[j, rk, cre] = jnp.dot(drive_in[rk], win_re_ref[j, :, cre],
                                                  preferred_element_type=F32)
                        g_w[j, rk, cim] = jnp.dot(drive_in[rk], win_im_ref[j, :, cre],
                                                  preferred_element_type=F32)
                    if not back:
                        continue
                    for k in range(rk.start // bsz, rk.stop // bsz):
                        r = slice(k * bsz, (k + 1) * bsz)
                        sr, si = (l2r * sr - l2i * si + g_r[j, r, cre],
                                  l2r * si + l2i * sr + g_r[j, r, cim])
                        st_ref[j, r, cre] = sr.astype(BF16)
                        st_ref[j, r, cim] = si.astype(BF16)
                    yk = (jnp.dot(st_ref[j, rk, cre], wout_re_ref[j, cre, :],
                                  preferred_element_type=F32)
                          + jnp.dot(st_ref[j, rk, cim], wout_im_ref[j, cre, :],
                                    preferred_element_type=F32))
                    odd_rows = slice(half + rk.start, half + rk.stop)
                    if c == 0:
                        y_ref[rk, ch] = yk[:, :SLAB_CH]
                        y_ref[odd_rows, ch] = yk[:, SLAB_CH:] + jnp.dot(
                            ut_r[odd_rows, ch], wdir_ref[j], preferred_element_type=F32)
                    else:
                        y_ref[rk, ch] += yk[:, :SLAB_CH]
                        y_ref[odd_rows, ch] += yk[:, SLAB_CH:]
                if back:
                    sre_ref[:, lo:lo + SCAN_COLS] = sr
                    sim_ref[:, lo:lo + SCAN_COLS] = si
        if not back:
            return

        y = y_ref[...] + d_ref[...] * ut_r[...].astype(F32)
        g = jax.nn.gelu(y)
        gate = jnp.dot(g.astype(BF16), wglu_ref[...], preferred_element_type=F32) + bglu_ref[...]
        out_tb = (g * jax.nn.sigmoid(gate)).reshape(T_SCAN, groups, PERM_BATCH, D_SSM)
        for grp in range(groups):
            out_g = out_tb[:, grp].reshape(grows, D_SSM).astype(BF16)
            out_bt = jnp.dot(permt_ref[...], out_g, preferred_element_type=F32).astype(BF16)
            o_ref[grp * PERM_BATCH:(grp + 1) * PERM_BATCH] = out_bt.reshape(
                PERM_BATCH, T_SCAN, D_SSM)

    _pipelined(step, n_blocks, body, (g0_ref, ut0_ref), (g1_ref, ut1_ref))


def _row_permutation():
    rows = PERM_BATCH * T_SCAN
    perm = np.zeros((rows, rows), np.float32)
    t, b = np.meshgrid(np.arange(T_SCAN), np.arange(PERM_BATCH), indexing="ij")
    dst = (t % 2) * (rows // 2) + (t // 2) * PERM_BATCH + b
    perm[dst.ravel(), (b * T_SCAN + t).ravel()] = 1.0
    return perm


def _ssm(u, params, d, wglu, bglu):
    bsz, seq, _ = u.shape
    rows = T_SCAN * bsz
    half = rows // 2
    nblk = seq // T_SCAN
    assert bsz % PERM_BATCH == 0
    perm = _row_permutation()
    full = lambda shape: pl.BlockSpec(shape, lambda f: (0,) * len(shape))
    g_scratch = pltpu.VMEM((N_SLABS, half, 2 * SLAB_STATE), F32)
    ut_scratch = pltpu.VMEM((rows, D_SSM), BF16)
    return pl.pallas_call(
        functools.partial(_ssm_kernel, bsz=bsz, n_blocks=nblk),
        grid=(nblk + 1,),
        in_specs=[
            pl.BlockSpec((bsz, T_SCAN, D_SSM), lambda f: (0, jnp.minimum(f, nblk - 1), 0)),
            full(perm.shape), full(perm.shape),
            full((N_SLABS, 2 * SLAB_CH, SLAB_STATE)), full((N_SLABS, 2 * SLAB_CH, SLAB_STATE)),
            full((1, N_STATE)), full((1, N_STATE)),
            full((N_SLABS, SLAB_STATE, 2 * SLAB_CH)), full((N_SLABS, SLAB_STATE, 2 * SLAB_CH)),
            full((N_SLABS, SLAB_CH, SLAB_CH)),
            full((1, D_SSM)), full((D_SSM, D_SSM)), full((1, D_SSM)),
        ],
        out_specs=pl.BlockSpec((bsz, T_SCAN, D_SSM), lambda f: (0, jnp.maximum(f - 1, 0), 0)),
        out_shape=jax.ShapeDtypeStruct((bsz, seq, D_SSM), BF16),
        scratch_shapes=[
            pltpu.VMEM((bsz, N_STATE), F32),
            pltpu.VMEM((bsz, N_STATE), F32),
            pltpu.VMEM((N_SLABS, half, 2 * SLAB_STATE), BF16),
            pltpu.VMEM((rows, D_SSM), F32),
            g_scratch, ut_scratch, g_scratch, ut_scratch,
        ],
        compiler_params=pltpu.CompilerParams(
            dimension_semantics=("arbitrary",), vmem_limit_bytes=VMEM_LIMIT),
        name="s5_scan",
    )(u, jnp.asarray(perm, BF16), jnp.asarray(perm.T, BF16), *params, d, wglu, bglu)


def _attention_pieces(first, sink_ref, q_ref, k_ref, v_ref, kp_ref, vp_ref, za_ref, bias_ref,
                      half_ref, a_ref):
    low_q = lax.broadcasted_iota(jnp.int32, (WINDOW, LANES), 1) < HEAD_DIM
    kv_cache = {}

    def swap_halves(t):
        packed = pltpu.bitcast(t, jnp.uint32)
        return pltpu.bitcast(pltpu.roll(packed, HEAD_DIM, axis=1), BF16)

    def keys_values(jj, g):
        if jj not in kv_cache:
            rows = slice(jj * WINDOW, (jj + 1) * WINDOW)
            if jj == 0:
                kp, vp = kp_ref[...], vp_ref[...]
            else:
                prev = slice((jj - 1) * WINDOW, jj * WINDOW)
                kp, vp = k_ref[prev, :], v_ref[prev, :]
            kk = jnp.concatenate([kp, k_ref[rows, :]], axis=0)
            vv = jnp.concatenate([vp, v_ref[rows, :]], axis=0)
            kv_cache[jj] = ((kk, swap_halves(kk)), (vv, swap_halves(vv)))
        if (jj, g) not in kv_cache:
            keep_even, keep_odd = half_ref[0], half_ref[1]
            k2, v2 = kv_cache[jj]
            kcat = jnp.concatenate([k2[g] * keep_even, k2[1 - g] * keep_odd], axis=0)
            vcat = jnp.concatenate(
                [jnp.concatenate([v2[g] * keep_even, keep_even], axis=1),
                 jnp.concatenate([v2[1 - g] * keep_odd, keep_odd], axis=1)], axis=0)
            kv_cache[(jj, g)] = (kcat, vcat)
        return kv_cache[(jj, g)]

    items = [(jj, pair) for jj in range(TOK_OUT // WINDOW) for pair in range(N_HEADS // 2)]
    scores, probs = {}, {}

    def where(i):
        jj, pair = items[i]
        return (jj, pair, slice(jj * WINDOW, (jj + 1) * WINDOW),
                slice(pair * LANES, (pair + 1) * LANES))

    def stage_scores(i):
        jj, pair, rows, lanes = where(i)
        kcat, _ = keys_values(jj, pair // (Q_PER_KV // 2))
        s = lax.dot_general(q_ref[rows, lanes], kcat, (((1,), (1,)), ((), ())),
                            preferred_element_type=F32)
        scores[i] = s + (bias_ref[pair + (N_HEADS // 2) * first] if jj == 0 else bias_ref[pair])

    def stage_softmax(i):
        _, pair, _, _ = where(i)
        s = scores.pop(i)
        sink_e, sink_o = sink_ref[2 * pair], sink_ref[2 * pair + 1]
        m_e = jnp.maximum(jnp.max(s[:, :2 * WINDOW], axis=-1, keepdims=True), sink_e)
        m_o = jnp.maximum(jnp.max(s[:, 2 * WINDOW:], axis=-1, keepdims=True), sink_o)
        e = jnp.concatenate([jnp.exp(s[:, :2 * WINDOW] - m_e),
                             jnp.exp(s[:, 2 * WINDOW:] - m_o)], axis=1).astype(BF16)
        probs[i] = (e, jnp.where(low_q, jnp.exp(sink_e - m_e), jnp.exp(sink_o - m_o)))

    def stage_output(i):
        jj, pair, rows, lanes = where(i)
        _, vcat = keys_values(jj, pair // (Q_PER_KV // 2))
        e, sink_term = probs.pop(i)
        o = jnp.dot(e, vcat, preferred_element_type=F32)
        den = o[:, LANES:] + sink_term
        a_ref[rows, lanes] = (o[:, :LANES] / den * za_ref[rows, lanes].astype(F32)).astype(BF16)

    def slot(it):
        if it < len(items):
            stage_scores(it)
        if 0 <= it - 1 < len(items):
            stage_softmax(it - 1)
        if 0 <= it - 2 < len(items):
            stage_output(it - 2)

    return [functools.partial(slot, it) for it in range(len(items) + 2)]


def _attn_out_kernel(sink_ref, x_ref, p_ref, ssm_ref, zs_ref, q_ref, k_ref, v_ref, kp_ref, vp_ref,
                     za_ref, bias_ref, half_ref, wout_f32, pg_ref, wgate_f32, bgate_ref, wproj_f32,
                     o_ref, wout_ref, wgate_ref, wproj_ref, a0_ref, a1_ref,
                     *, blocks_per_seq, n_blocks):
    step = pl.program_id(0)
    seq_block = jnp.minimum(step, n_blocks - 1) % blocks_per_seq
    first = (seq_block == 0).astype(jnp.int32)

    @pl.when(step == 0)
    def _():
        wout_ref[...] = wout_f32[...].astype(BF16)
        wgate_ref[...] = wgate_f32[...].astype(BF16)
        wproj_ref[...] = wproj_f32[...].astype(BF16)

    def body(a_w, a_r, front, back):
        pieces = []
        if front:
            pieces = _attention_pieces(first, sink_ref, q_ref, k_ref, v_ref, kp_ref, vp_ref,
                                       za_ref, bias_ref, half_ref, a_w)
        n_chunks = D_MODEL // OUT_COLS
        per_chunk = max(len(pieces) - 2, 0) // (2 * n_chunks)
        pieces = iter(pieces)

        def run_pieces(count=per_chunk):
            for piece in (list(pieces) if count is None else [p for _, p in zip(range(count), pieces)]):
                piece()

        if not back:
            run_pieces(None)
            return
        run_pieces(2)

        s5 = (ssm_ref[...].astype(F32) * zs_ref[...].astype(F32)).astype(BF16)
        a_prev = a_r[...]
        mixed = []
        for n in range(n_chunks):
            cols = slice(n * OUT_COLS, (n + 1) * OUT_COLS)
            mixed.append(jnp.dot(s5, wout_ref[:D_SSM, cols], preferred_element_type=F32)
                         + jnp.dot(a_prev, wout_ref[D_SSM:, cols], preferred_element_type=F32))
            run_pieces()
        mixed = jnp.concatenate(mixed, axis=1)
        ms = jnp.mean(mixed * mixed, axis=-1, keepdims=True)
        h1 = x_ref[...] + mixed * lax.rsqrt(ms + EPS) * pg_ref[...]
        h1b = h1.astype(BF16)
        pb = p_ref[...].astype(BF16)
        for n in range(n_chunks):
            cols = slice(n * OUT_COLS, (n + 1) * OUT_COLS)
            gate = jax.nn.sigmoid(
                jnp.dot(h1b, wgate_ref[:, cols], preferred_element_type=F32) + bgate_ref[:, cols])
            ple = jnp.dot(pb, wproj_ref[:, cols], preferred_element_type=F32)
            o_ref[:, cols] = h1[:, cols] + gate * ple
            run_pieces()
        run_pieces(None)

    _pipelined(step, n_blocks, body, a0_ref, a1_ref)


def _attn_bias_table():
    q_idx = np.arange(WINDOW)[:, None]
    s_idx = np.arange(2 * WINDOW)[None, :]
    dist = q_idx + WINDOW - s_idx
    valid = (dist >= 0) & (dist < WINDOW)
    slopes = np.exp2(-8.0 * (np.arange(N_HEADS, dtype=np.float32) + 1.0) / N_HEADS).astype(np.float32)
    bias = -slopes[:, None, None] * dist.astype(np.float32)[None]
    normal = np.where(valid[None], bias, -np.inf)
    first = np.where((valid & (s_idx >= WINDOW))[None], bias, -np.inf)
    paired = lambda t: t.reshape(N_HEADS // 2, 2, WINDOW, 2 * WINDOW).transpose(0, 2, 1, 3).reshape(
        N_HEADS // 2, WINDOW, 4 * WINDOW)
    return np.concatenate([paired(normal), paired(first)], axis=0).astype(np.float32)


def _attn_out(sinks, x, p, ssm, zs, q, k, v, za, wout, pg, wgate, bgate, wproj):
    bsz, seq, _ = x.shape
    sub = TOK_OUT // WINDOW
    per_seq = seq // TOK_OUT
    n_blocks = bsz * per_seq

    def cur(f):
        f = jnp.minimum(f, n_blocks - 1)
        return f // per_seq, f % per_seq

    def prv(f):
        f = jnp.maximum(f - 1, 0)
        return f // per_seq, f % per_seq

    front = lambda width: pl.BlockSpec((None, TOK_OUT, width), lambda f: (*cur(f), 0))
    back = lambda width: pl.BlockSpec((None, TOK_OUT, width), lambda f: (*prv(f), 0))
    prev_kv = pl.BlockSpec(
        (None, WINDOW, D_KV),
        lambda f: (cur(f)[0], jnp.maximum(cur(f)[1] * sub - 1, 0), 0))
    full = lambda shape: pl.BlockSpec(shape, lambda f: (0,) * len(shape))
    once = lambda shape: pl.BlockSpec(shape, lambda f: (0,) * len(shape),
                                      pipeline_mode=pl.Buffered(1))
    bias = jnp.asarray(_attn_bias_table())
    lane_is_even = np.broadcast_to(np.arange(LANES) < HEAD_DIM, (2 * WINDOW, LANES))
    half = jnp.asarray(np.stack([lane_is_even, ~lane_is_even]), BF16)
    a_scratch = pltpu.VMEM((TOK_OUT, D_ATTN), BF16)
    return pl.pallas_call(
        functools.partial(_attn_out_kernel, blocks_per_seq=per_seq, n_blocks=n_blocks),
        grid=(n_blocks + 1,),
        in_specs=[
            pl.BlockSpec(memory_space=pltpu.SMEM),
            back(D_MODEL), back(D_PLE), back(D_SSM), back(D_SSM), front(D_ATTN),
            front(D_KV), front(D_KV), prev_kv, prev_kv, front(D_ATTN),
            once(bias.shape), once(half.shape),
            once((D_MODEL, D_MODEL)), full((1, D_MODEL)),
            once((D_MODEL, D_MODEL)), full((1, D_MODEL)), once((D_PLE, D_MODEL)),
        ],
        out_specs=back(D_MODEL),
        out_shape=jax.ShapeDtypeStruct((bsz, seq, D_MODEL), F32),
        scratch_shapes=[pltpu.VMEM((D_MODEL, D_MODEL), BF16), pltpu.VMEM((D_MODEL, D_MODEL), BF16),
                        pltpu.VMEM((D_PLE, D_MODEL), BF16), a_scratch, a_scratch],
        compiler_params=pltpu.CompilerParams(
            dimension_semantics=("arbitrary",), vmem_limit_bytes=VMEM_LIMIT),
        name="attn_out",
    )(sinks, x, p, ssm, zs, q, k, v, k, v, za, bias, half, wout, pg, wgate, bgate, wproj)


def _block_diag(blocks):
    s, n, a, b = blocks.shape
    eye = jnp.eye(n, dtype=blocks.dtype)
    return jnp.einsum('snab,nm->snamb', blocks, eye).reshape(s, n * a, n * b)


def _ssm_params(lam_re, lam_im, log_step, b_re, b_im, c_re, c_im):
    lr, li = lam_re.astype(F32), lam_im.astype(F32)
    step = jnp.exp(log_step.astype(F32))[:, None]
    mag = jnp.exp(lr * step)
    lbr, lbi = mag * jnp.cos(li * step), mag * jnp.sin(li * step)
    den = lr * lr + li * li
    fr = ((lbr - 1.0) * lr + lbi * li) / den
    fi = (lbi * lr - (lbr - 1.0) * li) / den
    br, bi = b_re.astype(F32), b_im.astype(F32)
    bbr = fr[..., None] * br - fi[..., None] * bi
    bbi = fr[..., None] * bi + fi[..., None] * br
    cr, ci = c_re.astype(F32), c_im.astype(F32)
    l2r, l2i = lbr * lbr - lbi * lbi, 2.0 * lbr * lbi
    lbbr = lbr[..., None] * bbr - lbi[..., None] * bbi
    lbbi = lbr[..., None] * bbi + lbi[..., None] * bbr
    clr = cr * lbr[:, None, :] - ci * lbi[:, None, :]
    cli = cr * lbi[:, None, :] + ci * lbr[:, None, :]
    cb = jnp.sum(cr[:, :, :, None] * bbr[:, None, :, :] - ci[:, :, :, None] * bbi[:, None, :, :],
                 axis=2)
    gps = SLAB_CH // SSM_GROUP_CH

    def in_w(part):
        blocks = part.reshape(N_SLABS, gps, SSM_STATE, SSM_GROUP_CH).transpose(0, 1, 3, 2)
        return _block_diag(blocks)

    def out_w(part):
        blocks = part.reshape(N_SLABS, gps, SSM_GROUP_CH, SSM_STATE).transpose(0, 1, 3, 2)
        return _block_diag(blocks)

    cat = lambda parts, axis: jnp.concatenate(parts, axis=axis).astype(BF16)
    win_re = cat([in_w(lbbr), in_w(bbr)], 1)
    win_im = cat([in_w(lbbi), in_w(bbi)], 1)
    wout_re = cat([out_w(cr), out_w(clr)], 2)
    wout_im = cat([out_w(-ci), out_w(-cli)], 2)
    wdir = _block_diag(cb.reshape(N_SLABS, gps, SSM_GROUP_CH, SSM_GROUP_CH)
                       .transpose(0, 1, 3, 2)).astype(BF16)
    return (win_re, win_im, l2r.reshape(1, N_STATE), l2i.reshape(1, N_STATE),
            wout_re, wout_im, wdir)


def _layer(h, p, pre_norm_g, w_in, ssm_lam_re, ssm_lam_im, ssm_log_step, ssm_b_re, ssm_b_im,
           ssm_c_re, ssm_c_im, ssm_d, ssm_w_glu, ssm_b_glu, attn_sinks, w_out, post_norm_g,
           pl_w_proj, pl_w_gate, pl_b_gate):
    row = lambda a: a.astype(F32).reshape(1, -1)
    u, zs, q, k, v, za = _in_proj(h, row(pre_norm_g), w_in.astype(F32))
    params = _ssm_params(
        ssm_lam_re, ssm_lam_im, ssm_log_step, ssm_b_re, ssm_b_im, ssm_c_re, ssm_c_im)
    ssm = _ssm(u, params, row(ssm_d), ssm_w_glu.astype(BF16), row(ssm_b_glu))
    return _attn_out(attn_sinks.astype(F32), h, p, ssm, zs, q, k, v, za,
                     w_out.astype(F32), row(post_norm_g), pl_w_gate.astype(F32), row(pl_b_gate),
                     pl_w_proj.astype(F32))


def kernel(x, p, pre_norm_g, w_in, ssm_lam_re, ssm_lam_im, ssm_log_step, ssm_b_re, ssm_b_im,
           ssm_c_re, ssm_c_im, ssm_d, ssm_w_glu, ssm_b_glu, attn_sinks, w_out, post_norm_g,
           pl_w_proj, pl_w_gate, pl_b_gate):
    h = x
    for i in range(p.shape[0]):
        h = _layer(h, p[i], pre_norm_g[i], w_in[i], ssm_lam_re[i], ssm_lam_im[i], ssm_log_step[i],
                   ssm_b_re[i], ssm_b_im[i], ssm_c_re[i], ssm_c_im[i], ssm_d[i], ssm_w_glu[i],
                   ssm_b_glu[i], attn_sinks[i], w_out[i], post_norm_g[i], pl_w_proj[i],
                   pl_w_gate[i], pl_b_gate[i])
    return h
```

```python
import functools
import math

import numpy as np
import jax
import jax.numpy as jnp
from jax import lax
from jax.experimental import pallas as pl
from jax.experimental.pallas import tpu as pltpu

F32 = jnp.float32
BF16 = jnp.bfloat16

D_MODEL = 1024
D_SSM = 512
D_ATTN = 512
SSM_GROUP_CH = 16
SSM_GROUPS = D_SSM // SSM_GROUP_CH
SSM_STATE = 64
N_STATE = SSM_GROUPS * SSM_STATE
HEAD_DIM = 64
N_HEADS = D_ATTN // HEAD_DIM
KV_HEADS = 2
Q_PER_KV = N_HEADS // KV_HEADS
D_KV = KV_HEADS * HEAD_DIM
WINDOW = 128
D_PLE = 256
EPS = 1e-6

LANES = 128
assert D_KV == LANES and 2 * HEAD_DIM == LANES

OFF_U, OFF_ZS, OFF_Q, OFF_K, OFF_V, OFF_ZA, D_IN = 0, 512, 1024, 1536, 1664, 1792, 2304

SLAB_CH = LANES
N_SLABS = D_SSM // SLAB_CH
SLAB_STATE = (SLAB_CH // SSM_GROUP_CH) * SSM_STATE
SCAN_COLS = 256

TOK_IN = 1024
NORM_ROWS = 256
T_SCAN = 32
PIECE_ROWS = 128
PERM_BATCH = 8
TOK_OUT = 512
PROJ_COLS = 256
OUT_COLS = 256
VMEM_LIMIT = 48 * 1024 * 1024


def _silu(z):
    return z * jax.nn.sigmoid(z)


def _pipelined(step, n_blocks, body, even_bufs, odd_bufs):
    @pl.when(step == 0)
    def _():
        body(even_bufs, odd_bufs, True, False)

    @pl.when((step > 0) & (step < n_blocks) & (step % 2 == 0))
    def _():
        body(even_bufs, odd_bufs, True, True)

    @pl.when((step < n_blocks) & (step % 2 == 1))
    def _():
        body(odd_bufs, even_bufs, True, True)

    @pl.when(step == n_blocks)
    def _():
        last_read = odd_bufs if n_blocks % 2 == 0 else even_bufs
        body(None, last_read, False, True)


def _in_proj_kernel(x_ref, g_ref, w_ref, u_ref, zs_ref, q_ref, k_ref, v_ref, za_ref, w_scr):
    @pl.when((pl.program_id(0) == 0) & (pl.program_id(1) == 0))
    def _():
        w_scr[...] = w_ref[...].astype(BF16)

    def emit(rows, tile, acc):
        lo = tile * PROJ_COLS
        if lo < OFF_ZS:
            u_ref[rows, lo - OFF_U:lo - OFF_U + PROJ_COLS] = acc.astype(BF16)
        elif lo < OFF_Q:
            zs_ref[rows, lo - OFF_ZS:lo - OFF_ZS + PROJ_COLS] = _silu(acc).astype(BF16)
        elif lo < OFF_K:
            q_ref[rows, lo - OFF_Q:lo - OFF_Q + PROJ_COLS] = (
                acc * (1.0 / math.sqrt(HEAD_DIM))).astype(BF16)
        elif lo < OFF_ZA:
            k_ref[rows, :] = acc[:, :D_KV].astype(BF16)
            v_ref[rows, :] = acc[:, D_KV:].astype(BF16)
        else:
            za_ref[rows, lo - OFF_ZA:lo - OFF_ZA + PROJ_COLS] = _silu(acc).astype(BF16)

    gain = g_ref[...]
    for piece in range(TOK_IN // NORM_ROWS):
        rows = slice(piece * NORM_ROWS, (piece + 1) * NORM_ROWS)
        x = x_ref[rows, :]
        ms = jnp.mean(x * x, axis=-1, keepdims=True)
        hn = (x * lax.rsqrt(ms + EPS) * gain).astype(BF16)
        for tile in range(D_IN // PROJ_COLS):
            cols = slice(tile * PROJ_COLS, (tile + 1) * PROJ_COLS)
            emit(rows, tile, jnp.dot(hn, w_scr[:, cols], preferred_element_type=F32))


def _in_proj(x, g, w_in):
    bsz, seq, _ = x.shape
    tok = lambda width: pl.BlockSpec((None, TOK_IN, width), lambda b, i: (b, i, 0))
    widths = (D_SSM, D_SSM, D_ATTN, D_KV, D_KV, D_ATTN)
    return pl.pallas_call(
        _in_proj_kernel,
        grid=(bsz, seq // TOK_IN),
        in_specs=[
            tok(D_MODEL),
            pl.BlockSpec((1, D_MODEL), lambda b, i: (0, 0)),
            pl.BlockSpec((D_MODEL, D_IN), lambda b, i: (0, 0), pipeline_mode=pl.Buffered(1)),
        ],
        out_specs=tuple(tok(w) for w in widths),
        out_shape=tuple(jax.ShapeDtypeStruct((bsz, seq, w), BF16) for w in widths),
        scratch_shapes=[pltpu.VMEM((D_MODEL, D_IN), BF16)],
        compiler_params=pltpu.CompilerParams(
            dimension_semantics=("arbitrary", "arbitrary"), vmem_limit_bytes=VMEM_LIMIT),
        name="in_proj",
    )(x, g, w_in)


def _ssm_kernel(u_ref, perm_ref, permt_ref, win_re_ref, win_im_ref, l2re_ref, l2im_ref,
                wout_re_ref, wout_im_ref, wdir_ref, d_ref, wglu_f32, bglu_ref, o_ref,
                wglu_ref, sre_ref, sim_ref, st_ref, y_ref,
                g0_ref, ut0_ref, g1_ref, ut1_ref, *, bsz, n_blocks):
    rows = bsz * T_SCAN
    half = rows // 2
    step = pl.program_id(0)

    @pl.when(step == 0)
    def _():
        wglu_ref[...] = wglu_f32[...].astype(BF16)
        sre_ref[...] = jnp.zeros_like(sre_ref)
        sim_ref[...] = jnp.zeros_like(sim_ref)

    def body(write, read, front, back):
        g_w, ut_w = write if front else (None, None)
        g_r, ut_r = read if back else (None, None)
        groups = bsz // PERM_BATCH
        grows = PERM_BATCH * T_SCAN
        if front:
            permuted = [
                jnp.dot(perm_ref[...],
                        u_ref[g * PERM_BATCH:(g + 1) * PERM_BATCH].reshape(grows, D_SSM),
                        preferred_element_type=F32).reshape(T_SCAN, 1, PERM_BATCH, D_SSM)
                for g in range(groups)]
            u_new = jnp.concatenate(permuted, axis=1).reshape(rows, D_SSM).astype(BF16)
            ut_w[...] = u_new
            u_last = ut_r[rows - bsz:, :] if back else jnp.zeros((bsz, D_SSM), BF16)
            u_before = jnp.concatenate([u_last, u_new[half:rows - bsz]], axis=0)
        for j in range(N_SLABS):
            ch = slice(j * SLAB_CH, (j + 1) * SLAB_CH)
            if front:
                drive_in = jnp.concatenate([u_before[:, ch], u_new[:half, ch]], axis=1)
            for c in range(SLAB_STATE // SCAN_COLS):
                lo = j * SLAB_STATE + c * SCAN_COLS
                cre = slice(c * SCAN_COLS, (c + 1) * SCAN_COLS)
                cim = slice(SLAB_STATE + c * SCAN_COLS, SLAB_STATE + (c + 1) * SCAN_COLS)
                if back:
                    l2r = jnp.broadcast_to(l2re_ref[:, lo:lo + SCAN_COLS], (bsz, SCAN_COLS))
                    l2i = jnp.broadcast_to(l2im_ref[:, lo:lo + SCAN_COLS], (bsz, SCAN_COLS))
                    sr = sre_ref[:, lo:lo + SCAN_COLS]
                    si = sim_ref[:, lo:lo + SCAN_COLS]
                for piece in range(half // PIECE_ROWS):
                    rk = slice(piece * PIECE_ROWS, (piece + 1) * PIECE_ROWS)
                    if front:
                        g_w[j, rk, cre] = jnp.dot(drive_in[rk], win_re_ref[j, :, cre],
                                                  preferred_element_type=F32)
                        g_w[j, rk, cim] = jnp.dot(drive_in[rk], win_im_ref[j, :, cre],
                                                  preferred_element_type=F32)
                    if not back:
                        continue
                    for k in range(rk.start // bsz, rk.stop // bsz):
                        r = slice(k * bsz, (k + 1) * bsz)
                        sr, si = (l2r * sr - l2i * si + g_r[j, r, cre],
                                  l2r * si + l2i * sr + g_r[j, r, cim])
                        st_ref[j, r, cre] = sr.astype(BF16)
                        st_ref[j, r, cim] = si.astype(BF16)
                    yk = (jnp.dot(st_ref[j, rk, cre], wout_re_ref[j, cre, :],
                                  preferred_element_type=F32)
                          + jnp.dot(st_ref[j, rk, cim], wout_im_ref[j, cre, :],
                                    preferred_element_type=F32))
                    odd_rows = slice(half + rk.start, half + rk.stop)
                    if c == 0:
                        y_ref[rk, ch] = yk[:, :SLAB_CH]
                        y_ref[odd_rows, ch] = yk[:, SLAB_CH:] + jnp.dot(
                            ut_r[odd_rows, ch], wdir_ref[j], preferred_element_type=F32)
                    else:
                        y_ref[rk, ch] += yk[:, :SLAB_CH]
                        y_ref[odd_rows, ch] += yk[:, SLAB_CH:]
                if back:
                    sre_ref[:, lo:lo + SCAN_COLS] = sr
                    sim_ref[:, lo:lo + SCAN_COLS] = si
        if not back:
            return

        y = y_ref[...] + d_ref[...] * ut_r[...].astype(F32)
        g = jax.nn.gelu(y)
        gate = jnp.dot(g.astype(BF16), wglu_ref[...], preferred_element_type=F32) + bglu_ref[...]
        out_tb = (g * jax.nn.sigmoid(gate)).reshape(T_SCAN, groups, PERM_BATCH, D_SSM)
        for grp in range(groups):
            out_g = out_tb[:, grp].reshape(grows, D_SSM).astype(BF16)
            out_bt = jnp.dot(permt_ref[...], out_g, preferred_element_type=F32).astype(BF16)
            o_ref[grp * PERM_BATCH:(grp + 1) * PERM_BATCH] = out_bt.reshape(
                PERM_BATCH, T_SCAN, D_SSM)

    _pipelined(step, n_blocks, body, (g0_ref, ut0_ref), (g1_ref, ut1_ref))


def _row_permutation():
    rows = PERM_BATCH * T_SCAN
    perm = np.zeros((rows, rows), np.float32)
    t, b = np.meshgrid(np.arange(T_SCAN), np.arange(PERM_BATCH), indexing="ij")
    dst = (t % 2) * (rows // 2) + (t // 2) * PERM_BATCH + b
    perm[dst.ravel(), (b * T_SCAN + t).ravel()] = 1.0
    return perm


def _ssm(u, params, d, wglu, bglu):
    bsz, seq, _ = u.shape
    rows = T_SCAN * bsz
    half = rows // 2
    nblk = seq // T_SCAN
    assert bsz % PERM_BATCH == 0
    perm = _row_permutation()
    full = lambda shape: pl.BlockSpec(shape, lambda f: (0,) * len(shape))
    g_scratch = pltpu.VMEM((N_SLABS, half, 2 * SLAB_STATE), F32)
    ut_scratch = pltpu.VMEM((rows, D_SSM), BF16)
    return pl.pallas_call(
        functools.partial(_ssm_kernel, bsz=bsz, n_blocks=nblk),
        grid=(nblk + 1,),
        in_specs=[
            pl.BlockSpec((bsz, T_SCAN, D_SSM), lambda f: (0, jnp.minimum(f, nblk - 1), 0)),
            full(perm.shape), full(perm.shape),
            full((N_SLABS, 2 * SLAB_CH, SLAB_STATE)), full((N_SLABS, 2 * SLAB_CH, SLAB_STATE)),
            full((1, N_STATE)), full((1, N_STATE)),
            full((N_SLABS, SLAB_STATE, 2 * SLAB_CH)), full((N_SLABS, SLAB_STATE, 2 * SLAB_CH)),
            full((N_SLABS, SLAB_CH, SLAB_CH)),
            full((1, D_SSM)),
            pl.BlockSpec((D_SSM, D_SSM), lambda f: (0, 0), pipeline_mode=pl.Buffered(1)),
            full((1, D_SSM)),
        ],
        out_specs=pl.BlockSpec((bsz, T_SCAN, D_SSM), lambda f: (0, jnp.maximum(f - 1, 0), 0)),
        out_shape=jax.ShapeDtypeStruct((bsz, seq, D_SSM), BF16),
        scratch_shapes=[
            pltpu.VMEM((D_SSM, D_SSM), BF16),
            pltpu.VMEM((bsz, N_STATE), F32),
            pltpu.VMEM((bsz, N_STATE), F32),
            pltpu.VMEM((N_SLABS, half, 2 * SLAB_STATE), BF16),
            pltpu.VMEM((rows, D_SSM), F32),
            g_scratch, ut_scratch, g_scratch, ut_scratch,
        ],
        compiler_params=pltpu.CompilerParams(
            dimension_semantics=("arbitrary",), vmem_limit_bytes=VMEM_LIMIT),
        name="s5_scan",
    )(u, jnp.asarray(perm, BF16), jnp.asarray(perm.T, BF16), *params, d, wglu, bglu)


def _attention_pieces(first, sink_ref, q_ref, k_ref, v_ref, kp_ref, vp_ref, za_ref, bias_ref,
                      half_ref, a_ref):
    low_q = lax.broadcasted_iota(jnp.int32, (WINDOW, LANES), 1) < HEAD_DIM
    kv_cache = {}

    def swap_halves(t):
        packed = pltpu.bitcast(t, jnp.uint32)
        return pltpu.bitcast(pltpu.roll(packed, HEAD_DIM, axis=1), BF16)

    def keys_values(jj, g):
        if jj not in kv_cache:
            rows = slice(jj * WINDOW, (jj + 1) * WINDOW)
            if jj == 0:
                kp, vp = kp_ref[...], vp_ref[...]
            else:
                prev = slice((jj - 1) * WINDOW, jj * WINDOW)
                kp, vp = k_ref[prev, :], v_ref[prev, :]
            kk = jnp.concatenate([kp, k_ref[rows, :]], axis=0)
            vv = jnp.concatenate([vp, v_ref[rows, :]], axis=0)
            kv_cache[jj] = ((kk, swap_halves(kk)), (vv, swap_halves(vv)))
        if (jj, g) not in kv_cache:
            keep_even, keep_odd = half_ref[0], half_ref[1]
            k2, v2 = kv_cache[jj]
            kcat = jnp.concatenate([k2[g] * keep_even, k2[1 - g] * keep_odd], axis=0)
            vcat = jnp.concatenate(
                [jnp.concatenate([v2[g] * keep_even, keep_even], axis=1),
                 jnp.concatenate([v2[1 - g] * keep_odd, keep_odd], axis=1)], axis=0)
            kv_cache[(jj, g)] = (kcat, vcat)
        return kv_cache[(jj, g)]

    items = [(jj, pair) for jj in range(TOK_OUT // WINDOW) for pair in range(N_HEADS // 2)]
    scores, probs = {}, {}

    def where(i):
        jj, pair = items[i]
        return (jj, pair, slice(jj * WINDOW, (jj + 1) * WINDOW),
                slice(pair * LANES, (pair + 1) * LANES))

    def stage_scores(i):
        jj, pair, rows, lanes = where(i)
        kcat, _ = keys_values(jj, pair // (Q_PER_KV // 2))
        s = lax.dot_general(q_ref[rows, lanes], kcat, (((1,), (1,)), ((), ())),
                            preferred_element_type=F32)
        scores[i] = s + (bias_ref[pair + (N_HEADS // 2) * first] if jj == 0 else bias_ref[pair])

    def stage_softmax(i):
        _, pair, _, _ = where(i)
        s = scores.pop(i)
        sink_e, sink_o = sink_ref[2 * pair], sink_ref[2 * pair + 1]
        m_e = jnp.maximum(jnp.max(s[:, :2 * WINDOW], axis=-1, keepdims=True), sink_e)
        m_o = jnp.maximum(jnp.max(s[:, 2 * WINDOW:], axis=-1, keepdims=True), sink_o)
        e = jnp.concatenate([jnp.exp(s[:, :2 * WINDOW] - m_e),
                             jnp.exp(s[:, 2 * WINDOW:] - m_o)], axis=1).astype(BF16)
        probs[i] = (e, jnp.where(low_q, jnp.exp(sink_e - m_e), jnp.exp(sink_o - m_o)))

    def stage_output(i):
        jj, pair, rows, lanes = where(i)
        _, vcat = keys_values(jj, pair // (Q_PER_KV // 2))
        e, sink_term = probs.pop(i)
        o = jnp.dot(e, vcat, preferred_element_type=F32)
        den = o[:, LANES:] + sink_term
        a_ref[rows, lanes] = (o[:, :LANES] / den * za_ref[rows, lanes].astype(F32)).astype(BF16)

    def slot(it):
        if it < len(items):
            stage_scores(it)
        if 0 <= it - 1 < len(items):
            stage_softmax(it - 1)
        if 0 <= it - 2 < len(items):
            stage_output(it - 2)

    return [functools.partial(slot, it) for it in range(len(items) + 2)]


def _attn_out_kernel(sink_ref, x_ref, p_ref, ssm_ref, zs_ref, q_ref, k_ref, v_ref, kp_ref, vp_ref,
                     za_ref, bias_ref, half_ref, wout_f32, pg_ref, wgate_f32, bgate_ref, wproj_f32,
                     o_ref, wout_ref, wgate_ref, wproj_ref, a0_ref, a1_ref,
                     *, blocks_per_seq, n_blocks):
    step = pl.program_id(0)
    seq_block = jnp.minimum(step, n_blocks - 1) % blocks_per_seq
    first = (seq_block == 0).astype(jnp.int32)

    @pl.when(step == 0)
    def _():
        wout_ref[...] = wout_f32[...].astype(BF16)
        wgate_ref[...] = wgate_f32[...].astype(BF16)
        wproj_ref[...] = wproj_f32[...].astype(BF16)

    def body(a_w, a_r, front, back):
        pieces = []
        if front:
            pieces = _attention_pieces(first, sink_ref, q_ref, k_ref, v_ref, kp_ref, vp_ref,
                                       za_ref, bias_ref, half_ref, a_w)
        n_chunks = D_MODEL // OUT_COLS
        per_chunk = max(len(pieces) - 2, 0) // (2 * n_chunks)
        pieces = iter(pieces)

        def run_pieces(count=per_chunk):
            for piece in (list(pieces) if count is None else [p for _, p in zip(range(count), pieces)]):
                piece()

        if not back:
            run_pieces(None)
            return
        run_pieces(2)

        s5 = (ssm_ref[...].astype(F32) * zs_ref[...].astype(F32)).astype(BF16)
        a_prev = a_r[...]
        mixed = []
        for n in range(n_chunks):
            cols = slice(n * OUT_COLS, (n + 1) * OUT_COLS)
            mixed.append(jnp.dot(s5, wout_ref[:D_SSM, cols], preferred_element_type=F32)
                         + jnp.dot(a_prev, wout_ref[D_SSM:, cols], preferred_element_type=F32))
            run_pieces()
        mixed = jnp.concatenate(mixed, axis=1)
        ms = jnp.mean(mixed * mixed, axis=-1, keepdims=True)
        h1 = x_ref[...] + mixed * lax.rsqrt(ms + EPS) * pg_ref[...]
        h1b = h1.astype(BF16)
        pb = p_ref[...].astype(BF16)
        for n in range(n_chunks):
            cols = slice(n * OUT_COLS, (n + 1) * OUT_COLS)
            gate = jax.nn.sigmoid(
                jnp.dot(h1b, wgate_ref[:, cols], preferred_element_type=F32) + bgate_ref[:, cols])
            ple = jnp.dot(pb, wproj_ref[:, cols], preferred_element_type=F32)
            o_ref[:, cols] = h1[:, cols] + gate * ple
            run_pieces()
        run_pieces(None)

    _pipelined(step, n_blocks, body, a0_ref, a1_ref)


def _attn_bias_table():
    q_idx = np.arange(WINDOW)[:, None]
    s_idx = np.arange(2 * WINDOW)[None, :]
    dist = q_idx + WINDOW - s_idx
    valid = (dist >= 0) & (dist < WINDOW)
    slopes = np.exp2(-8.0 * (np.arange(N_HEADS, dtype=np.float32) + 1.0) / N_HEADS).astype(np.float32)
    bias = -slopes[:, None, None] * dist.astype(np.float32)[None]
    normal = np.where(valid[None], bias, -np.inf)
    first = np.where((valid & (s_idx >= WINDOW))[None], bias, -np.inf)
    paired = lambda t: t.reshape(N_HEADS // 2, 2, WINDOW, 2 * WINDOW).transpose(0, 2, 1, 3).reshape(
        N_HEADS // 2, WINDOW, 4 * WINDOW)
    return np.concatenate([paired(normal), paired(first)], axis=0).astype(np.float32)


def _attn_out(sinks, x, p, ssm, zs, q, k, v, za, wout, pg, wgate, bgate, wproj):
    bsz, seq, _ = x.shape
    sub = TOK_OUT // WINDOW
    per_seq = seq // TOK_OUT
    n_blocks = bsz * per_seq

    def cur(f):
        f = jnp.minimum(f, n_blocks - 1)
        return f // per_seq, f % per_seq

    def prv(f):
        f = jnp.maximum(f - 1, 0)
        return f // per_seq, f % per_seq

    front = lambda width: pl.BlockSpec((None, TOK_OUT, width), lambda f: (*cur(f), 0))
    back = lambda width: pl.BlockSpec((None, TOK_OUT, width), lambda f: (*prv(f), 0))
    prev_kv = pl.BlockSpec(
        (None, WINDOW, D_KV),
        lambda f: (cur(f)[0], jnp.maximum(cur(f)[1] * sub - 1, 0), 0))
    full = lambda shape: pl.BlockSpec(shape, lambda f: (0,) * len(shape))
    once = lambda shape: pl.BlockSpec(shape, lambda f: (0,) * len(shape),
                                      pipeline_mode=pl.Buffered(1))
    bias = jnp.asarray(_attn_bias_table())
    lane_is_even = np.broadcast_to(np.arange(LANES) < HEAD_DIM, (2 * WINDOW, LANES))
    half = jnp.asarray(np.stack([lane_is_even, ~lane_is_even]), BF16)
    a_scratch = pltpu.VMEM((TOK_OUT, D_ATTN), BF16)
    return pl.pallas_call(
        functools.partial(_attn_out_kernel, blocks_per_seq=per_seq, n_blocks=n_blocks),
        grid=(n_blocks + 1,),
        in_specs=[
            pl.BlockSpec(memory_space=pltpu.SMEM),
            back(D_MODEL), back(D_PLE), back(D_SSM), back(D_SSM), front(D_ATTN),
            front(D_KV), front(D_KV), prev_kv, prev_kv, front(D_ATTN),
            once(bias.shape), once(half.shape),
            once((D_MODEL, D_MODEL)), full((1, D_MODEL)),
            once((D_MODEL, D_MODEL)), full((1, D_MODEL)), once((D_PLE, D_MODEL)),
        ],
        out_specs=back(D_MODEL),
        out_shape=jax.ShapeDtypeStruct((bsz, seq, D_MODEL), F32),
        scratch_shapes=[pltpu.VMEM((D_MODEL, D_MODEL), BF16), pltpu.VMEM((D_MODEL, D_MODEL), BF16),
                        pltpu.VMEM((D_PLE, D_MODEL), BF16), a_scratch, a_scratch],
        compiler_params=pltpu.CompilerParams(
            dimension_semantics=("arbitrary",), vmem_limit_bytes=VMEM_LIMIT),
        name="attn_out",
    )(sinks, x, p, ssm, zs, q, k, v, k, v, za, bias, half, wout, pg, wgate, bgate, wproj)


def _block_diag(blocks):
    s, n, a, b = blocks.shape
    eye = jnp.eye(n, dtype=blocks.dtype)
    return jnp.einsum('snab,nm->snamb', blocks, eye).reshape(s, n * a, n * b)


def _ssm_params(lam_re, lam_im, log_step, b_re, b_im, c_re, c_im):
    lr, li = lam_re.astype(F32), lam_im.astype(F32)
    step = jnp.exp(log_step.astype(F32))[:, None]
    mag = jnp.exp(lr * step)
    lbr, lbi = mag * jnp.cos(li * step), mag * jnp.sin(li * step)
    den = lr * lr + li * li
    fr = ((lbr - 1.0) * lr + lbi * li) / den
    fi = (lbi * lr - (lbr - 1.0) * li) / den
    br, bi = b_re.astype(F32), b_im.astype(F32)
    bbr = fr[..., None] * br - fi[..., None] * bi
    bbi = fr[..., None] * bi + fi[..., None] * br
    cr, ci = c_re.astype(F32), c_im.astype(F32)
    l2r, l2i = lbr * lbr - lbi * lbi, 2.0 * lbr * lbi
    lbbr = lbr[..., None] * bbr - lbi[..., None] * bbi
    lbbi = lbr[..., None] * bbi + lbi[..., None] * bbr
    clr = cr * lbr[:, None, :] - ci * lbi[:, None, :]
    cli = cr * lbi[:, None, :] + ci * lbr[:, None, :]
    cb = jnp.sum(cr[:, :, :, None] * bbr[:, None, :, :] - ci[:, :, :, None] * bbi[:, None, :, :],
                 axis=2)
    gps = SLAB_CH // SSM_GROUP_CH

    def in_w(part):
        blocks = part.reshape(N_SLABS, gps, SSM_STATE, SSM_GROUP_CH).transpose(0, 1, 3, 2)
        return _block_diag(blocks)

    def out_w(part):
        blocks = part.reshape(N_SLABS, gps, SSM_GROUP_CH, SSM_STATE).transpose(0, 1, 3, 2)
        return _block_diag(blocks)

    cat = lambda parts, axis: jnp.concatenate(parts, axis=axis).astype(BF16)
    win_re = cat([in_w(lbbr), in_w(bbr)], 1)
    win_im = cat([in_w(lbbi), in_w(bbi)], 1)
    wout_re = cat([out_w(cr), out_w(clr)], 2)
    wout_im = cat([out_w(-ci), out_w(-cli)], 2)
    wdir = _block_diag(cb.reshape(N_SLABS, gps, SSM_GROUP_CH, SSM_GROUP_CH)
                       .transpose(0, 1, 3, 2)).astype(BF16)
    return (win_re, win_im, l2r.reshape(1, N_STATE), l2i.reshape(1, N_STATE),
            wout_re, wout_im, wdir)


def _layer(h, p, pre_norm_g, w_in, ssm_lam_re, ssm_lam_im, ssm_log_step, ssm_b_re, ssm_b_im,
           ssm_c_re, ssm_c_im, ssm_d, ssm_w_glu, ssm_b_glu, attn_sinks, w_out, post_norm_g,
           pl_w_proj, pl_w_gate, pl_b_gate):
    row = lambda a: a.astype(F32).reshape(1, -1)
    u, zs, q, k, v, za = _in_proj(h, row(pre_norm_g), w_in.astype(F32))
    params = _ssm_params(
        ssm_lam_re, ssm_lam_im, ssm_log_step, ssm_b_re, ssm_b_im, ssm_c_re, ssm_c_im)
    ssm = _ssm(u, params, row(ssm_d), ssm_w_glu.astype(F32), row(ssm_b_glu))
    return _attn_out(attn_sinks.astype(F32), h, p, ssm, zs, q, k, v, za,
                     w_out.astype(F32), row(post_norm_g), pl_w_gate.astype(F32), row(pl_b_gate),
                     pl_w_proj.astype(F32))


def kernel(x, p, pre_norm_g, w_in, ssm_lam_re, ssm_lam_im, ssm_log_step, ssm_b_re, ssm_b_im,
           ssm_c_re, ssm_c_im, ssm_d, ssm_w_glu, ssm_b_glu, attn_sinks, w_out, post_norm_g,
           pl_w_proj, pl_w_gate, pl_b_gate):
    h = x
    for i in range(p.shape[0]):
        h = _layer(h, p[i], pre_norm_g[i], w_in[i], ssm_lam_re[i], ssm_lam_im[i], ssm_log_step[i],
                   ssm_b_re[i], ssm_b_im[i], ssm_c_re[i], ssm_c_im[i], ssm_d[i], ssm_w_glu[i],
                   ssm_b_glu[i], attn_sinks[i], w_out[i], post_norm_g[i], pl_w_proj[i],
                   pl_w_gate[i], pl_b_gate[i])
    return h
```

```python
import functools
import math

import numpy as np
import jax
import jax.numpy as jnp
from jax import lax
from jax.experimental import pallas as pl
from jax.experimental.pallas import tpu as pltpu

F32 = jnp.float32
BF16 = jnp.bfloat16

D_MODEL = 1024
D_SSM = 512
D_ATTN = 512
SSM_GROUP_CH = 16
SSM_GROUPS = D_SSM // SSM_GROUP_CH
SSM_STATE = 64
N_STATE = SSM_GROUPS * SSM_STATE
HEAD_DIM = 64
N_HEADS = D_ATTN // HEAD_DIM
KV_HEADS = 2
Q_PER_KV = N_HEADS // KV_HEADS
D_KV = KV_HEADS * HEAD_DIM
WINDOW = 128
D_PLE = 256
EPS = 1e-6

LANES = 128
assert D_KV == LANES and 2 * HEAD_DIM == LANES

OFF_U, OFF_ZS, OFF_Q, OFF_K, OFF_V, OFF_ZA, D_IN = 0, 512, 1024, 1536, 1664, 1792, 2304

SLAB_CH = LANES
N_SLABS = D_SSM // SLAB_CH
SLAB_STATE = (SLAB_CH // SSM_GROUP_CH) * SSM_STATE
SCAN_COLS = 256

TOK_IN = 1024
NORM_ROWS = 256
T_SCAN = 32
PIECE_ROWS = 128
PERM_BATCH = 8
TOK_OUT = 512
PROJ_COLS = 256
OUT_COLS = 256
VMEM_LIMIT = 48 * 1024 * 1024


def _silu(z):
    return z * jax.nn.sigmoid(z)


def _pipelined(step, n_blocks, body, even_bufs, odd_bufs):
    @pl.when(step == 0)
    def _():
        body(even_bufs, odd_bufs, True, False)

    @pl.when((step > 0) & (step < n_blocks) & (step % 2 == 0))
    def _():
        body(even_bufs, odd_bufs, True, True)

    @pl.when((step < n_blocks) & (step % 2 == 1))
    def _():
        body(odd_bufs, even_bufs, True, True)

    @pl.when(step == n_blocks)
    def _():
        last_read = odd_bufs if n_blocks % 2 == 0 else even_bufs
        body(None, last_read, False, True)


def _in_proj_kernel(x_ref, g_ref, w_ref, u_ref, zs_ref, q_ref, k_ref, v_ref, za_ref, w_scr):
    @pl.when((pl.program_id(0) == 0) & (pl.program_id(1) == 0))
    def _():
        w_scr[...] = w_ref[...].astype(BF16)

    def emit(rows, tile, acc):
        lo = tile * PROJ_COLS
        if lo < OFF_ZS:
            u_ref[rows, lo - OFF_U:lo - OFF_U + PROJ_COLS] = acc.astype(BF16)
        elif lo < OFF_Q:
            zs_ref[rows, lo - OFF_ZS:lo - OFF_ZS + PROJ_COLS] = _silu(acc).astype(BF16)
        elif lo < OFF_K:
            q_ref[rows, lo - OFF_Q:lo - OFF_Q + PROJ_COLS] = (
                acc * (1.0 / math.sqrt(HEAD_DIM))).astype(BF16)
        elif lo < OFF_ZA:
            k_ref[rows, :] = acc[:, :D_KV].astype(BF16)
            v_ref[rows, :] = acc[:, D_KV:].astype(BF16)
        else:
            za_ref[rows, lo - OFF_ZA:lo - OFF_ZA + PROJ_COLS] = _silu(acc).astype(BF16)

    gain = g_ref[...]
    for piece in range(TOK_IN // NORM_ROWS):
        rows = slice(piece * NORM_ROWS, (piece + 1) * NORM_ROWS)
        x = x_ref[rows, :]
        ms = jnp.mean(x * x, axis=-1, keepdims=True)
        hn = (x * lax.rsqrt(ms + EPS) * gain).astype(BF16)
        for tile in range(D_IN // PROJ_COLS):
            cols = slice(tile * PROJ_COLS, (tile + 1) * PROJ_COLS)
            emit(rows, tile, jnp.dot(hn, w_scr[:, cols], preferred_element_type=F32))


def _in_proj(x, g, w_in):
    bsz, seq, _ = x.shape
    tok = lambda width: pl.BlockSpec((None, TOK_IN, width), lambda b, i: (b, i, 0))
    widths = (D_SSM, D_SSM, D_ATTN, D_KV, D_KV, D_ATTN)
    return pl.pallas_call(
        _in_proj_kernel,
        grid=(bsz, seq // TOK_IN),
        in_specs=[
            tok(D_MODEL),
            pl.BlockSpec((1, D_MODEL), lambda b, i: (0, 0)),
            pl.BlockSpec((D_MODEL, D_IN), lambda b, i: (0, 0), pipeline_mode=pl.Buffered(1)),
        ],
        out_specs=tuple(tok(w) for w in widths),
        out_shape=tuple(jax.ShapeDtypeStruct((bsz, seq, w), BF16) for w in widths),
        scratch_shapes=[pltpu.VMEM((D_MODEL, D_IN), BF16)],
        compiler_params=pltpu.CompilerParams(
            dimension_semantics=("arbitrary", "arbitrary"), vmem_limit_bytes=VMEM_LIMIT),
        name="in_proj",
    )(x, g, w_in)


def _ssm_kernel(u_ref, perm_ref, permt_ref, win_ref, l2_ref, wout_ref, wdir_ref, d_ref,
                wglu_f32, bglu_ref, o_ref,
                wglu_ref, sre_ref, sim_ref, st_ref, y_ref,
                g0_ref, ut0_ref, g1_ref, ut1_ref, *, bsz, n_blocks):
    rows = bsz * T_SCAN
    half = rows // 2
    step = pl.program_id(0)

    @pl.when(step == 0)
    def _():
        wglu_ref[...] = wglu_f32[...].astype(BF16)
        sre_ref[...] = jnp.zeros_like(sre_ref)
        sim_ref[...] = jnp.zeros_like(sim_ref)

    def body(write, read, front, back):
        g_w, ut_w = write if front else (None, None)
        g_r, ut_r = read if back else (None, None)
        groups = bsz // PERM_BATCH
        grows = PERM_BATCH * T_SCAN
        if front:
            permuted = [
                jnp.dot(perm_ref[...],
                        u_ref[g * PERM_BATCH:(g + 1) * PERM_BATCH].reshape(grows, D_SSM),
                        preferred_element_type=F32).reshape(T_SCAN, 1, PERM_BATCH, D_SSM)
                for g in range(groups)]
            u_new = jnp.concatenate(permuted, axis=1).reshape(rows, D_SSM).astype(BF16)
            ut_w[...] = u_new
            u_last = ut_r[rows - bsz:, :] if back else jnp.zeros((bsz, D_SSM), BF16)
            u_before = jnp.concatenate([u_last, u_new[half:rows - bsz]], axis=0)
        for j in range(N_SLABS):
            ch = slice(j * SLAB_CH, (j + 1) * SLAB_CH)
            if front:
                drive_in = jnp.concatenate([u_before[:, ch], u_new[:half, ch]], axis=1)
            for c in range(SLAB_STATE // SCAN_COLS):
                lo = j * SLAB_STATE + c * SCAN_COLS
                cre = slice(c * SCAN_COLS, (c + 1) * SCAN_COLS)
                cim = slice(SLAB_STATE + c * SCAN_COLS, SLAB_STATE + (c + 1) * SCAN_COLS)
                if back:
                    l2r = jnp.broadcast_to(l2_ref[0, :, lo:lo + SCAN_COLS], (bsz, SCAN_COLS))
                    l2i = jnp.broadcast_to(l2_ref[1, :, lo:lo + SCAN_COLS], (bsz, SCAN_COLS))
                    sr = sre_ref[:, lo:lo + SCAN_COLS]
                    si = sim_ref[:, lo:lo + SCAN_COLS]
                for piece in range(half // PIECE_ROWS):
                    rk = slice(piece * PIECE_ROWS, (piece + 1) * PIECE_ROWS)
                    if front:
                        g_w[j, rk, cre] = jnp.dot(drive_in[rk], win_ref[0, j, :, cre],
                                                  preferred_element_type=F32)
                        g_w[j, rk, cim] = jnp.dot(drive_in[rk], win_ref[1, j, :, cre],
                                                  preferred_element_type=F32)
                    if not back:
                        continue
                    for k in range(rk.start // bsz, rk.stop // bsz):
                        r = slice(k * bsz, (k + 1) * bsz)
                        sr, si = (l2r * sr - l2i * si + g_r[j, r, cre],
                                  l2r * si + l2i * sr + g_r[j, r, cim])
                        st_ref[j, r, cre] = sr.astype(BF16)
                        st_ref[j, r, cim] = si.astype(BF16)
                    yk = (jnp.dot(st_ref[j, rk, cre], wout_ref[0, j, cre, :],
                                  preferred_element_type=F32)
                          + jnp.dot(st_ref[j, rk, cim], wout_ref[1, j, cre, :],
                                    preferred_element_type=F32))
                    odd_rows = slice(half + rk.start, half + rk.stop)
                    if c == 0:
                        y_ref[rk, ch] = yk[:, :SLAB_CH]
                        y_ref[odd_rows, ch] = yk[:, SLAB_CH:] + jnp.dot(
                            ut_r[odd_rows, ch], wdir_ref[j], preferred_element_type=F32)
                    else:
                        y_ref[rk, ch] += yk[:, :SLAB_CH]
                        y_ref[odd_rows, ch] += yk[:, SLAB_CH:]
                if back:
                    sre_ref[:, lo:lo + SCAN_COLS] = sr
                    sim_ref[:, lo:lo + SCAN_COLS] = si
        if not back:
            return

        y = y_ref[...] + d_ref[...] * ut_r[...].astype(F32)
        g = jax.nn.gelu(y)
        gate = jnp.dot(g.astype(BF16), wglu_ref[...], preferred_element_type=F32) + bglu_ref[...]
        out_tb = (g * jax.nn.sigmoid(gate)).reshape(T_SCAN, groups, PERM_BATCH, D_SSM)
        for grp in range(groups):
            out_g = out_tb[:, grp].reshape(grows, D_SSM).astype(BF16)
            out_bt = jnp.dot(permt_ref[...], out_g, preferred_element_type=F32).astype(BF16)
            o_ref[grp * PERM_BATCH:(grp + 1) * PERM_BATCH] = out_bt.reshape(
                PERM_BATCH, T_SCAN, D_SSM)

    _pipelined(step, n_blocks, body, (g0_ref, ut0_ref), (g1_ref, ut1_ref))


def _row_permutation():
    rows = PERM_BATCH * T_SCAN
    perm = np.zeros((rows, rows), np.float32)
    t, b = np.meshgrid(np.arange(T_SCAN), np.arange(PERM_BATCH), indexing="ij")
    dst = (t % 2) * (rows // 2) + (t // 2) * PERM_BATCH + b
    perm[dst.ravel(), (b * T_SCAN + t).ravel()] = 1.0
    return perm


def _ssm(u, params, d, wglu, bglu):
    bsz, seq, _ = u.shape
    rows = T_SCAN * bsz
    half = rows // 2
    nblk = seq // T_SCAN
    assert bsz % PERM_BATCH == 0
    perm = _row_permutation()
    full = lambda shape: pl.BlockSpec(shape, lambda f: (0,) * len(shape))
    g_scratch = pltpu.VMEM((N_SLABS, half, 2 * SLAB_STATE), F32)
    ut_scratch = pltpu.VMEM((rows, D_SSM), BF16)
    return pl.pallas_call(
        functools.partial(_ssm_kernel, bsz=bsz, n_blocks=nblk),
        grid=(nblk + 1,),
        in_specs=[
            pl.BlockSpec((bsz, T_SCAN, D_SSM), lambda f: (0, jnp.minimum(f, nblk - 1), 0)),
            full(perm.shape), full(perm.shape),
            full((2, N_SLABS, 2 * SLAB_CH, SLAB_STATE)), full((2, 1, N_STATE)),
            full((2, N_SLABS, SLAB_STATE, 2 * SLAB_CH)), full((N_SLABS, SLAB_CH, SLAB_CH)),
            full((1, D_SSM)),
            pl.BlockSpec((D_SSM, D_SSM), lambda f: (0, 0), pipeline_mode=pl.Buffered(1)),
            full((1, D_SSM)),
        ],
        out_specs=pl.BlockSpec((bsz, T_SCAN, D_SSM), lambda f: (0, jnp.maximum(f - 1, 0), 0)),
        out_shape=jax.ShapeDtypeStruct((bsz, seq, D_SSM), BF16),
        scratch_shapes=[
            pltpu.VMEM((D_SSM, D_SSM), BF16),
            pltpu.VMEM((bsz, N_STATE), F32),
            pltpu.VMEM((bsz, N_STATE), F32),
            pltpu.VMEM((N_SLABS, half, 2 * SLAB_STATE), BF16),
            pltpu.VMEM((rows, D_SSM), F32),
            g_scratch, ut_scratch, g_scratch, ut_scratch,
        ],
        compiler_params=pltpu.CompilerParams(
            dimension_semantics=("arbitrary",), vmem_limit_bytes=VMEM_LIMIT),
        name="s5_scan",
    )(u, jnp.asarray(perm, BF16), jnp.asarray(perm.T, BF16), *params, d, wglu, bglu)


def _attention_pieces(first, sink_ref, q_ref, k_ref, v_ref, kp_ref, vp_ref, za_ref, bias_ref,
                      half_ref, a_ref):
    low_q = lax.broadcasted_iota(jnp.int32, (WINDOW, LANES), 1) < HEAD_DIM
    kv_cache = {}

    def swap_halves(t):
        packed = pltpu.bitcast(t, jnp.uint32)
        return pltpu.bitcast(pltpu.roll(packed, HEAD_DIM, axis=1), BF16)

    def keys_values(jj, g):
        if jj not in kv_cache:
            rows = slice(jj * WINDOW, (jj + 1) * WINDOW)
            if jj == 0:
                kp, vp = kp_ref[...], vp_ref[...]
            else:
                prev = slice((jj - 1) * WINDOW, jj * WINDOW)
                kp, vp = k_ref[prev, :], v_ref[prev, :]
            kk = jnp.concatenate([kp, k_ref[rows, :]], axis=0)
            vv = jnp.concatenate([vp, v_ref[rows, :]], axis=0)
            kv_cache[jj] = ((kk, swap_halves(kk)), (vv, swap_halves(vv)))
        if (jj, g) not in kv_cache:
            keep_even, keep_odd = half_ref[0], half_ref[1]
            k2, v2 = kv_cache[jj]
            kcat = jnp.concatenate([k2[g] * keep_even, k2[1 - g] * keep_odd], axis=0)
            vcat = jnp.concatenate(
                [jnp.concatenate([v2[g] * keep_even, keep_even], axis=1),
                 jnp.concatenate([v2[1 - g] * keep_odd, keep_odd], axis=1)], axis=0)
            kv_cache[(jj, g)] = (kcat, vcat)
        return kv_cache[(jj, g)]

    items = [(jj, pair) for jj in range(TOK_OUT // WINDOW) for pair in range(N_HEADS // 2)]
    scores, probs = {}, {}

    def where(i):
        jj, pair = items[i]
        return (jj, pair, slice(jj * WINDOW, (jj + 1) * WINDOW),
                slice(pair * LANES, (pair + 1) * LANES))

    def stage_scores(i):
        jj, pair, rows, lanes = where(i)
        kcat, _ = keys_values(jj, pair // (Q_PER_KV // 2))
        s = lax.dot_general(q_ref[rows, lanes], kcat, (((1,), (1,)), ((), ())),
                            preferred_element_type=F32)
        scores[i] = s + (bias_ref[pair + (N_HEADS // 2) * first] if jj == 0 else bias_ref[pair])

    def stage_softmax(i):
        _, pair, _, _ = where(i)
        s = scores.pop(i)
        sink_e, sink_o = sink_ref[2 * pair], sink_ref[2 * pair + 1]
        m_e = jnp.maximum(jnp.max(s[:, :2 * WINDOW], axis=-1, keepdims=True), sink_e)
        m_o = jnp.maximum(jnp.max(s[:, 2 * WINDOW:], axis=-1, keepdims=True), sink_o)
        e = jnp.concatenate([jnp.exp(s[:, :2 * WINDOW] - m_e),
                             jnp.exp(s[:, 2 * WINDOW:] - m_o)], axis=1).astype(BF16)
        probs[i] = (e, jnp.where(low_q, jnp.exp(sink_e - m_e), jnp.exp(sink_o - m_o)))

    def stage_output(i):
        jj, pair, rows, lanes = where(i)
        _, vcat = keys_values(jj, pair // (Q_PER_KV // 2))
        e, sink_term = probs.pop(i)
        o = jnp.dot(e, vcat, preferred_element_type=F32)
        den = o[:, LANES:] + sink_term
        a_ref[rows, lanes] = (o[:, :LANES] / den * za_ref[rows, lanes].astype(F32)).astype(BF16)

    def slot(it):
        if it < len(items):
            stage_scores(it)
        if 0 <= it - 1 < len(items):
            stage_softmax(it - 1)
        if 0 <= it - 2 < len(items):
            stage_output(it - 2)

    return [functools.partial(slot, it) for it in range(len(items) + 2)]


def _attn_out_kernel(sink_ref, x_ref, p_ref, ssm_ref, zs_ref, q_ref, k_ref, v_ref, kp_ref, vp_ref,
                     za_ref, bias_ref, half_ref, wout_f32, pg_ref, wgate_f32, bgate_ref, wproj_f32,
                     o_ref, wout_ref, wgate_ref, wproj_ref, a0_ref, a1_ref,
                     *, blocks_per_seq, n_blocks):
    step = pl.program_id(0)
    seq_block = jnp.minimum(step, n_blocks - 1) % blocks_per_seq
    first = (seq_block == 0).astype(jnp.int32)

    @pl.when(step == 0)
    def _():
        wout_ref[...] = wout_f32[...].astype(BF16)
        wgate_ref[...] = wgate_f32[...].astype(BF16)
        wproj_ref[...] = wproj_f32[...].astype(BF16)

    def body(a_w, a_r, front, back):
        pieces = []
        if front:
            pieces = _attention_pieces(first, sink_ref, q_ref, k_ref, v_ref, kp_ref, vp_ref,
                                       za_ref, bias_ref, half_ref, a_w)
        n_chunks = D_MODEL // OUT_COLS
        per_chunk = max(len(pieces) - 2, 0) // (2 * n_chunks)
        pieces = iter(pieces)

        def run_pieces(count=per_chunk):
            for piece in (list(pieces) if count is None else [p for _, p in zip(range(count), pieces)]):
                piece()

        if not back:
            run_pieces(None)
            return
        run_pieces(2)

        s5 = (ssm_ref[...].astype(F32) * zs_ref[...].astype(F32)).astype(BF16)
        a_prev = a_r[...]
        mixed = []
        for n in range(n_chunks):
            cols = slice(n * OUT_COLS, (n + 1) * OUT_COLS)
            mixed.append(jnp.dot(s5, wout_ref[:D_SSM, cols], preferred_element_type=F32)
                         + jnp.dot(a_prev, wout_ref[D_SSM:, cols], preferred_element_type=F32))
            run_pieces()
        mixed = jnp.concatenate(mixed, axis=1)
        ms = jnp.mean(mixed * mixed, axis=-1, keepdims=True)
        h1 = x_ref[...] + mixed * lax.rsqrt(ms + EPS) * pg_ref[...]
        h1b = h1.astype(BF16)
        pb = p_ref[...].astype(BF16)
        for n in range(n_chunks):
            cols = slice(n * OUT_COLS, (n + 1) * OUT_COLS)
            gate = jax.nn.sigmoid(
                jnp.dot(h1b, wgate_ref[:, cols], preferred_element_type=F32) + bgate_ref[:, cols])
            ple = jnp.dot(pb, wproj_ref[:, cols], preferred_element_type=F32)
            o_ref[:, cols] = h1[:, cols] + gate * ple
            run_pieces()
        run_pieces(None)

    _pipelined(step, n_blocks, body, a0_ref, a1_ref)


def _attn_bias_table():
    q_idx = np.arange(WINDOW)[:, None]
    s_idx = np.arange(2 * WINDOW)[None, :]
    dist = q_idx + WINDOW - s_idx
    valid = (dist >= 0) & (dist < WINDOW)
    slopes = np.exp2(-8.0 * (np.arange(N_HEADS, dtype=np.float32) + 1.0) / N_HEADS).astype(np.float32)
    bias = -slopes[:, None, None] * dist.astype(np.float32)[None]
    normal = np.where(valid[None], bias, -np.inf)
    first = np.where((valid & (s_idx >= WINDOW))[None], bias, -np.inf)
    paired = lambda t: t.reshape(N_HEADS // 2, 2, WINDOW, 2 * WINDOW).transpose(0, 2, 1, 3).reshape(
        N_HEADS // 2, WINDOW, 4 * WINDOW)
    return np.concatenate([paired(normal), paired(first)], axis=0).astype(np.float32)


def _attn_out(sinks, x, p, ssm, zs, q, k, v, za, wout, pg, wgate, bgate, wproj):
    bsz, seq, _ = x.shape
    sub = TOK_OUT // WINDOW
    per_seq = seq // TOK_OUT
    n_blocks = bsz * per_seq

    def cur(f):
        f = jnp.minimum(f, n_blocks - 1)
        return f // per_seq, f % per_seq

    def prv(f):
        f = jnp.maximum(f - 1, 0)
        return f // per_seq, f % per_seq

    front = lambda width: pl.BlockSpec((None, TOK_OUT, width), lambda f: (*cur(f), 0))
    back = lambda width: pl.BlockSpec((None, TOK_OUT, width), lambda f: (*prv(f), 0))
    prev_kv = pl.BlockSpec(
        (None, WINDOW, D_KV),
        lambda f: (cur(f)[0], jnp.maximum(cur(f)[1] * sub - 1, 0), 0))
    full = lambda shape: pl.BlockSpec(shape, lambda f: (0,) * len(shape))
    once = lambda shape: pl.BlockSpec(shape, lambda f: (0,) * len(shape),
                                      pipeline_mode=pl.Buffered(1))
    bias = jnp.asarray(_attn_bias_table())
    lane_is_even = np.broadcast_to(np.arange(LANES) < HEAD_DIM, (2 * WINDOW, LANES))
    half = jnp.asarray(np.stack([lane_is_even, ~lane_is_even]), BF16)
    a_scratch = pltpu.VMEM((TOK_OUT, D_ATTN), BF16)
    return pl.pallas_call(
        functools.partial(_attn_out_kernel, blocks_per_seq=per_seq, n_blocks=n_blocks),
        grid=(n_blocks + 1,),
        in_specs=[
            pl.BlockSpec(memory_space=pltpu.SMEM),
            back(D_MODEL), back(D_PLE), back(D_SSM), back(D_SSM), front(D_ATTN),
            front(D_KV), front(D_KV), prev_kv, prev_kv, front(D_ATTN),
            once(bias.shape), once(half.shape),
            once((D_MODEL, D_MODEL)), full((1, D_MODEL)),
            once((D_MODEL, D_MODEL)), full((1, D_MODEL)), once((D_PLE, D_MODEL)),
        ],
        out_specs=back(D_MODEL),
        out_shape=jax.ShapeDtypeStruct((bsz, seq, D_MODEL), F32),
        scratch_shapes=[pltpu.VMEM((D_MODEL, D_MODEL), BF16), pltpu.VMEM((D_MODEL, D_MODEL), BF16),
                        pltpu.VMEM((D_PLE, D_MODEL), BF16), a_scratch, a_scratch],
        compiler_params=pltpu.CompilerParams(
            dimension_semantics=("arbitrary",), vmem_limit_bytes=VMEM_LIMIT),
        name="attn_out",
    )(sinks, x, p, ssm, zs, q, k, v, k, v, za, bias, half, wout, pg, wgate, bgate, wproj)


def _block_diag(blocks, axis):
    n = blocks.shape[axis]
    eye = jnp.eye(n, dtype=blocks.dtype)
    lead = ''.join(chr(ord('a') + i) for i in range(axis))
    rest = ''.join(chr(ord('o') + i) for i in range(blocks.ndim - axis - 2))
    return jnp.einsum(f'{lead}nm{rest},nz->{lead}nmz{rest}', blocks, eye)


def _ssm_params(lam_re, lam_im, log_step, b_re, b_im, c_re, c_im):
    lr, li = lam_re.astype(F32), lam_im.astype(F32)
    step = jnp.exp(log_step.astype(F32))[:, None]
    mag = jnp.exp(lr * step)
    lbr, lbi = mag * jnp.cos(li * step), mag * jnp.sin(li * step)
    den = lr * lr + li * li
    fr = ((lbr - 1.0) * lr + lbi * li) / den
    fi = (lbi * lr - (lbr - 1.0) * li) / den
    br, bi = b_re.astype(F32), b_im.astype(F32)
    bbr = fr[..., None] * br - fi[..., None] * bi
    bbi = fr[..., None] * bi + fi[..., None] * br
    cr, ci = c_re.astype(F32), c_im.astype(F32)
    l2 = jnp.stack([lbr * lbr - lbi * lbi, 2.0 * lbr * lbi])
    lbbr = lbr[..., None] * bbr - lbi[..., None] * bbi
    lbbi = lbr[..., None] * bbi + lbi[..., None] * bbr
    clr = cr * lbr[:, None, :] - ci * lbi[:, None, :]
    cli = cr * lbi[:, None, :] + ci * lbr[:, None, :]
    cb = jnp.sum(cr[:, :, :, None] * bbr[:, None, :, :] - ci[:, :, :, None] * bbi[:, None, :, :],
                 axis=2)
    gps = SLAB_CH // SSM_GROUP_CH
    n, p = SSM_STATE, SSM_GROUP_CH

    w_in = jnp.stack([jnp.stack([lbbr, bbr]), jnp.stack([lbbi, bbi])])
    w_in = w_in.reshape(2, 2, N_SLABS, gps, n, p).transpose(0, 2, 1, 3, 5, 4)
    w_in = _block_diag(w_in, 3).reshape(2, N_SLABS, 2 * SLAB_CH, SLAB_STATE).astype(BF16)
    w_out = jnp.stack([jnp.stack([cr, clr]), jnp.stack([-ci, -cli])])
    w_out = w_out.reshape(2, 2, N_SLABS, gps, p, n).transpose(0, 2, 1, 3, 5, 4)
    w_out = _block_diag(w_out, 3)
    w_out = w_out.transpose(0, 1, 3, 4, 2, 5, 6).reshape(
        2, N_SLABS, SLAB_STATE, 2 * SLAB_CH).astype(BF16)
    w_dir = _block_diag(cb.reshape(N_SLABS, gps, p, p).transpose(0, 1, 3, 2), 1)
    w_dir = w_dir.reshape(N_SLABS, SLAB_CH, SLAB_CH).astype(BF16)
    return w_in, l2.reshape(2, 1, N_STATE), w_out, w_dir


def _layer(h, p, pre_norm_g, w_in, ssm_lam_re, ssm_lam_im, ssm_log_step, ssm_b_re, ssm_b_im,
           ssm_c_re, ssm_c_im, ssm_d, ssm_w_glu, ssm_b_glu, attn_sinks, w_out, post_norm_g,
           pl_w_proj, pl_w_gate, pl_b_gate):
    row = lambda a: a.astype(F32).reshape(1, -1)
    u, zs, q, k, v, za = _in_proj(h, row(pre_norm_g), w_in.astype(F32))
    params = _ssm_params(
        ssm_lam_re, ssm_lam_im, ssm_log_step, ssm_b_re, ssm_b_im, ssm_c_re, ssm_c_im)
    ssm = _ssm(u, params, row(ssm_d), ssm_w_glu.astype(F32), row(ssm_b_glu))
    return _attn_out(attn_sinks.astype(F32), h, p, ssm, zs, q, k, v, za,
                     w_out.astype(F32), row(post_norm_g), pl_w_gate.astype(F32), row(pl_b_gate),
                     pl_w_proj.astype(F32))


def kernel(x, p, pre_norm_g, w_in, ssm_lam_re, ssm_lam_im, ssm_log_step, ssm_b_re, ssm_b_im,
           ssm_c_re, ssm_c_im, ssm_d, ssm_w_glu, ssm_b_glu, attn_sinks, w_out, post_norm_g,
           pl_w_proj, pl_w_gate, pl_b_gate):
    h = x
    for i in range(p.shape[0]):
        h = _layer(h, p[i], pre_norm_g[i], w_in[i], ssm_lam_re[i], ssm_lam_im[i], ssm_log_step[i],
                   ssm_b_re[i], ssm_b_im[i], ssm_c_re[i], ssm_c_im[i], ssm_d[i], ssm_w_glu[i],
                   ssm_b_glu[i], attn_sinks[i], w_out[i], post_norm_g[i], pl_w_proj[i],
                   pl_w_gate[i], pl_b_gate[i])
    return h
```

```python
import functools
import math

import numpy as np
import jax
import jax.numpy as jnp
from jax import lax
from jax.experimental import pallas as pl
from jax.experimental.pallas import tpu as pltpu

F32 = jnp.float32
BF16 = jnp.bfloat16

D_MODEL = 1024
D_SSM = 512
D_ATTN = 512
SSM_GROUP_CH = 16
SSM_GROUPS = D_SSM // SSM_GROUP_CH
SSM_STATE = 64
N_STATE = SSM_GROUPS * SSM_STATE
HEAD_DIM = 64
N_HEADS = D_ATTN // HEAD_DIM
KV_HEADS = 2
Q_PER_KV = N_HEADS // KV_HEADS
D_KV = KV_HEADS * HEAD_DIM
WINDOW = 128
D_PLE = 256
EPS = 1e-6

LANES = 128
assert D_KV == LANES and 2 * HEAD_DIM == LANES

OFF_U, OFF_ZS, OFF_Q, OFF_K, OFF_V, OFF_ZA, D_IN = 0, 512, 1024, 1536, 1664, 1792, 2304

SLAB_CH = LANES
N_SLABS = D_SSM // SLAB_CH
SLAB_STATE = (SLAB_CH // SSM_GROUP_CH) * SSM_STATE
SCAN_COLS = 256

TOK_IN = 1024
NORM_ROWS = 256
T_SCAN = 32
PIECE_ROWS = 128
PERM_BATCH = 8
TOK_OUT = 512
PROJ_COLS = 256
OUT_COLS = 256
VMEM_LIMIT = 48 * 1024 * 1024


def _silu(z):
    return z * jax.nn.sigmoid(z)


def _pipelined(step, n_blocks, body, even_bufs, odd_bufs):
    @pl.when(step == 0)
    def _():
        body(even_bufs, odd_bufs, True, False)

    @pl.when((step > 0) & (step < n_blocks) & (step % 2 == 0))
    def _():
        body(even_bufs, odd_bufs, True, True)

    @pl.when((step < n_blocks) & (step % 2 == 1))
    def _():
        body(odd_bufs, even_bufs, True, True)

    @pl.when(step == n_blocks)
    def _():
        last_read = odd_bufs if n_blocks % 2 == 0 else even_bufs
        body(None, last_read, False, True)


def _in_proj_kernel(x_ref, g_ref, w_ref, u_ref, zs_ref, q_ref, k_ref, v_ref, za_ref, w_scr):
    @pl.when((pl.program_id(0) == 0) & (pl.program_id(1) == 0))
    def _():
        w_scr[...] = w_ref[...].astype(BF16)

    def emit(rows, tile, acc):
        lo = tile * PROJ_COLS
        if lo < OFF_ZS:
            u_ref[rows, lo - OFF_U:lo - OFF_U + PROJ_COLS] = acc.astype(BF16)
        elif lo < OFF_Q:
            zs_ref[rows, lo - OFF_ZS:lo - OFF_ZS + PROJ_COLS] = _silu(acc).astype(BF16)
        elif lo < OFF_K:
            q_ref[rows, lo - OFF_Q:lo - OFF_Q + PROJ_COLS] = (
                acc * (1.0 / math.sqrt(HEAD_DIM))).astype(BF16)
        elif lo < OFF_ZA:
            k_ref[rows, :] = acc[:, :D_KV].astype(BF16)
            v_ref[rows, :] = acc[:, D_KV:].astype(BF16)
        else:
            za_ref[rows, lo - OFF_ZA:lo - OFF_ZA + PROJ_COLS] = _silu(acc).astype(BF16)

    gain = g_ref[...]
    for piece in range(TOK_IN // NORM_ROWS):
        rows = slice(piece * NORM_ROWS, (piece + 1) * NORM_ROWS)
        x = x_ref[rows, :]
        ms = jnp.mean(x * x, axis=-1, keepdims=True)
        hn = (x * lax.rsqrt(ms + EPS) * gain).astype(BF16)
        for tile in range(D_IN // PROJ_COLS):
            cols = slice(tile * PROJ_COLS, (tile + 1) * PROJ_COLS)
            emit(rows, tile, jnp.dot(hn, w_scr[:, cols], preferred_element_type=F32))


def _in_proj(x, g, w_in):
    bsz, seq, _ = x.shape
    tok = lambda width: pl.BlockSpec((None, TOK_IN, width), lambda b, i: (b, i, 0))
    widths = (D_SSM, D_SSM, D_ATTN, D_KV, D_KV, D_ATTN)
    return pl.pallas_call(
        _in_proj_kernel,
        grid=(bsz, seq // TOK_IN),
        in_specs=[
            tok(D_MODEL),
            pl.BlockSpec((1, D_MODEL), lambda b, i: (0, 0)),
            pl.BlockSpec((D_MODEL, D_IN), lambda b, i: (0, 0), pipeline_mode=pl.Buffered(1)),
        ],
        out_specs=tuple(tok(w) for w in widths),
        out_shape=tuple(jax.ShapeDtypeStruct((bsz, seq, w), BF16) for w in widths),
        scratch_shapes=[pltpu.VMEM((D_MODEL, D_IN), BF16)],
        compiler_params=pltpu.CompilerParams(
            dimension_semantics=("arbitrary", "arbitrary"), vmem_limit_bytes=VMEM_LIMIT),
        name="in_proj",
    )(x, g, w_in)


def _ssm_kernel(u_ref, perm_ref, permt_ref, win_ref, l2_ref, wout_ref, wdir_ref, d_ref,
                wglu_f32, bglu_ref, o_ref,
                wglu_ref, sre_ref, sim_ref, st_ref, y_ref,
                g0_ref, ut0_ref, g1_ref, ut1_ref, *, bsz, n_blocks):
    rows = bsz * T_SCAN
    half = rows // 2
    step = pl.program_id(0)

    @pl.when(step == 0)
    def _():
        wglu_ref[...] = wglu_f32[...].astype(BF16)
        sre_ref[...] = jnp.zeros_like(sre_ref)
        sim_ref[...] = jnp.zeros_like(sim_ref)

    def body(write, read, front, back):
        g_w, ut_w = write if front else (None, None)
        g_r, ut_r = read if back else (None, None)
        groups = bsz // PERM_BATCH
        grows = PERM_BATCH * T_SCAN
        if front:
            permuted = [
                jnp.dot(perm_ref[...],
                        u_ref[g * PERM_BATCH:(g + 1) * PERM_BATCH].reshape(grows, D_SSM),
                        preferred_element_type=F32).reshape(T_SCAN, 1, PERM_BATCH, D_SSM)
                for g in range(groups)]
            u_new = jnp.concatenate(permuted, axis=1).reshape(rows, D_SSM).astype(BF16)
            ut_w[...] = u_new
            u_last = ut_r[rows - bsz:, :] if back else jnp.zeros((bsz, D_SSM), BF16)
            u_before = jnp.concatenate([u_last, u_new[half:rows - bsz]], axis=0)
        for j in range(N_SLABS):
            ch = slice(j * SLAB_CH, (j + 1) * SLAB_CH)
            if front:
                drive_in = jnp.concatenate([u_before[:, ch], u_new[:half, ch]], axis=1)
            for c in range(SLAB_STATE // SCAN_COLS):
                lo = j * SLAB_STATE + c * SCAN_COLS
                cre = slice(c * SCAN_COLS, (c + 1) * SCAN_COLS)
                cim = slice(SLAB_STATE + c * SCAN_COLS, SLAB_STATE + (c + 1) * SCAN_COLS)
                if back:
                    l2r = jnp.broadcast_to(l2_ref[0, :, lo:lo + SCAN_COLS], (bsz, SCAN_COLS))
                    l2i = jnp.broadcast_to(l2_ref[1, :, lo:lo + SCAN_COLS], (bsz, SCAN_COLS))
                    sr = sre_ref[:, lo:lo + SCAN_COLS]
                    si = sim_ref[:, lo:lo + SCAN_COLS]
                for piece in range(half // PIECE_ROWS):
                    rk = slice(piece * PIECE_ROWS, (piece + 1) * PIECE_ROWS)
                    if front:
                        g_w[j, rk, cre] = jnp.dot(drive_in[rk], win_ref[0, j, :, cre],
                                                  preferred_element_type=F32)
                        g_w[j, rk, cim] = jnp.dot(drive_in[rk], win_ref[1, j, :, cre],
                                                  preferred_element_type=F32)
                    if not back:
                        continue
                    for k in range(rk.start // bsz, rk.stop // bsz):
                        r = slice(k * bsz, (k + 1) * bsz)
                        sr, si = (l2r * sr - l2i * si + g_r[j, r, cre],
                                  l2r * si + l2i * sr + g_r[j, r, cim])
                        st_ref[j, r, cre] = sr.astype(BF16)
                        st_ref[j, r, cim] = si.astype(BF16)
                    yk = (jnp.dot(st_ref[j, rk, cre], wout_ref[0, j, cre, :],
                                  preferred_element_type=F32)
                          + jnp.dot(st_ref[j, rk, cim], wout_ref[1, j, cre, :],
                                    preferred_element_type=F32))
                    odd_rows = slice(half + rk.start, half + rk.stop)
                    if c == 0:
                        y_ref[rk, ch] = yk[:, :SLAB_CH]
                        y_ref[odd_rows, ch] = yk[:, SLAB_CH:] + jnp.dot(
                            ut_r[odd_rows, ch], wdir_ref[j], preferred_element_type=F32)
                    else:
                        y_ref[rk, ch] += yk[:, :SLAB_CH]
                        y_ref[odd_rows, ch] += yk[:, SLAB_CH:]
                if back:
                    sre_ref[:, lo:lo + SCAN_COLS] = sr
                    sim_ref[:, lo:lo + SCAN_COLS] = si
        if not back:
            return

        y = y_ref[...] + d_ref[...] * ut_r[...].astype(F32)
        g = jax.nn.gelu(y)
        gate = jnp.dot(g.astype(BF16), wglu_ref[...], preferred_element_type=F32) + bglu_ref[...]
        out_tb = (g * jax.nn.sigmoid(gate)).reshape(T_SCAN, groups, PERM_BATCH, D_SSM)
        for grp in range(groups):
            out_g = out_tb[:, grp].reshape(grows, D_SSM).astype(BF16)
            out_bt = jnp.dot(permt_ref[...], out_g, preferred_element_type=F32).astype(BF16)
            o_ref[grp * PERM_BATCH:(grp + 1) * PERM_BATCH] = out_bt.reshape(
                PERM_BATCH, T_SCAN, D_SSM)

    _pipelined(step, n_blocks, body, (g0_ref, ut0_ref), (g1_ref, ut1_ref))


def _row_permutation():
    rows = PERM_BATCH * T_SCAN
    perm = np.zeros((rows, rows), np.float32)
    t, b = np.meshgrid(np.arange(T_SCAN), np.arange(PERM_BATCH), indexing="ij")
    dst = (t % 2) * (rows // 2) + (t // 2) * PERM_BATCH + b
    perm[dst.ravel(), (b * T_SCAN + t).ravel()] = 1.0
    return perm


def _ssm(u, params, d, wglu, bglu):
    bsz, seq, _ = u.shape
    rows = T_SCAN * bsz
    half = rows // 2
    nblk = seq // T_SCAN
    assert bsz % PERM_BATCH == 0
    perm = _row_permutation()
    full = lambda shape: pl.BlockSpec(shape, lambda f: (0,) * len(shape))
    g_scratch = pltpu.VMEM((N_SLABS, half, 2 * SLAB_STATE), F32)
    ut_scratch = pltpu.VMEM((rows, D_SSM), BF16)
    return pl.pallas_call(
        functools.partial(_ssm_kernel, bsz=bsz, n_blocks=nblk),
        grid=(nblk + 1,),
        in_specs=[
            pl.BlockSpec((bsz, T_SCAN, D_SSM), lambda f: (0, jnp.minimum(f, nblk - 1), 0)),
            full(perm.shape), full(perm.shape),
            full((2, N_SLABS, 2 * SLAB_CH, SLAB_STATE)), full((2, 1, N_STATE)),
            full((2, N_SLABS, SLAB_STATE, 2 * SLAB_CH)), full((N_SLABS, SLAB_CH, SLAB_CH)),
            full((1, D_SSM)),
            pl.BlockSpec((D_SSM, D_SSM), lambda f: (0, 0), pipeline_mode=pl.Buffered(1)),
            full((1, D_SSM)),
        ],
        out_specs=pl.BlockSpec((bsz, T_SCAN, D_SSM), lambda f: (0, jnp.maximum(f - 1, 0), 0)),
        out_shape=jax.ShapeDtypeStruct((bsz, seq, D_SSM), BF16),
        scratch_shapes=[
            pltpu.VMEM((D_SSM, D_SSM), BF16),
            pltpu.VMEM((bsz, N_STATE), F32),
            pltpu.VMEM((bsz, N_STATE), F32),
            pltpu.VMEM((N_SLABS, half, 2 * SLAB_STATE), BF16),
            pltpu.VMEM((rows, D_SSM), F32),
            g_scratch, ut_scratch, g_scratch, ut_scratch,
        ],
        compiler_params=pltpu.CompilerParams(
            dimension_semantics=("arbitrary",), vmem_limit_bytes=VMEM_LIMIT),
        name="s5_scan",
    )(u, jnp.asarray(perm, BF16), jnp.asarray(perm.T, BF16), *params, d, wglu, bglu)


def _attention_pieces(first, sink_ref, q_ref, k_ref, v_ref, kp_ref, vp_ref, za_ref, bias_ref,
                      half_ref, a_ref):
    low_q = lax.broadcasted_iota(jnp.int32, (WINDOW, LANES), 1) < HEAD_DIM
    kv_cache = {}

    def swap_halves(t):
        packed = pltpu.bitcast(t, jnp.uint32)
        return pltpu.bitcast(pltpu.roll(packed, HEAD_DIM, axis=1), BF16)

    def keys_values(jj, g):
        if jj not in kv_cache:
            rows = slice(jj * WINDOW, (jj + 1) * WINDOW)
            if jj == 0:
                kp, vp = kp_ref[...], vp_ref[...]
            else:
                prev = slice((jj - 1) * WINDOW, jj * WINDOW)
                kp, vp = k_ref[prev, :], v_ref[prev, :]
            kk = jnp.concatenate([kp, k_ref[rows, :]], axis=0)
            vv = jnp.concatenate([vp, v_ref[rows, :]], axis=0)
            kv_cache[jj] = ((kk, swap_halves(kk)), (vv, swap_halves(vv)))
        if (jj, g) not in kv_cache:
            keep_even, keep_odd = half_ref[0], half_ref[1]
            k2, v2 = kv_cache[jj]
            kcat = jnp.concatenate([k2[g] * keep_even, k2[1 - g] * keep_odd], axis=0)
            vcat = jnp.concatenate(
                [jnp.concatenate([v2[g] * keep_even, keep_even], axis=1),
                 jnp.concatenate([v2[1 - g] * keep_odd, keep_odd], axis=1)], axis=0)
            kv_cache[(jj, g)] = (kcat, vcat)
        return kv_cache[(jj, g)]

    items = [(jj, pair) for jj in range(TOK_OUT // WINDOW) for pair in range(N_HEADS // 2)]
    scores, probs = {}, {}

    def where(i):
        jj, pair = items[i]
        return (jj, pair, slice(jj * WINDOW, (jj + 1) * WINDOW),
                slice(pair * LANES, (pair + 1) * LANES))

    def stage_scores(i):
        jj, pair, rows, lanes = where(i)
        kcat, _ = keys_values(jj, pair // (Q_PER_KV // 2))
        s = lax.dot_general(q_ref[rows, lanes], kcat, (((1,), (1,)), ((), ())),
                            preferred_element_type=F32)
        scores[i] = s + (bias_ref[pair + (N_HEADS // 2) * first] if jj == 0 else bias_ref[pair])

    def stage_softmax(i):
        _, pair, _, _ = where(i)
        s = scores.pop(i)
        sink_e, sink_o = sink_ref[2 * pair], sink_ref[2 * pair + 1]
        m_e = jnp.maximum(jnp.max(s[:, :2 * WINDOW], axis=-1, keepdims=True), sink_e)
        m_o = jnp.maximum(jnp.max(s[:, 2 * WINDOW:], axis=-1, keepdims=True), sink_o)
        e = jnp.concatenate([jnp.exp(s[:, :2 * WINDOW] - m_e),
                             jnp.exp(s[:, 2 * WINDOW:] - m_o)], axis=1).astype(BF16)
        probs[i] = (e, jnp.where(low_q, jnp.exp(sink_e - m_e), jnp.exp(sink_o - m_o)))

    def stage_output(i):
        jj, pair, rows, lanes = where(i)
        _, vcat = keys_values(jj, pair // (Q_PER_KV // 2))
        e, sink_term = probs.pop(i)
        o = jnp.dot(e, vcat, preferred_element_type=F32)
        den = o[:, LANES:] + sink_term
        a_ref[rows, lanes] = (o[:, :LANES] / den * za_ref[rows, lanes].astype(F32)).astype(BF16)

    def slot(it):
        if it < len(items):
            stage_scores(it)
        if 0 <= it - 1 < len(items):
            stage_softmax(it - 1)
        if 0 <= it - 2 < len(items):
            stage_output(it - 2)

    return [functools.partial(slot, it) for it in range(len(items) + 2)]


def _attn_out_kernel(sink_ref, x_ref, p_ref, ssm_ref, zs_ref, q_ref, k_ref, v_ref, kp_ref, vp_ref,
                     za_ref, bias_ref, half_ref, wout_f32, pg_ref, wgate_f32, bgate_ref, wproj_f32,
                     o_ref, wout_ref, wgate_ref, wproj_ref, a0_ref, a1_ref,
                     *, blocks_per_seq, n_blocks):
    step = pl.program_id(0)
    seq_block = jnp.minimum(step, n_blocks - 1) % blocks_per_seq
    first = (seq_block == 0).astype(jnp.int32)

    @pl.when(step == 0)
    def _():
        wout_ref[...] = wout_f32[...].astype(BF16)
        wgate_ref[...] = wgate_f32[...].astype(BF16)
        wproj_ref[...] = wproj_f32[...].astype(BF16)

    def body(a_w, a_r, front, back):
        pieces = []
        if front:
            pieces = _attention_pieces(first, sink_ref, q_ref, k_ref, v_ref, kp_ref, vp_ref,
                                       za_ref, bias_ref, half_ref, a_w)
        n_chunks = D_MODEL // OUT_COLS
        per_chunk = max(len(pieces) - 2, 0) // (2 * n_chunks)
        pieces = iter(pieces)

        def run_pieces(count=per_chunk):
            for piece in (list(pieces) if count is None else [p for _, p in zip(range(count), pieces)]):
                piece()

        if not back:
            run_pieces(None)
            return
        run_pieces(2)

        s5 = (ssm_ref[...].astype(F32) * zs_ref[...].astype(F32)).astype(BF16)
        a_prev = a_r[...]
        mixed = []
        for n in range(n_chunks):
            cols = slice(n * OUT_COLS, (n + 1) * OUT_COLS)
            mixed.append(jnp.dot(s5, wout_ref[:D_SSM, cols], preferred_element_type=F32)
                         + jnp.dot(a_prev, wout_ref[D_SSM:, cols], preferred_element_type=F32))
            run_pieces()
        mixed = jnp.concatenate(mixed, axis=1)
        ms = jnp.mean(mixed * mixed, axis=-1, keepdims=True)
        h1 = x_ref[...] + mixed * lax.rsqrt(ms + EPS) * pg_ref[...]
        h1b = h1.astype(BF16)
        pb = p_ref[...].astype(BF16)
        for n in range(n_chunks):
            cols = slice(n * OUT_COLS, (n + 1) * OUT_COLS)
            gate = jax.nn.sigmoid(
                jnp.dot(h1b, wgate_ref[:, cols], preferred_element_type=F32) + bgate_ref[:, cols])
            ple = jnp.dot(pb, wproj_ref[:, cols], preferred_element_type=F32)
            o_ref[:, cols] = h1[:, cols] + gate * ple
            run_pieces()
        run_pieces(None)

    _pipelined(step, n_blocks, body, a0_ref, a1_ref)


def _attn_bias_table():
    q_idx = np.arange(WINDOW)[:, None]
    s_idx = np.arange(2 * WINDOW)[None, :]
    dist = q_idx + WINDOW - s_idx
    valid = (dist >= 0) & (dist < WINDOW)
    slopes = np.exp2(-8.0 * (np.arange(N_HEADS, dtype=np.float32) + 1.0) / N_HEADS).astype(np.float32)
    bias = -slopes[:, None, None] * dist.astype(np.float32)[None]
    normal = np.where(valid[None], bias, -np.inf)
    first = np.where((valid & (s_idx >= WINDOW))[None], bias, -np.inf)
    paired = lambda t: t.reshape(N_HEADS // 2, 2, WINDOW, 2 * WINDOW).transpose(0, 2, 1, 3).reshape(
        N_HEADS // 2, WINDOW, 4 * WINDOW)
    return np.concatenate([paired(normal), paired(first)], axis=0).astype(np.float32)


def _attn_out(sinks, x, p, ssm, zs, q, k, v, za, wout, pg, wgate, bgate, wproj):
    bsz, seq, _ = x.shape
    sub = TOK_OUT // WINDOW
    per_seq = seq // TOK_OUT
    n_blocks = bsz * per_seq

    def cur(f):
        f = jnp.minimum(f, n_blocks - 1)
        return f // per_seq, f % per_seq

    def prv(f):
        f = jnp.maximum(f - 1, 0)
        return f // per_seq, f % per_seq

    front = lambda width: pl.BlockSpec((None, TOK_OUT, width), lambda f: (*cur(f), 0))
    back = lambda width: pl.BlockSpec((None, TOK_OUT, width), lambda f: (*prv(f), 0))
    prev_kv = pl.BlockSpec(
        (None, WINDOW, D_KV),
        lambda f: (cur(f)[0], jnp.maximum(cur(f)[1] * sub - 1, 0), 0))
    full = lambda shape: pl.BlockSpec(shape, lambda f: (0,) * len(shape))
    once = lambda shape: pl.BlockSpec(shape, lambda f: (0,) * len(shape),
                                      pipeline_mode=pl.Buffered(1))
    bias = jnp.asarray(_attn_bias_table())
    lane_is_even = np.broadcast_to(np.arange(LANES) < HEAD_DIM, (2 * WINDOW, LANES))
    half = jnp.asarray(np.stack([lane_is_even, ~lane_is_even]), BF16)
    a_scratch = pltpu.VMEM((TOK_OUT, D_ATTN), BF16)
    return pl.pallas_call(
        functools.partial(_attn_out_kernel, blocks_per_seq=per_seq, n_blocks=n_blocks),
        grid=(n_blocks + 1,),
        in_specs=[
            pl.BlockSpec(memory_space=pltpu.SMEM),
            back(D_MODEL), back(D_PLE), back(D_SSM), back(D_SSM), front(D_ATTN),
            front(D_KV), front(D_KV), prev_kv, prev_kv, front(D_ATTN),
            once(bias.shape), once(half.shape),
            once((D_MODEL, D_MODEL)), full((1, D_MODEL)),
            once((D_MODEL, D_MODEL)), full((1, D_MODEL)), once((D_PLE, D_MODEL)),
        ],
        out_specs=back(D_MODEL),
        out_shape=jax.ShapeDtypeStruct((bsz, seq, D_MODEL), F32),
        scratch_shapes=[pltpu.VMEM((D_MODEL, D_MODEL), BF16), pltpu.VMEM((D_MODEL, D_MODEL), BF16),
                        pltpu.VMEM((D_PLE, D_MODEL), BF16), a_scratch, a_scratch],
        compiler_params=pltpu.CompilerParams(
            dimension_semantics=("arbitrary",), vmem_limit_bytes=VMEM_LIMIT),
        name="attn_out",
    )(sinks, x, p, ssm, zs, q, k, v, k, v, za, bias, half, wout, pg, wgate, bgate, wproj)


def _ssm_params(lam_re, lam_im, log_step, b_re, b_im, c_re, c_im):
    lr, li = lam_re.astype(F32), lam_im.astype(F32)
    step = jnp.exp(log_step.astype(F32))[:, None]
    mag = jnp.exp(lr * step)
    lbr, lbi = mag * jnp.cos(li * step), mag * jnp.sin(li * step)
    den = lr * lr + li * li
    fr = ((lbr - 1.0) * lr + lbi * li) / den
    fi = (lbi * lr - (lbr - 1.0) * li) / den
    br, bi = b_re.astype(F32), b_im.astype(F32)
    bbr = fr[..., None] * br - fi[..., None] * bi
    bbi = fr[..., None] * bi + fi[..., None] * br
    cr, ci = c_re.astype(F32), c_im.astype(F32)
    l2 = jnp.stack([lbr * lbr - lbi * lbi, 2.0 * lbr * lbi])
    lbbr = lbr[..., None] * bbr - lbi[..., None] * bbi
    lbbi = lbr[..., None] * bbi + lbi[..., None] * bbr
    clr = cr * lbr[:, None, :] - ci * lbi[:, None, :]
    cli = cr * lbi[:, None, :] + ci * lbr[:, None, :]
    cb = jnp.sum(cr[:, :, :, None] * bbr[:, None, :, :] - ci[:, :, :, None] * bbi[:, None, :, :],
                 axis=2)
    gps = SLAB_CH // SSM_GROUP_CH
    n, p = SSM_STATE, SSM_GROUP_CH
    same_group = (np.arange(SLAB_CH)[:, None] // p == np.arange(SLAB_STATE)[None, :] // n)
    expand_p = np.tile(np.eye(p, dtype=np.float32), (gps, 1))
    expand_n = np.tile(np.eye(n, dtype=np.float32), (gps, 1))

    w_in = jnp.stack([jnp.stack([lbbr, bbr]), jnp.stack([lbbi, bbi])])
    w_in = w_in.reshape(2, 2, N_SLABS, SLAB_STATE, p)
    w_in = jnp.einsum('rp,absqp->asbrq', expand_p, w_in) * same_group.astype(np.float32)
    w_in = w_in.astype(BF16).reshape(2, N_SLABS, 2 * SLAB_CH, SLAB_STATE)
    w_out = jnp.stack([jnp.stack([cr, clr]), jnp.stack([-ci, -cli])])
    w_out = w_out.reshape(2, 2, N_SLABS, SLAB_CH, n)
    w_out = jnp.einsum('rn,aesqn->aesrq', expand_n, w_out) * same_group.T.astype(np.float32)
    w_out = jnp.concatenate([w_out[:, 0], w_out[:, 1]], axis=-1).astype(BF16)
    same_group_pp = same_group[:, ::n // p]
    w_dir = jnp.einsum('rq,sxq->srx', expand_p, cb.reshape(N_SLABS, SLAB_CH, p))
    w_dir = (w_dir * same_group_pp.astype(np.float32)).astype(BF16)
    return w_in, l2.reshape(2, 1, N_STATE), w_out, w_dir


def _layer(h, p, pre_norm_g, w_in, ssm_lam_re, ssm_lam_im, ssm_log_step, ssm_b_re, ssm_b_im,
           ssm_c_re, ssm_c_im, ssm_d, ssm_w_glu, ssm_b_glu, attn_sinks, w_out, post_norm_g,
           pl_w_proj, pl_w_gate, pl_b_gate):
    row = lambda a: a.astype(F32).reshape(1, -1)
    u, zs, q, k, v, za = _in_proj(h, row(pre_norm_g), w_in.astype(F32))
    params = _ssm_params(
        ssm_lam_re, ssm_lam_im, ssm_log_step, ssm_b_re, ssm_b_im, ssm_c_re, ssm_c_im)
    ssm = _ssm(u, params, row(ssm_d), ssm_w_glu.astype(F32), row(ssm_b_glu))
    return _attn_out(attn_sinks.astype(F32), h, p, ssm, zs, q, k, v, za,
                     w_out.astype(F32), row(post_norm_g), pl_w_gate.astype(F32), row(pl_b_gate),
                     pl_w_proj.astype(F32))


def kernel(x, p, pre_norm_g, w_in, ssm_lam_re, ssm_lam_im, ssm_log_step, ssm_b_re, ssm_b_im,
           ssm_c_re, ssm_c_im, ssm_d, ssm_w_glu, ssm_b_glu, attn_sinks, w_out, post_norm_g,
           pl_w_proj, pl_w_gate, pl_b_gate):
    h = x
    for i in range(p.shape[0]):
        h = _layer(h, p[i], pre_norm_g[i], w_in[i], ssm_lam_re[i], ssm_lam_im[i], ssm_log_step[i],
                   ssm_b_re[i], ssm_b_im[i], ssm_c_re[i], ssm_c_im[i], ssm_d[i], ssm_w_glu[i],
                   ssm_b_glu[i], attn_sinks[i], w_out[i], post_norm_g[i], pl_w_proj[i],
                   pl_w_gate[i], pl_b_gate[i])
    return h
```

```python
import functools
import math

import numpy as np
import jax
import jax.numpy as jnp
from jax import lax
from jax.experimental import pallas as pl
from jax.experimental.pallas import tpu as pltpu

F32 = jnp.float32
BF16 = jnp.bfloat16

D_MODEL = 1024
D_SSM = 512
D_ATTN = 512
SSM_GROUP_CH = 16
SSM_GROUPS = D_SSM // SSM_GROUP_CH
SSM_STATE = 64
N_STATE = SSM_GROUPS * SSM_STATE
HEAD_DIM = 64
N_HEADS = D_ATTN // HEAD_DIM
KV_HEADS = 2
Q_PER_KV = N_HEADS // KV_HEADS
D_KV = KV_HEADS * HEAD_DIM
WINDOW = 128
D_PLE = 256
EPS = 1e-6

LANES = 128
assert D_KV == LANES and 2 * HEAD_DIM == LANES

OFF_U, OFF_ZS, OFF_Q, OFF_K, OFF_V, OFF_ZA, D_IN = 0, 512, 1024, 1536, 1664, 1792, 2304

SLAB_CH = LANES
N_SLABS = D_SSM // SLAB_CH
SLAB_STATE = (SLAB_CH // SSM_GROUP_CH) * SSM_STATE
SCAN_COLS = 256

TOK_IN = 1024
NORM_ROWS = 256
T_SCAN = 32
PIECE_ROWS = 128
PERM_BATCH = 8
TOK_OUT = 512
PROJ_COLS = 256
OUT_COLS = 256
VMEM_LIMIT = 48 * 1024 * 1024


def _silu(z):
    return z * jax.nn.sigmoid(z)


def _pipelined(step, n_blocks, body, even_bufs, odd_bufs):
    @pl.when(step == 0)
    def _():
        body(even_bufs, odd_bufs, True, False)

    @pl.when((step > 0) & (step < n_blocks) & (step % 2 == 0))
    def _():
        body(even_bufs, odd_bufs, True, True)

    @pl.when((step < n_blocks) & (step % 2 == 1))
    def _():
        body(odd_bufs, even_bufs, True, True)

    @pl.when(step == n_blocks)
    def _():
        last_read = odd_bufs if n_blocks % 2 == 0 else even_bufs
        body(None, last_read, False, True)


def _in_proj_kernel(x_ref, g_ref, w_ref, u_ref, zs_ref, q_ref, k_ref, v_ref, za_ref, w_scr):
    @pl.when((pl.program_id(0) == 0) & (pl.program_id(1) == 0))
    def _():
        w_scr[...] = w_ref[...].astype(BF16)

    def emit(rows, tile, acc):
        lo = tile * PROJ_COLS
        if lo < OFF_ZS:
            u_ref[rows, lo - OFF_U:lo - OFF_U + PROJ_COLS] = acc.astype(BF16)
        elif lo < OFF_Q:
            zs_ref[rows, lo - OFF_ZS:lo - OFF_ZS + PROJ_COLS] = _silu(acc).astype(BF16)
        elif lo < OFF_K:
            q_ref[rows, lo - OFF_Q:lo - OFF_Q + PROJ_COLS] = (
                acc * (1.0 / math.sqrt(HEAD_DIM))).astype(BF16)
        elif lo < OFF_ZA:
            k_ref[rows, :] = acc[:, :D_KV].astype(BF16)
            v_ref[rows, :] = acc[:, D_KV:].astype(BF16)
        else:
            za_ref[rows, lo - OFF_ZA:lo - OFF_ZA + PROJ_COLS] = _silu(acc).astype(BF16)

    gain = g_ref[...]
    for piece in range(TOK_IN // NORM_ROWS):
        rows = slice(piece * NORM_ROWS, (piece + 1) * NORM_ROWS)
        x = x_ref[rows, :]
        ms = jnp.mean(x * x, axis=-1, keepdims=True)
        hn = (x * lax.rsqrt(ms + EPS) * gain).astype(BF16)
        for tile in range(D_IN // PROJ_COLS):
            cols = slice(tile * PROJ_COLS, (tile + 1) * PROJ_COLS)
            emit(rows, tile, jnp.dot(hn, w_scr[:, cols], preferred_element_type=F32))


def _in_proj(x, g, w_in):
    bsz, seq, _ = x.shape
    tok = lambda width: pl.BlockSpec((None, TOK_IN, width), lambda b, i: (b, i, 0))
    widths = (D_SSM, D_SSM, D_ATTN, D_KV, D_KV, D_ATTN)
    return pl.pallas_call(
        _in_proj_kernel,
        grid=(bsz, seq // TOK_IN),
        in_specs=[
            tok(D_MODEL),
            pl.BlockSpec((1, D_MODEL), lambda b, i: (0, 0)),
            pl.BlockSpec((D_MODEL, D_IN), lambda b, i: (0, 0), pipeline_mode=pl.Buffered(1)),
        ],
        out_specs=tuple(tok(w) for w in widths),
        out_shape=tuple(jax.ShapeDtypeStruct((bsz, seq, w), BF16) for w in widths),
        scratch_shapes=[pltpu.VMEM((D_MODEL, D_IN), BF16)],
        compiler_params=pltpu.CompilerParams(
            dimension_semantics=("arbitrary", "arbitrary"), vmem_limit_bytes=VMEM_LIMIT),
        name="in_proj",
    )(x, g, w_in)


def _ssm_kernel(u_ref, perm_ref, permt_ref, win_ref, l2_ref, wout_ref, wdir_ref, d_ref,
                wglu_f32, bglu_ref, o_ref,
                wglu_ref, sre_ref, sim_ref, st_ref, y_ref,
                g0_ref, ut0_ref, g1_ref, ut1_ref, *, bsz, n_blocks):
    rows = bsz * T_SCAN
    half = rows // 2
    step = pl.program_id(0)

    @pl.when(step == 0)
    def _():
        wglu_ref[...] = wglu_f32[...].astype(BF16)
        sre_ref[...] = jnp.zeros_like(sre_ref)
        sim_ref[...] = jnp.zeros_like(sim_ref)

    def body(write, read, front, back):
        g_w, ut_w = write if front else (None, None)
        g_r, ut_r = read if back else (None, None)
        groups = bsz // PERM_BATCH
        grows = PERM_BATCH * T_SCAN
        if front:
            permuted = [
                jnp.dot(perm_ref[...],
                        u_ref[g * PERM_BATCH:(g + 1) * PERM_BATCH].reshape(grows, D_SSM),
                        preferred_element_type=F32).reshape(T_SCAN, 1, PERM_BATCH, D_SSM)
                for g in range(groups)]
            u_new = jnp.concatenate(permuted, axis=1).reshape(rows, D_SSM).astype(BF16)
            ut_w[...] = u_new
            u_last = ut_r[rows - bsz:, :] if back else jnp.zeros((bsz, D_SSM), BF16)
            u_before = jnp.concatenate([u_last, u_new[half:rows - bsz]], axis=0)
        for j in range(N_SLABS):
            ch = slice(j * SLAB_CH, (j + 1) * SLAB_CH)
            if front:
                drive_in = jnp.concatenate([u_before[:, ch], u_new[:half, ch]], axis=1)
            for c in range(SLAB_STATE // SCAN_COLS):
                lo = j * SLAB_STATE + c * SCAN_COLS
                cre = slice(c * SCAN_COLS, (c + 1) * SCAN_COLS)
                cim = slice(SLAB_STATE + c * SCAN_COLS, SLAB_STATE + (c + 1) * SCAN_COLS)
                if back:
                    l2r = jnp.broadcast_to(l2_ref[0, :, lo:lo + SCAN_COLS], (bsz, SCAN_COLS))
                    l2i = jnp.broadcast_to(l2_ref[1, :, lo:lo + SCAN_COLS], (bsz, SCAN_COLS))
                    sr = sre_ref[:, lo:lo + SCAN_COLS]
                    si = sim_ref[:, lo:lo + SCAN_COLS]
                for piece in range(half // PIECE_ROWS):
                    rk = slice(piece * PIECE_ROWS, (piece + 1) * PIECE_ROWS)
                    if front:
                        g_w[j, rk, cre] = jnp.dot(drive_in[rk], win_ref[0, j, :, cre],
                                                  preferred_element_type=F32)
                        g_w[j, rk, cim] = jnp.dot(drive_in[rk], win_ref[1, j, :, cre],
                                                  preferred_element_type=F32)
                    if not back:
                        continue
                    for k in range(rk.start // bsz, rk.stop // bsz):
                        r = slice(k * bsz, (k + 1) * bsz)
                        sr, si = (l2r * sr - l2i * si + g_r[j, r, cre],
                                  l2r * si + l2i * sr + g_r[j, r, cim])
                        st_ref[j, r, cre] = sr.astype(BF16)
                        st_ref[j, r, cim] = si.astype(BF16)
                    yk = (jnp.dot(st_ref[j, rk, cre], wout_ref[0, j, cre, :],
                                  preferred_element_type=F32)
                          + jnp.dot(st_ref[j, rk, cim], wout_ref[1, j, cre, :],
                                    preferred_element_type=F32))
                    odd_rows = slice(half + rk.start, half + rk.stop)
                    if c == 0:
                        y_ref[rk, ch] = yk[:, :SLAB_CH]
                        y_ref[odd_rows, ch] = yk[:, SLAB_CH:] + jnp.dot(
                            ut_r[odd_rows, ch], wdir_ref[j], preferred_element_type=F32)
                    else:
                        y_ref[rk, ch] += yk[:, :SLAB_CH]
                        y_ref[odd_rows, ch] += yk[:, SLAB_CH:]
                if back:
                    sre_ref[:, lo:lo + SCAN_COLS] = sr
                    sim_ref[:, lo:lo + SCAN_COLS] = si
        if not back:
            return

        y = y_ref[...] + d_ref[...] * ut_r[...].astype(F32)
        g = jax.nn.gelu(y)
        gate = jnp.dot(g.astype(BF16), wglu_ref[...], preferred_element_type=F32) + bglu_ref[...]
        out_tb = (g * jax.nn.sigmoid(gate)).reshape(T_SCAN, groups, PERM_BATCH, D_SSM)
        for grp in range(groups):
            out_g = out_tb[:, grp].reshape(grows, D_SSM).astype(BF16)
            out_bt = jnp.dot(permt_ref[...], out_g, preferred_element_type=F32).astype(BF16)
            o_ref[grp * PERM_BATCH:(grp + 1) * PERM_BATCH] = out_bt.reshape(
                PERM_BATCH, T_SCAN, D_SSM)

    _pipelined(step, n_blocks, body, (g0_ref, ut0_ref), (g1_ref, ut1_ref))


def _row_permutation():
    rows = PERM_BATCH * T_SCAN
    perm = np.zeros((rows, rows), np.float32)
    t, b = np.meshgrid(np.arange(T_SCAN), np.arange(PERM_BATCH), indexing="ij")
    dst = (t % 2) * (rows // 2) + (t // 2) * PERM_BATCH + b
    perm[dst.ravel(), (b * T_SCAN + t).ravel()] = 1.0
    return perm


def _ssm(u, params, d, wglu, bglu):
    bsz, seq, _ = u.shape
    rows = T_SCAN * bsz
    half = rows // 2
    nblk = seq // T_SCAN
    assert bsz % PERM_BATCH == 0
    perm = _row_permutation()
    full = lambda shape: pl.BlockSpec(shape, lambda f: (0,) * len(shape))
    g_scratch = pltpu.VMEM((N_SLABS, half, 2 * SLAB_STATE), F32)
    ut_scratch = pltpu.VMEM((rows, D_SSM), BF16)
    return pl.pallas_call(
        functools.partial(_ssm_kernel, bsz=bsz, n_blocks=nblk),
        grid=(nblk + 1,),
        in_specs=[
            pl.BlockSpec((bsz, T_SCAN, D_SSM), lambda f: (0, jnp.minimum(f, nblk - 1), 0)),
            full(perm.shape), full(perm.shape),
            full((2, N_SLABS, 2 * SLAB_CH, SLAB_STATE)), full((2, 1, N_STATE)),
            full((2, N_SLABS, SLAB_STATE, 2 * SLAB_CH)), full((N_SLABS, SLAB_CH, SLAB_CH)),
            full((1, D_SSM)),
            pl.BlockSpec((D_SSM, D_SSM), lambda f: (0, 0), pipeline_mode=pl.Buffered(1)),
            full((1, D_SSM)),
        ],
        out_specs=pl.BlockSpec((bsz, T_SCAN, D_SSM), lambda f: (0, jnp.maximum(f - 1, 0), 0)),
        out_shape=jax.ShapeDtypeStruct((bsz, seq, D_SSM), BF16),
        scratch_shapes=[
            pltpu.VMEM((D_SSM, D_SSM), BF16),
            pltpu.VMEM((bsz, N_STATE), F32),
            pltpu.VMEM((bsz, N_STATE), F32),
            pltpu.VMEM((N_SLABS, half, 2 * SLAB_STATE), BF16),
            pltpu.VMEM((rows, D_SSM), F32),
            g_scratch, ut_scratch, g_scratch, ut_scratch,
        ],
        compiler_params=pltpu.CompilerParams(
            dimension_semantics=("arbitrary",), vmem_limit_bytes=VMEM_LIMIT),
        name="s5_scan",
    )(u, jnp.asarray(perm, BF16), jnp.asarray(perm.T, BF16), *params, d, wglu, bglu)


def _attention_pieces(first, sink_ref, q_ref, k_ref, v_ref, kp_ref, vp_ref, za_ref, bias_ref,
                      half_ref, a_ref):
    low_q = lax.broadcasted_iota(jnp.int32, (WINDOW, LANES), 1) < HEAD_DIM
    kv_cache = {}

    def swap_halves(t):
        packed = pltpu.bitcast(t, jnp.uint32)
        return pltpu.bitcast(pltpu.roll(packed, HEAD_DIM, axis=1), BF16)

    def keys_values(jj, g):
        if jj not in kv_cache:
            rows = slice(jj * WINDOW, (jj + 1) * WINDOW)
            if jj == 0:
                kp, vp = kp_ref[...], vp_ref[...]
            else:
                prev = slice((jj - 1) * WINDOW, jj * WINDOW)
                kp, vp = k_ref[prev, :], v_ref[prev, :]
            kk = jnp.concatenate([kp, k_ref[rows, :]], axis=0)
            vv = jnp.concatenate([vp, v_ref[rows, :]], axis=0)
            kv_cache[jj] = ((kk, swap_halves(kk)), (vv, swap_halves(vv)))
        if (jj, g) not in kv_cache:
            keep_even, keep_odd = half_ref[0], half_ref[1]
            k2, v2 = kv_cache[jj]
            kcat = jnp.concatenate([k2[g] * keep_even, k2[1 - g] * keep_odd], axis=0)
            vcat = jnp.concatenate(
                [jnp.concatenate([v2[g] * keep_even, keep_even], axis=1),
                 jnp.concatenate([v2[1 - g] * keep_odd, keep_odd], axis=1)], axis=0)
            kv_cache[(jj, g)] = (kcat, vcat)
        return kv_cache[(jj, g)]

    group_pairs = Q_PER_KV // 2
    items = [(jj, g) for jj in range(TOK_OUT // WINDOW) for g in range(KV_HEADS)]
    scores, probs = {}, {}

    def where(i):
        jj, g = items[i]
        pairs = [g * group_pairs + pp for pp in range(group_pairs)]
        return (jj, g, slice(jj * WINDOW, (jj + 1) * WINDOW), pairs,
                [slice(pair * LANES, (pair + 1) * LANES) for pair in pairs])

    def part(stacked, pp):
        return stacked[pp * WINDOW:(pp + 1) * WINDOW]

    def stage_scores(i):
        jj, g, rows, pairs, lanes = where(i)
        kcat, _ = keys_values(jj, g)
        q_stack = jnp.concatenate([q_ref[rows, ln] for ln in lanes], axis=0)
        s = lax.dot_general(q_stack, kcat, (((1,), (1,)), ((), ())),
                            preferred_element_type=F32)
        scores[i] = [
            part(s, pp) + (bias_ref[pair + (N_HEADS // 2) * first] if jj == 0 else bias_ref[pair])
            for pp, pair in enumerate(pairs)]

    def stage_softmax(i):
        _, _, _, pairs, _ = where(i)
        es, sink_terms = [], []
        for s, pair in zip(scores.pop(i), pairs):
            sink_e, sink_o = sink_ref[2 * pair], sink_ref[2 * pair + 1]
            m_e = jnp.maximum(jnp.max(s[:, :2 * WINDOW], axis=-1, keepdims=True), sink_e)
            m_o = jnp.maximum(jnp.max(s[:, 2 * WINDOW:], axis=-1, keepdims=True), sink_o)
            es.append(jnp.concatenate([jnp.exp(s[:, :2 * WINDOW] - m_e),
                                       jnp.exp(s[:, 2 * WINDOW:] - m_o)], axis=1).astype(BF16))
            sink_terms.append(jnp.where(low_q, jnp.exp(sink_e - m_e), jnp.exp(sink_o - m_o)))
        probs[i] = (jnp.concatenate(es, axis=0), sink_terms)

    def stage_output(i):
        jj, g, rows, _, lanes = where(i)
        _, vcat = keys_values(jj, g)
        e, sink_terms = probs.pop(i)
        o = jnp.dot(e, vcat, preferred_element_type=F32)
        for pp, (ln, sink_term) in enumerate(zip(lanes, sink_terms)):
            o_p = part(o, pp)
            den = o_p[:, LANES:] + sink_term
            a_ref[rows, ln] = (o_p[:, :LANES] / den * za_ref[rows, ln].astype(F32)).astype(BF16)

    def slot(it):
        if it < len(items):
            stage_scores(it)
        if 0 <= it - 1 < len(items):
            stage_softmax(it - 1)
        if 0 <= it - 2 < len(items):
            stage_output(it - 2)

    return [functools.partial(slot, it) for it in range(len(items) + 2)]


def _attn_out_kernel(sink_ref, x_ref, p_ref, ssm_ref, zs_ref, q_ref, k_ref, v_ref, kp_ref, vp_ref,
                     za_ref, bias_ref, half_ref, wout_f32, pg_ref, wgate_f32, bgate_ref, wproj_f32,
                     o_ref, wout_ref, wgate_ref, wproj_ref, a0_ref, a1_ref,
                     *, blocks_per_seq, n_blocks):
    step = pl.program_id(0)
    seq_block = jnp.minimum(step, n_blocks - 1) % blocks_per_seq
    first = (seq_block == 0).astype(jnp.int32)

    @pl.when(step == 0)
    def _():
        wout_ref[...] = wout_f32[...].astype(BF16)
        wgate_ref[...] = wgate_f32[...].astype(BF16)
        wproj_ref[...] = wproj_f32[...].astype(BF16)

    def body(a_w, a_r, front, back):
        pieces = []
        if front:
            pieces = _attention_pieces(first, sink_ref, q_ref, k_ref, v_ref, kp_ref, vp_ref,
                                       za_ref, bias_ref, half_ref, a_w)
        n_chunks = D_MODEL // OUT_COLS
        per_chunk = max(len(pieces) - 2, 0) // (2 * n_chunks)
        pieces = iter(pieces)

        def run_pieces(count=per_chunk):
            for piece in (list(pieces) if count is None else [p for _, p in zip(range(count), pieces)]):
                piece()

        if not back:
            run_pieces(None)
            return
        run_pieces(2)

        s5 = (ssm_ref[...].astype(F32) * zs_ref[...].astype(F32)).astype(BF16)
        a_prev = a_r[...]
        mixed = []
        for n in range(n_chunks):
            cols = slice(n * OUT_COLS, (n + 1) * OUT_COLS)
            mixed.append(jnp.dot(s5, wout_ref[:D_SSM, cols], preferred_element_type=F32)
                         + jnp.dot(a_prev, wout_ref[D_SSM:, cols], preferred_element_type=F32))
            run_pieces()
        mixed = jnp.concatenate(mixed, axis=1)
        ms = jnp.mean(mixed * mixed, axis=-1, keepdims=True)
        h1 = x_ref[...] + mixed * lax.rsqrt(ms + EPS) * pg_ref[...]
        h1b = h1.astype(BF16)
        pb = p_ref[...].astype(BF16)
        for n in range(n_chunks):
            cols = slice(n * OUT_COLS, (n + 1) * OUT_COLS)
            gate = jax.nn.sigmoid(
                jnp.dot(h1b, wgate_ref[:, cols], preferred_element_type=F32) + bgate_ref[:, cols])
            ple = jnp.dot(pb, wproj_ref[:, cols], preferred_element_type=F32)
            o_ref[:, cols] = h1[:, cols] + gate * ple
            run_pieces()
        run_pieces(None)

    _pipelined(step, n_blocks, body, a0_ref, a1_ref)


def _attn_bias_table():
    q_idx = np.arange(WINDOW)[:, None]
    s_idx = np.arange(2 * WINDOW)[None, :]
    dist = q_idx + WINDOW - s_idx
    valid = (dist >= 0) & (dist < WINDOW)
    slopes = np.exp2(-8.0 * (np.arange(N_HEADS, dtype=np.float32) + 1.0) / N_HEADS).astype(np.float32)
    bias = -slopes[:, None, None] * dist.astype(np.float32)[None]
    normal = np.where(valid[None], bias, -np.inf)
    first = np.where((valid & (s_idx >= WINDOW))[None], bias, -np.inf)
    paired = lambda t: t.reshape(N_HEADS // 2, 2, WINDOW, 2 * WINDOW).transpose(0, 2, 1, 3).reshape(
        N_HEADS // 2, WINDOW, 4 * WINDOW)
    return np.concatenate([paired(normal), paired(first)], axis=0).astype(np.float32)


def _attn_out(sinks, x, p, ssm, zs, q, k, v, za, wout, pg, wgate, bgate, wproj):
    bsz, seq, _ = x.shape
    sub = TOK_OUT // WINDOW
    per_seq = seq // TOK_OUT
    n_blocks = bsz * per_seq

    def cur(f):
        f = jnp.minimum(f, n_blocks - 1)
        return f // per_seq, f % per_seq

    def prv(f):
        f = jnp.maximum(f - 1, 0)
        return f // per_seq, f % per_seq

    front = lambda width: pl.BlockSpec((None, TOK_OUT, width), lambda f: (*cur(f), 0))
    back = lambda width: pl.BlockSpec((None, TOK_OUT, width), lambda f: (*prv(f), 0))
    prev_kv = pl.BlockSpec(
        (None, WINDOW, D_KV),
        lambda f: (cur(f)[0], jnp.maximum(cur(f)[1] * sub - 1, 0), 0))
    full = lambda shape: pl.BlockSpec(shape, lambda f: (0,) * len(shape))
    once = lambda shape: pl.BlockSpec(shape, lambda f: (0,) * len(shape),
                                      pipeline_mode=pl.Buffered(1))
    bias = jnp.asarray(_attn_bias_table())
    lane_is_even = np.broadcast_to(np.arange(LANES) < HEAD_DIM, (2 * WINDOW, LANES))
    half = jnp.asarray(np.stack([lane_is_even, ~lane_is_even]), BF16)
    a_scratch = pltpu.VMEM((TOK_OUT, D_ATTN), BF16)
    return pl.pallas_call(
        functools.partial(_attn_out_kernel, blocks_per_seq=per_seq, n_blocks=n_blocks),
        grid=(n_blocks + 1,),
        in_specs=[
            pl.BlockSpec(memory_space=pltpu.SMEM),
            back(D_MODEL), back(D_PLE), back(D_SSM), back(D_SSM), front(D_ATTN),
            front(D_KV), front(D_KV), prev_kv, prev_kv, front(D_ATTN),
            once(bias.shape), once(half.shape),
            once((D_MODEL, D_MODEL)), full((1, D_MODEL)),
            once((D_MODEL, D_MODEL)), full((1, D_MODEL)), once((D_PLE, D_MODEL)),
        ],
        out_specs=back(D_MODEL),
        out_shape=jax.ShapeDtypeStruct((bsz, seq, D_MODEL), F32),
        scratch_shapes=[pltpu.VMEM((D_MODEL, D_MODEL), BF16), pltpu.VMEM((D_MODEL, D_MODEL), BF16),
                        pltpu.VMEM((D_PLE, D_MODEL), BF16), a_scratch, a_scratch],
        compiler_params=pltpu.CompilerParams(
            dimension_semantics=("arbitrary",), vmem_limit_bytes=VMEM_LIMIT),
        name="attn_out",
    )(sinks, x, p, ssm, zs, q, k, v, k, v, za, bias, half, wout, pg, wgate, bgate, wproj)


def _ssm_params(lam_re, lam_im, log_step, b_re, b_im, c_re, c_im):
    lr, li = lam_re.astype(F32), lam_im.astype(F32)
    step = jnp.exp(log_step.astype(F32))[:, None]
    mag = jnp.exp(lr * step)
    lbr, lbi = mag * jnp.cos(li * step), mag * jnp.sin(li * step)
    den = lr * lr + li * li
    fr = ((lbr - 1.0) * lr + lbi * li) / den
    fi = (lbi * lr - (lbr - 1.0) * li) / den
    br, bi = b_re.astype(F32), b_im.astype(F32)
    bbr = fr[..., None] * br - fi[..., None] * bi
    bbi = fr[..., None] * bi + fi[..., None] * br
    cr, ci = c_re.astype(F32), c_im.astype(F32)
    l2 = jnp.stack([lbr * lbr - lbi * lbi, 2.0 * lbr * lbi])
    lbbr = lbr[..., None] * bbr - lbi[..., None] * bbi
    lbbi = lbr[..., None] * bbi + lbi[..., None] * bbr
    clr = cr * lbr[:, None, :] - ci * lbi[:, None, :]
    cli = cr * lbi[:, None, :] + ci * lbr[:, None, :]
    cb = jnp.sum(cr[:, :, :, None] * bbr[:, None, :, :] - ci[:, :, :, None] * bbi[:, None, :, :],
                 axis=2)
    gps = SLAB_CH // SSM_GROUP_CH
    n, p = SSM_STATE, SSM_GROUP_CH
    same_group = (np.arange(SLAB_CH)[:, None] // p == np.arange(SLAB_STATE)[None, :] // n)
    expand_p = np.tile(np.eye(p, dtype=np.float32), (gps, 1))
    expand_n = np.tile(np.eye(n, dtype=np.float32), (gps, 1))

    w_in = jnp.stack([jnp.stack([lbbr, bbr]), jnp.stack([lbbi, bbi])])
    w_in = w_in.reshape(2, 2, N_SLABS, SLAB_STATE, p)
    w_in = jnp.einsum('rp,absqp->asbrq', expand_p, w_in) * same_group.astype(np.float32)
    w_in = w_in.astype(BF16).reshape(2, N_SLABS, 2 * SLAB_CH, SLAB_STATE)
    w_out = jnp.stack([jnp.stack([cr, clr]), jnp.stack([-ci, -cli])])
    w_out = w_out.reshape(2, 2, N_SLABS, SLAB_CH, n)
    w_out = jnp.einsum('rn,aesqn->aesrq', expand_n, w_out) * same_group.T.astype(np.float32)
    w_out = jnp.concatenate([w_out[:, 0], w_out[:, 1]], axis=-1).astype(BF16)
    same_group_pp = same_group[:, ::n // p]
    w_dir = jnp.einsum('rq,sxq->srx', expand_p, cb.reshape(N_SLABS, SLAB_CH, p))
    w_dir = (w_dir * same_group_pp.astype(np.float32)).astype(BF16)
    return w_in, l2.reshape(2, 1, N_STATE), w_out, w_dir


def _layer(h, p, pre_norm_g, w_in, ssm_lam_re, ssm_lam_im, ssm_log_step, ssm_b_re, ssm_b_im,
           ssm_c_re, ssm_c_im, ssm_d, ssm_w_glu, ssm_b_glu, attn_sinks, w_out, post_norm_g,
           pl_w_proj, pl_w_gate, pl_b_gate):
    row = lambda a: a.astype(F32).reshape(1, -1)
    u, zs, q, k, v, za = _in_proj(h, row(pre_norm_g), w_in.astype(F32))
    params = _ssm_params(
        ssm_lam_re, ssm_lam_im, ssm_log_step, ssm_b_re, ssm_b_im, ssm_c_re, ssm_c_im)
    ssm = _ssm(u, params, row(ssm_d), ssm_w_glu.astype(F32), row(ssm_b_glu))
    return _attn_out(attn_sinks.astype(F32), h, p, ssm, zs, q, k, v, za,
                     w_out.astype(F32), row(post_norm_g), pl_w_gate.astype(F32), row(pl_b_gate),
                     pl_w_proj.astype(F32))


def kernel(x, p, pre_norm_g, w_in, ssm_lam_re, ssm_lam_im, ssm_log_step, ssm_b_re, ssm_b_im,
           ssm_c_re, ssm_c_im, ssm_d, ssm_w_glu, ssm_b_glu, attn_sinks, w_out, post_norm_g,
           pl_w_proj, pl_w_gate, pl_b_gate):
    h = x
    for i in range(p.shape[0]):
        h = _layer(h, p[i], pre_norm_g[i], w_in[i], ssm_lam_re[i], ssm_lam_im[i], ssm_log_step[i],
                   ssm_b_re[i], ssm_b_im[i], ssm_c_re[i], ssm_c_im[i], ssm_d[i], ssm_w_glu[i],
                   ssm_b_glu[i], attn_sinks[i], w_out[i], post_norm_g[i], pl_w_proj[i],
                   pl_w_gate[i], pl_b_gate[i])
    return h
```

```python
import functools
import math

import numpy as np
import jax
import jax.numpy as jnp
from jax import lax
from jax.experimental import pallas as pl
from jax.experimental.pallas import tpu as pltpu

F32 = jnp.float32
BF16 = jnp.bfloat16

D_MODEL = 1024
D_SSM = 512
D_ATTN = 512
SSM_GROUP_CH = 16
SSM_GROUPS = D_SSM // SSM_GROUP_CH
SSM_STATE = 64
N_STATE = SSM_GROUPS * SSM_STATE
HEAD_DIM = 64
N_HEADS = D_ATTN // HEAD_DIM
KV_HEADS = 2
Q_PER_KV = N_HEADS // KV_HEADS
D_KV = KV_HEADS * HEAD_DIM
WINDOW = 128
D_PLE = 256
EPS = 1e-6

LANES = 128
assert D_KV == LANES and 2 * HEAD_DIM == LANES

OFF_U, OFF_ZS, OFF_Q, OFF_K, OFF_V, OFF_ZA, D_IN = 0, 512, 1024, 1536, 1664, 1792, 2304

SLAB_CH = LANES
N_SLABS = D_SSM // SLAB_CH
SLAB_STATE = (SLAB_CH // SSM_GROUP_CH) * SSM_STATE
SCAN_COLS = 256

TOK_IN = 1024
NORM_ROWS = 256
T_SCAN = 32
PIECE_ROWS = 128
PERM_BATCH = 8
TOK_OUT = 1024
PROJ_COLS = 256
OUT_COLS = 256
VMEM_LIMIT = 60 * 1024 * 1024


def _silu(z):
    return z * jax.nn.sigmoid(z)


def _pipelined(step, n_blocks, body, even_bufs, odd_bufs):
    @pl.when(step == 0)
    def _():
        body(even_bufs, odd_bufs, True, False)

    @pl.when((step > 0) & (step < n_blocks) & (step % 2 == 0))
    def _():
        body(even_bufs, odd_bufs, True, True)

    @pl.when((step < n_blocks) & (step % 2 == 1))
    def _():
        body(odd_bufs, even_bufs, True, True)

    @pl.when(step == n_blocks)
    def _():
        last_read = odd_bufs if n_blocks % 2 == 0 else even_bufs
        body(None, last_read, False, True)


def _in_proj_kernel(x_ref, g_ref, w_ref, u_ref, zs_ref, q_ref, k_ref, v_ref, za_ref, w_scr):
    @pl.when((pl.program_id(0) == 0) & (pl.program_id(1) == 0))
    def _():
        w_scr[...] = w_ref[...].astype(BF16)

    def emit(rows, tile, acc):
        lo = tile * PROJ_COLS
        if lo < OFF_ZS:
            u_ref[rows, lo - OFF_U:lo - OFF_U + PROJ_COLS] = acc.astype(BF16)
        elif lo < OFF_Q:
            zs_ref[rows, lo - OFF_ZS:lo - OFF_ZS + PROJ_COLS] = _silu(acc).astype(BF16)
        elif lo < OFF_K:
            q_ref[rows, lo - OFF_Q:lo - OFF_Q + PROJ_COLS] = (
                acc * (1.0 / math.sqrt(HEAD_DIM))).astype(BF16)
        elif lo < OFF_ZA:
            k_ref[rows, :] = acc[:, :D_KV].astype(BF16)
            v_ref[rows, :] = acc[:, D_KV:].astype(BF16)
        else:
            za_ref[rows, lo - OFF_ZA:lo - OFF_ZA + PROJ_COLS] = _silu(acc).astype(BF16)

    gain = g_ref[...]
    for piece in range(TOK_IN // NORM_ROWS):
        rows = slice(piece * NORM_ROWS, (piece + 1) * NORM_ROWS)
        x = x_ref[rows, :]
        ms = jnp.mean(x * x, axis=-1, keepdims=True)
        hn = (x * lax.rsqrt(ms + EPS) * gain).astype(BF16)
        for tile in range(D_IN // PROJ_COLS):
            cols = slice(tile * PROJ_COLS, (tile + 1) * PROJ_COLS)
            emit(rows, tile, jnp.dot(hn, w_scr[:, cols], preferred_element_type=F32))


def _in_proj(x, g, w_in):
    bsz, seq, _ = x.shape
    tok = lambda width: pl.BlockSpec((None, TOK_IN, width), lambda b, i: (b, i, 0))
    widths = (D_SSM, D_SSM, D_ATTN, D_KV, D_KV, D_ATTN)
    return pl.pallas_call(
        _in_proj_kernel,
        grid=(bsz, seq // TOK_IN),
        in_specs=[
            tok(D_MODEL),
            pl.BlockSpec((1, D_MODEL), lambda b, i: (0, 0)),
            pl.BlockSpec((D_MODEL, D_IN), lambda b, i: (0, 0), pipeline_mode=pl.Buffered(1)),
        ],
        out_specs=tuple(tok(w) for w in widths),
        out_shape=tuple(jax.ShapeDtypeStruct((bsz, seq, w), BF16) for w in widths),
        scratch_shapes=[pltpu.VMEM((D_MODEL, D_IN), BF16)],
        compiler_params=pltpu.CompilerParams(
            dimension_semantics=("arbitrary", "arbitrary"), vmem_limit_bytes=VMEM_LIMIT),
        name="in_proj",
    )(x, g, w_in)


def _ssm_kernel(u_ref, perm_ref, permt_ref, win_ref, l2_ref, wout_ref, wdir_ref, d_ref,
                wglu_f32, bglu_ref, o_ref,
                wglu_ref, sre_ref, sim_ref, st_ref, y_ref,
                g0_ref, ut0_ref, g1_ref, ut1_ref, *, bsz, n_blocks):
    rows = bsz * T_SCAN
    half = rows // 2
    step = pl.program_id(0)

    @pl.when(step == 0)
    def _():
        wglu_ref[...] = wglu_f32[...].astype(BF16)
        sre_ref[...] = jnp.zeros_like(sre_ref)
        sim_ref[...] = jnp.zeros_like(sim_ref)

    def body(write, read, front, back):
        g_w, ut_w = write if front else (None, None)
        g_r, ut_r = read if back else (None, None)
        groups = bsz // PERM_BATCH
        grows = PERM_BATCH * T_SCAN
        if front:
            permuted = [
                jnp.dot(perm_ref[...],
                        u_ref[g * PERM_BATCH:(g + 1) * PERM_BATCH].reshape(grows, D_SSM),
                        preferred_element_type=F32).reshape(T_SCAN, 1, PERM_BATCH, D_SSM)
                for g in range(groups)]
            u_new = jnp.concatenate(permuted, axis=1).reshape(rows, D_SSM).astype(BF16)
            ut_w[...] = u_new
            u_last = ut_r[rows - bsz:, :] if back else jnp.zeros((bsz, D_SSM), BF16)
            u_before = jnp.concatenate([u_last, u_new[half:rows - bsz]], axis=0)
        for j in range(N_SLABS):
            ch = slice(j * SLAB_CH, (j + 1) * SLAB_CH)
            if front:
                drive_in = jnp.concatenate([u_before[:, ch], u_new[:half, ch]], axis=1)
            for c in range(SLAB_STATE // SCAN_COLS):
                lo = j * SLAB_STATE + c * SCAN_COLS
                cre = slice(c * SCAN_COLS, (c + 1) * SCAN_COLS)
                cim = slice(SLAB_STATE + c * SCAN_COLS, SLAB_STATE + (c + 1) * SCAN_COLS)
                if back:
                    l2r = jnp.broadcast_to(l2_ref[0, :, lo:lo + SCAN_COLS], (bsz, SCAN_COLS))
                    l2i = jnp.broadcast_to(l2_ref[1, :, lo:lo + SCAN_COLS], (bsz, SCAN_COLS))
                    sr = sre_ref[:, lo:lo + SCAN_COLS]
                    si = sim_ref[:, lo:lo + SCAN_COLS]
                for piece in range(half // PIECE_ROWS):
                    rk = slice(piece * PIECE_ROWS, (piece + 1) * PIECE_ROWS)
                    if front:
                        g_w[j, rk, cre] = jnp.dot(drive_in[rk], win_ref[0, j, :, cre],
                                                  preferred_element_type=F32)
                        g_w[j, rk, cim] = jnp.dot(drive_in[rk], win_ref[1, j, :, cre],
                                                  preferred_element_type=F32)
                    if not back:
                        continue
                    for k in range(rk.start // bsz, rk.stop // bsz):
                        r = slice(k * bsz, (k + 1) * bsz)
                        sr, si = (l2r * sr - l2i * si + g_r[j, r, cre],
                                  l2r * si + l2i * sr + g_r[j, r, cim])
                        st_ref[j, r, cre] = sr.astype(BF16)
                        st_ref[j, r, cim] = si.astype(BF16)
                    yk = (jnp.dot(st_ref[j, rk, cre], wout_ref[0, j, cre, :],
                                  preferred_element_type=F32)
                          + jnp.dot(st_ref[j, rk, cim], wout_ref[1, j, cre, :],
                                    preferred_element_type=F32))
                    odd_rows = slice(half + rk.start, half + rk.stop)
                    if c == 0:
                        y_ref[rk, ch] = yk[:, :SLAB_CH]
                        y_ref[odd_rows, ch] = yk[:, SLAB_CH:] + jnp.dot(
                            ut_r[odd_rows, ch], wdir_ref[j], preferred_element_type=F32)
                    else:
                        y_ref[rk, ch] += yk[:, :SLAB_CH]
                        y_ref[odd_rows, ch] += yk[:, SLAB_CH:]
                if back:
                    sre_ref[:, lo:lo + SCAN_COLS] = sr
                    sim_ref[:, lo:lo + SCAN_COLS] = si
        if not back:
            return

        y = y_ref[...] + d_ref[...] * ut_r[...].astype(F32)
        g = jax.nn.gelu(y)
        gate = jnp.dot(g.astype(BF16), wglu_ref[...], preferred_element_type=F32) + bglu_ref[...]
        out_tb = (g * jax.nn.sigmoid(gate)).reshape(T_SCAN, groups, PERM_BATCH, D_SSM)
        for grp in range(groups):
            out_g = out_tb[:, grp].reshape(grows, D_SSM).astype(BF16)
            out_bt = jnp.dot(permt_ref[...], out_g, preferred_element_type=F32).astype(BF16)
            o_ref[grp * PERM_BATCH:(grp + 1) * PERM_BATCH] = out_bt.reshape(
                PERM_BATCH, T_SCAN, D_SSM)

    _pipelined(step, n_blocks, body, (g0_ref, ut0_ref), (g1_ref, ut1_ref))


def _row_permutation():
    rows = PERM_BATCH * T_SCAN
    perm = np.zeros((rows, rows), np.float32)
    t, b = np.meshgrid(np.arange(T_SCAN), np.arange(PERM_BATCH), indexing="ij")
    dst = (t % 2) * (rows // 2) + (t // 2) * PERM_BATCH + b
    perm[dst.ravel(), (b * T_SCAN + t).ravel()] = 1.0
    return perm


def _ssm(u, params, d, wglu, bglu):
    bsz, seq, _ = u.shape
    rows = T_SCAN * bsz
    half = rows // 2
    nblk = seq // T_SCAN
    assert bsz % PERM_BATCH == 0
    perm = _row_permutation()
    full = lambda shape: pl.BlockSpec(shape, lambda f: (0,) * len(shape))
    g_scratch = pltpu.VMEM((N_SLABS, half, 2 * SLAB_STATE), F32)
    ut_scratch = pltpu.VMEM((rows, D_SSM), BF16)
    return pl.pallas_call(
        functools.partial(_ssm_kernel, bsz=bsz, n_blocks=nblk),
        grid=(nblk + 1,),
        in_specs=[
            pl.BlockSpec((bsz, T_SCAN, D_SSM), lambda f: (0, jnp.minimum(f, nblk - 1), 0)),
            full(perm.shape), full(perm.shape),
            full((2, N_SLABS, 2 * SLAB_CH, SLAB_STATE)), full((2, 1, N_STATE)),
            full((2, N_SLABS, SLAB_STATE, 2 * SLAB_CH)), full((N_SLABS, SLAB_CH, SLAB_CH)),
            full((1, D_SSM)),
            pl.BlockSpec((D_SSM, D_SSM), lambda f: (0, 0), pipeline_mode=pl.Buffered(1)),
            full((1, D_SSM)),
        ],
        out_specs=pl.BlockSpec((bsz, T_SCAN, D_SSM), lambda f: (0, jnp.maximum(f - 1, 0), 0)),
        out_shape=jax.ShapeDtypeStruct((bsz, seq, D_SSM), BF16),
        scratch_shapes=[
            pltpu.VMEM((D_SSM, D_SSM), BF16),
            pltpu.VMEM((bsz, N_STATE), F32),
            pltpu.VMEM((bsz, N_STATE), F32),
            pltpu.VMEM((N_SLABS, half, 2 * SLAB_STATE), BF16),
            pltpu.VMEM((rows, D_SSM), F32),
            g_scratch, ut_scratch, g_scratch, ut_scratch,
        ],
        compiler_params=pltpu.CompilerParams(
            dimension_semantics=("arbitrary",), vmem_limit_bytes=VMEM_LIMIT),
        name="s5_scan",
    )(u, jnp.asarray(perm, BF16), jnp.asarray(perm.T, BF16), *params, d, wglu, bglu)


def _attention_pieces(first, sink_ref, q_ref, k_ref, v_ref, kp_ref, vp_ref, za_ref, bias_ref,
                      half_ref, a_ref):
    low_q = lax.broadcasted_iota(jnp.int32, (WINDOW, LANES), 1) < HEAD_DIM
    kv_cache = {}

    def swap_halves(t):
        packed = pltpu.bitcast(t, jnp.uint32)
        return pltpu.bitcast(pltpu.roll(packed, HEAD_DIM, axis=1), BF16)

    def keys_values(jj, g):
        if jj not in kv_cache:
            rows = slice(jj * WINDOW, (jj + 1) * WINDOW)
            if jj == 0:
                kp, vp = kp_ref[...], vp_ref[...]
            else:
                prev = slice((jj - 1) * WINDOW, jj * WINDOW)
                kp, vp = k_ref[prev, :], v_ref[prev, :]
            kk = jnp.concatenate([kp, k_ref[rows, :]], axis=0)
            vv = jnp.concatenate([vp, v_ref[rows, :]], axis=0)
            kv_cache[jj] = ((kk, swap_halves(kk)), (vv, swap_halves(vv)))
        if (jj, g) not in kv_cache:
            keep_even, keep_odd = half_ref[0], half_ref[1]
            k2, v2 = kv_cache[jj]
            kcat = jnp.concatenate([k2[g] * keep_even, k2[1 - g] * keep_odd], axis=0)
            vcat = jnp.concatenate(
                [jnp.concatenate([v2[g] * keep_even, keep_even], axis=1),
                 jnp.concatenate([v2[1 - g] * keep_odd, keep_odd], axis=1)], axis=0)
            kv_cache[(jj, g)] = (kcat, vcat)
        return kv_cache[(jj, g)]

    items = [(jj, pair) for jj in range(TOK_OUT // WINDOW) for pair in range(N_HEADS // 2)]
    scores, probs = {}, {}

    def where(i):
        jj, pair = items[i]
        return (jj, pair, slice(jj * WINDOW, (jj + 1) * WINDOW),
                slice(pair * LANES, (pair + 1) * LANES))

    def stage_scores(i):
        jj, pair, rows, lanes = where(i)
        kcat, _ = keys_values(jj, pair // (Q_PER_KV // 2))
        s = lax.dot_general(q_ref[rows, lanes], kcat, (((1,), (1,)), ((), ())),
                            preferred_element_type=F32)
        scores[i] = s + (bias_ref[pair + (N_HEADS // 2) * first] if jj == 0 else bias_ref[pair])

    def stage_softmax(i):
        _, pair, _, _ = where(i)
        s = scores.pop(i)
        sink_e, sink_o = sink_ref[2 * pair], sink_ref[2 * pair + 1]
        m_e = jnp.maximum(jnp.max(s[:, :2 * WINDOW], axis=-1, keepdims=True), sink_e)
        m_o = jnp.maximum(jnp.max(s[:, 2 * WINDOW:], axis=-1, keepdims=True), sink_o)
        e = jnp.concatenate([jnp.exp(s[:, :2 * WINDOW] - m_e),
                             jnp.exp(s[:, 2 * WINDOW:] - m_o)], axis=1).astype(BF16)
        probs[i] = (e, jnp.where(low_q, jnp.exp(sink_e - m_e), jnp.exp(sink_o - m_o)))

    def stage_output(i):
        jj, pair, rows, lanes = where(i)
        _, vcat = keys_values(jj, pair // (Q_PER_KV // 2))
        e, sink_term = probs.pop(i)
        o = jnp.dot(e, vcat, preferred_element_type=F32)
        den = o[:, LANES:] + sink_term
        a_ref[rows, lanes] = (o[:, :LANES] / den * za_ref[rows, lanes].astype(F32)).astype(BF16)

    def slot(it):
        if it < len(items):
            stage_scores(it)
        if 0 <= it - 1 < len(items):
            stage_softmax(it - 1)
        if 0 <= it - 2 < len(items):
            stage_output(it - 2)

    return [functools.partial(slot, it) for it in range(len(items) + 2)]


def _attn_out_kernel(sink_ref, x_ref, p_ref, ssm_ref, zs_ref, q_ref, k_ref, v_ref, kp_ref, vp_ref,
                     za_ref, bias_ref, half_ref, wout_f32, pg_ref, wgate_f32, bgate_ref, wproj_f32,
                     o_ref, wout_ref, wgate_ref, wproj_ref, a0_ref, a1_ref,
                     *, blocks_per_seq, n_blocks):
    step = pl.program_id(0)
    seq_block = jnp.minimum(step, n_blocks - 1) % blocks_per_seq
    first = (seq_block == 0).astype(jnp.int32)

    @pl.when(step == 0)
    def _():
        wout_ref[...] = wout_f32[...].astype(BF16)
        wgate_ref[...] = wgate_f32[...].astype(BF16)
        wproj_ref[...] = wproj_f32[...].astype(BF16)

    def body(a_w, a_r, front, back):
        pieces = []
        if front:
            pieces = _attention_pieces(first, sink_ref, q_ref, k_ref, v_ref, kp_ref, vp_ref,
                                       za_ref, bias_ref, half_ref, a_w)
        n_chunks = D_MODEL // OUT_COLS
        per_chunk = max(len(pieces) - 2, 0) // (2 * n_chunks)
        pieces = iter(pieces)

        def run_pieces(count=per_chunk):
            for piece in (list(pieces) if count is None else [p for _, p in zip(range(count), pieces)]):
                piece()

        if not back:
            run_pieces(None)
            return
        run_pieces(2)

        s5 = (ssm_ref[...].astype(F32) * zs_ref[...].astype(F32)).astype(BF16)
        a_prev = a_r[...]
        mixed = []
        for n in range(n_chunks):
            cols = slice(n * OUT_COLS, (n + 1) * OUT_COLS)
            mixed.append(jnp.dot(s5, wout_ref[:D_SSM, cols], preferred_element_type=F32)
                         + jnp.dot(a_prev, wout_ref[D_SSM:, cols], preferred_element_type=F32))
            run_pieces()
        mixed = jnp.concatenate(mixed, axis=1)
        ms = jnp.mean(mixed * mixed, axis=-1, keepdims=True)
        h1 = x_ref[...] + mixed * lax.rsqrt(ms + EPS) * pg_ref[...]
        h1b = h1.astype(BF16)
        pb = p_ref[...].astype(BF16)
        for n in range(n_chunks):
            cols = slice(n * OUT_COLS, (n + 1) * OUT_COLS)
            gate = jax.nn.sigmoid(
                jnp.dot(h1b, wgate_ref[:, cols], preferred_element_type=F32) + bgate_ref[:, cols])
            ple = jnp.dot(pb, wproj_ref[:, cols], preferred_element_type=F32)
            o_ref[:, cols] = h1[:, cols] + gate * ple
            run_pieces()
        run_pieces(None)

    _pipelined(step, n_blocks, body, a0_ref, a1_ref)


def _attn_bias_table():
    q_idx = np.arange(WINDOW)[:, None]
    s_idx = np.arange(2 * WINDOW)[None, :]
    dist = q_idx + WINDOW - s_idx
    valid = (dist >= 0) & (dist < WINDOW)
    slopes = np.exp2(-8.0 * (np.arange(N_HEADS, dtype=np.float32) + 1.0) / N_HEADS).astype(np.float32)
    bias = -slopes[:, None, None] * dist.astype(np.float32)[None]
    normal = np.where(valid[None], bias, -np.inf)
    first = np.where((valid & (s_idx >= WINDOW))[None], bias, -np.inf)
    paired = lambda t: t.reshape(N_HEADS // 2, 2, WINDOW, 2 * WINDOW).transpose(0, 2, 1, 3).reshape(
        N_HEADS // 2, WINDOW, 4 * WINDOW)
    return np.concatenate([paired(normal), paired(first)], axis=0).astype(np.float32)


def _attn_out(sinks, x, p, ssm, zs, q, k, v, za, wout, pg, wgate, bgate, wproj):
    bsz, seq, _ = x.shape
    sub = TOK_OUT // WINDOW
    per_seq = seq // TOK_OUT
    n_blocks = bsz * per_seq

    def cur(f):
        f = jnp.minimum(f, n_blocks - 1)
        return f // per_seq, f % per_seq

    def prv(f):
        f = jnp.maximum(f - 1, 0)
        return f // per_seq, f % per_seq

    front = lambda width: pl.BlockSpec((None, TOK_OUT, width), lambda f: (*cur(f), 0))
    back = lambda width: pl.BlockSpec((None, TOK_OUT, width), lambda f: (*prv(f), 0))
    prev_kv = pl.BlockSpec(
        (None, WINDOW, D_KV),
        lambda f: (cur(f)[0], jnp.maximum(cur(f)[1] * sub - 1, 0), 0))
    full = lambda shape: pl.BlockSpec(shape, lambda f: (0,) * len(shape))
    once = lambda shape: pl.BlockSpec(shape, lambda f: (0,) * len(shape),
                                      pipeline_mode=pl.Buffered(1))
    bias = jnp.asarray(_attn_bias_table())
    lane_is_even = np.broadcast_to(np.arange(LANES) < HEAD_DIM, (2 * WINDOW, LANES))
    half = jnp.asarray(np.stack([lane_is_even, ~lane_is_even]), BF16)
    a_scratch = pltpu.VMEM((TOK_OUT, D_ATTN), BF16)
    return pl.pallas_call(
        functools.partial(_attn_out_kernel, blocks_per_seq=per_seq, n_blocks=n_blocks),
        grid=(n_blocks + 1,),
        in_specs=[
            pl.BlockSpec(memory_space=pltpu.SMEM),
            back(D_MODEL), back(D_PLE), back(D_SSM), back(D_SSM), front(D_ATTN),
            front(D_KV), front(D_KV), prev_kv, prev_kv, front(D_ATTN),
            once(bias.shape), once(half.shape),
            once((D_MODEL, D_MODEL)), full((1, D_MODEL)),
            once((D_MODEL, D_MODEL)), full((1, D_MODEL)), once((D_PLE, D_MODEL)),
        ],
        out_specs=back(D_MODEL),
        out_shape=jax.ShapeDtypeStruct((bsz, seq, D_MODEL), F32),
        scratch_shapes=[pltpu.VMEM((D_MODEL, D_MODEL), BF16), pltpu.VMEM((D_MODEL, D_MODEL), BF16),
                        pltpu.VMEM((D_PLE, D_MODEL), BF16), a_scratch, a_scratch],
        compiler_params=pltpu.CompilerParams(
            dimension_semantics=("arbitrary",), vmem_limit_bytes=VMEM_LIMIT),
        name="attn_out",
    )(sinks, x, p, ssm, zs, q, k, v, k, v, za, bias, half, wout, pg, wgate, bgate, wproj)


def _ssm_params(lam_re, lam_im, log_step, b_re, b_im, c_re, c_im):
    lr, li = lam_re.astype(F32), lam_im.astype(F32)
    step = jnp.exp(log_step.astype(F32))[:, None]
    mag = jnp.exp(lr * step)
    lbr, lbi = mag * jnp.cos(li * step), mag * jnp.sin(li * step)
    den = lr * lr + li * li
    fr = ((lbr - 1.0) * lr + lbi * li) / den
    fi = (lbi * lr - (lbr - 1.0) * li) / den
    br, bi = b_re.astype(F32), b_im.astype(F32)
    bbr = fr[..., None] * br - fi[..., None] * bi
    bbi = fr[..., None] * bi + fi[..., None] * br
    cr, ci = c_re.astype(F32), c_im.astype(F32)
    l2 = jnp.stack([lbr * lbr - lbi * lbi, 2.0 * lbr * lbi])
    lbbr = lbr[..., None] * bbr - lbi[..., None] * bbi
    lbbi = lbr[..., None] * bbi + lbi[..., None] * bbr
    clr = cr * lbr[:, None, :] - ci * lbi[:, None, :]
    cli = cr * lbi[:, None, :] + ci * lbr[:, None, :]
    cb = jnp.sum(cr[:, :, :, None] * bbr[:, None, :, :] - ci[:, :, :, None] * bbi[:, None, :, :],
                 axis=2)
    gps = SLAB_CH // SSM_GROUP_CH
    n, p = SSM_STATE, SSM_GROUP_CH
    same_group = (np.arange(SLAB_CH)[:, None] // p == np.arange(SLAB_STATE)[None, :] // n)
    expand_p = np.tile(np.eye(p, dtype=np.float32), (gps, 1))
    expand_n = np.tile(np.eye(n, dtype=np.float32), (gps, 1))

    w_in = jnp.stack([jnp.stack([lbbr, bbr]), jnp.stack([lbbi, bbi])])
    w_in = w_in.reshape(2, 2, N_SLABS, SLAB_STATE, p)
    w_in = jnp.einsum('rp,absqp->asbrq', expand_p, w_in) * same_group.astype(np.float32)
    w_in = w_in.astype(BF16).reshape(2, N_SLABS, 2 * SLAB_CH, SLAB_STATE)
    w_out = jnp.stack([jnp.stack([cr, clr]), jnp.stack([-ci, -cli])])
    w_out = w_out.reshape(2, 2, N_SLABS, SLAB_CH, n)
    w_out = jnp.einsum('rn,aesqn->aesrq', expand_n, w_out) * same_group.T.astype(np.float32)
    w_out = jnp.concatenate([w_out[:, 0], w_out[:, 1]], axis=-1).astype(BF16)
    same_group_pp = same_group[:, ::n // p]
    w_dir = jnp.einsum('rq,sxq->srx', expand_p, cb.reshape(N_SLABS, SLAB_CH, p))
    w_dir = (w_dir * same_group_pp.astype(np.float32)).astype(BF16)
    return w_in, l2.reshape(2, 1, N_STATE), w_out, w_dir


def _layer(h, p, pre_norm_g, w_in, ssm_lam_re, ssm_lam_im, ssm_log_step, ssm_b_re, ssm_b_im,
           ssm_c_re, ssm_c_im, ssm_d, ssm_w_glu, ssm_b_glu, attn_sinks, w_out, post_norm_g,
           pl_w_proj, pl_w_gate, pl_b_gate):
    row = lambda a: a.astype(F32).reshape(1, -1)
    u, zs, q, k, v, za = _in_proj(h, row(pre_norm_g), w_in.astype(F32))
    params = _ssm_params(
        ssm_lam_re, ssm_lam_im, ssm_log_step, ssm_b_re, ssm_b_im, ssm_c_re, ssm_c_im)
    ssm = _ssm(u, params, row(ssm_d), ssm_w_glu.astype(F32), row(ssm_b_glu))
    return _attn_out(attn_sinks.astype(F32), h, p, ssm, zs, q, k, v, za,
                     w_out.astype(F32), row(post_norm_g), pl_w_gate.astype(F32), row(pl_b_gate),
                     pl_w_proj.astype(F32))


def kernel(x, p, pre_norm_g, w_in, ssm_lam_re, ssm_lam_im, ssm_log_step, ssm_b_re, ssm_b_im,
           ssm_c_re, ssm_c_im, ssm_d, ssm_w_glu, ssm_b_glu, attn_sinks, w_out, post_norm_g,
           pl_w_proj, pl_w_gate, pl_b_gate):
    h = x
    for i in range(p.shape[0]):
        h = _layer(h, p[i], pre_norm_g[i], w_in[i], ssm_lam_re[i], ssm_lam_im[i], ssm_log_step[i],
                   ssm_b_re[i], ssm_b_im[i], ssm_c_re[i], ssm_c_im[i], ssm_d[i], ssm_w_glu[i],
                   ssm_b_glu[i], attn_sinks[i], w_out[i], post_norm_g[i], pl_w_proj[i],
                   pl_w_gate[i], pl_b_gate[i])
    return h
```

```python
import functools
import math

import numpy as np
import jax
import jax.numpy as jnp
from jax import lax
from jax.experimental import pallas as pl
from jax.experimental.pallas import tpu as pltpu

F32 = jnp.float32
BF16 = jnp.bfloat16

D_MODEL = 1024
D_SSM = 512
D_ATTN = 512
SSM_GROUP_CH = 16
SSM_GROUPS = D_SSM // SSM_GROUP_CH
SSM_STATE = 64
N_STATE = SSM_GROUPS * SSM_STATE
HEAD_DIM = 64
N_HEADS = D_ATTN // HEAD_DIM
KV_HEADS = 2
Q_PER_KV = N_HEADS // KV_HEADS
D_KV = KV_HEADS * HEAD_DIM
WINDOW = 128
D_PLE = 256
EPS = 1e-6

LANES = 128
assert D_KV == LANES and 2 * HEAD_DIM == LANES

OFF_U, OFF_ZS, OFF_Q, OFF_K, OFF_V, OFF_ZA, D_IN = 0, 512, 1024, 1536, 1664, 1792, 2304

SLAB_CH = LANES
N_SLABS = D_SSM // SLAB_CH
SLAB_STATE = (SLAB_CH // SSM_GROUP_CH) * SSM_STATE
SCAN_COLS = 256

TOK_IN = 2048
NORM_ROWS = 256
T_SCAN = 64
T_SUB = 32
PIECE_ROWS = 128
PERM_BATCH = 8
TOK_OUT = 1024
PROJ_COLS = 256
OUT_COLS = 256
VMEM_LIMIT = 60 * 1024 * 1024


def _silu(z):
    return z * jax.nn.sigmoid(z)


def _pipelined(step, n_blocks, body, even_bufs, odd_bufs):
    @pl.when(step == 0)
    def _():
        body(even_bufs, odd_bufs, True, False)

    @pl.when((step > 0) & (step < n_blocks) & (step % 2 == 0))
    def _():
        body(even_bufs, odd_bufs, True, True)

    @pl.when((step < n_blocks) & (step % 2 == 1))
    def _():
        body(odd_bufs, even_bufs, True, True)

    @pl.when(step == n_blocks)
    def _():
        last_read = odd_bufs if n_blocks % 2 == 0 else even_bufs
        body(None, last_read, False, True)


def _in_proj_kernel(x_ref, g_ref, w_ref, u_ref, zs_ref, q_ref, k_ref, v_ref, za_ref, w_scr):
    @pl.when((pl.program_id(0) == 0) & (pl.program_id(1) == 0))
    def _():
        w_scr[...] = w_ref[...].astype(BF16)

    def emit(rows, tile, acc):
        lo = tile * PROJ_COLS
        if lo < OFF_ZS:
            u_ref[rows, lo - OFF_U:lo - OFF_U + PROJ_COLS] = acc.astype(BF16)
        elif lo < OFF_Q:
            zs_ref[rows, lo - OFF_ZS:lo - OFF_ZS + PROJ_COLS] = _silu(acc).astype(BF16)
        elif lo < OFF_K:
            q_ref[rows, lo - OFF_Q:lo - OFF_Q + PROJ_COLS] = (
                acc * (1.0 / math.sqrt(HEAD_DIM))).astype(BF16)
        elif lo < OFF_ZA:
            k_ref[rows, :] = acc[:, :D_KV].astype(BF16)
            v_ref[rows, :] = acc[:, D_KV:].astype(BF16)
        else:
            za_ref[rows, lo - OFF_ZA:lo - OFF_ZA + PROJ_COLS] = _silu(acc).astype(BF16)

    gain = g_ref[...]
    for piece in range(TOK_IN // NORM_ROWS):
        rows = slice(piece * NORM_ROWS, (piece + 1) * NORM_ROWS)
        x = x_ref[rows, :]
        ms = jnp.mean(x * x, axis=-1, keepdims=True)
        hn = (x * lax.rsqrt(ms + EPS) * gain).astype(BF16)
        for tile in range(D_IN // PROJ_COLS):
            cols = slice(tile * PROJ_COLS, (tile + 1) * PROJ_COLS)
            emit(rows, tile, jnp.dot(hn, w_scr[:, cols], preferred_element_type=F32))


def _in_proj(x, g, w_in):
    bsz, seq, _ = x.shape
    tok = lambda width: pl.BlockSpec((None, TOK_IN, width), lambda b, i: (b, i, 0))
    widths = (D_SSM, D_SSM, D_ATTN, D_KV, D_KV, D_ATTN)
    return pl.pallas_call(
        _in_proj_kernel,
        grid=(bsz, seq // TOK_IN),
        in_specs=[
            tok(D_MODEL),
            pl.BlockSpec((1, D_MODEL), lambda b, i: (0, 0)),
            pl.BlockSpec((D_MODEL, D_IN), lambda b, i: (0, 0), pipeline_mode=pl.Buffered(1)),
        ],
        out_specs=tuple(tok(w) for w in widths),
        out_shape=tuple(jax.ShapeDtypeStruct((bsz, seq, w), BF16) for w in widths),
        scratch_shapes=[pltpu.VMEM((D_MODEL, D_IN), BF16)],
        compiler_params=pltpu.CompilerParams(
            dimension_semantics=("arbitrary", "arbitrary"), vmem_limit_bytes=VMEM_LIMIT),
        name="in_proj",
    )(x, g, w_in)


def _ssm_kernel(u_ref, perm_ref, permt_ref, win_ref, l2_ref, wout_ref, wdir_ref, d_ref,
                wglu_f32, bglu_ref, o_ref,
                wglu_ref, sre_ref, sim_ref, st_ref, y_ref,
                g0_ref, ut0_ref, g1_ref, ut1_ref, *, bsz, n_blocks):
    subs = T_SCAN // T_SUB
    rows = bsz * T_SUB
    half = rows // 2
    step = pl.program_id(0)

    @pl.when(step == 0)
    def _():
        wglu_ref[...] = wglu_f32[...].astype(BF16)
        sre_ref[...] = jnp.zeros_like(sre_ref)
        sim_ref[...] = jnp.zeros_like(sim_ref)

    def body(write, read, front, back):
        g_w, ut_w = write if front else (None, None)
        g_r, ut_r = read if back else (None, None)
        groups = bsz // PERM_BATCH
        grows = PERM_BATCH * T_SUB
        u_new, u_before = [], []
        for sb in range(subs if front else 0):
            steps = slice(sb * T_SUB, (sb + 1) * T_SUB)
            permuted = [
                jnp.dot(perm_ref[...],
                        u_ref[g * PERM_BATCH:(g + 1) * PERM_BATCH, steps].reshape(grows, D_SSM),
                        preferred_element_type=F32).reshape(T_SUB, 1, PERM_BATCH, D_SSM)
                for g in range(groups)]
            u_new.append(jnp.concatenate(permuted, axis=1).reshape(rows, D_SSM).astype(BF16))
            ut_w[sb] = u_new[sb]
            if sb > 0:
                u_last = u_new[sb - 1][rows - bsz:]
            else:
                u_last = ut_r[subs - 1, rows - bsz:, :] if back else jnp.zeros((bsz, D_SSM), BF16)
            u_before.append(jnp.concatenate([u_last, u_new[sb][half:rows - bsz]], axis=0))
        for j in range(N_SLABS):
            ch = slice(j * SLAB_CH, (j + 1) * SLAB_CH)
            drive_in = [jnp.concatenate([u_before[sb][:, ch], u_new[sb][:half, ch]], axis=1)
                        for sb in range(subs if front else 0)]
            for c in range(SLAB_STATE // SCAN_COLS):
                lo = j * SLAB_STATE + c * SCAN_COLS
                cre = slice(c * SCAN_COLS, (c + 1) * SCAN_COLS)
                cim = slice(SLAB_STATE + c * SCAN_COLS, SLAB_STATE + (c + 1) * SCAN_COLS)
                if back:
                    l2r = jnp.broadcast_to(l2_ref[0, :, lo:lo + SCAN_COLS], (bsz, SCAN_COLS))
                    l2i = jnp.broadcast_to(l2_ref[1, :, lo:lo + SCAN_COLS], (bsz, SCAN_COLS))
                    sr = sre_ref[:, lo:lo + SCAN_COLS]
                    si = sim_ref[:, lo:lo + SCAN_COLS]
                for sb, piece in [(sb, piece) for sb in range(subs)
                                  for piece in range(half // PIECE_ROWS)]:
                    rk = slice(piece * PIECE_ROWS, (piece + 1) * PIECE_ROWS)
                    if front:
                        g_w[sb, j, rk, cre] = jnp.dot(drive_in[sb][rk], win_ref[0, j, :, cre],
                                                      preferred_element_type=F32)
                        g_w[sb, j, rk, cim] = jnp.dot(drive_in[sb][rk], win_ref[1, j, :, cre],
                                                      preferred_element_type=F32)
                    if not back:
                        continue
                    for k in range(rk.start // bsz, rk.stop // bsz):
                        r = slice(k * bsz, (k + 1) * bsz)
                        sr, si = (l2r * sr - l2i * si + g_r[sb, j, r, cre],
                                  l2r * si + l2i * sr + g_r[sb, j, r, cim])
                        st_ref[sb, j, r, cre] = sr.astype(BF16)
                        st_ref[sb, j, r, cim] = si.astype(BF16)
                    yk = (jnp.dot(st_ref[sb, j, rk, cre], wout_ref[0, j, cre, :],
                                  preferred_element_type=F32)
                          + jnp.dot(st_ref[sb, j, rk, cim], wout_ref[1, j, cre, :],
                                    preferred_element_type=F32))
                    odd_rows = slice(half + rk.start, half + rk.stop)
                    if c == 0:
                        y_ref[sb, rk, ch] = yk[:, :SLAB_CH]
                        y_ref[sb, odd_rows, ch] = yk[:, SLAB_CH:] + jnp.dot(
                            ut_r[sb, odd_rows, ch], wdir_ref[j], preferred_element_type=F32)
                    else:
                        y_ref[sb, rk, ch] += yk[:, :SLAB_CH]
                        y_ref[sb, odd_rows, ch] += yk[:, SLAB_CH:]
                if back:
                    sre_ref[:, lo:lo + SCAN_COLS] = sr
                    sim_ref[:, lo:lo + SCAN_COLS] = si
        if not back:
            return

        for sb in range(subs):
            y = y_ref[sb] + d_ref[...] * ut_r[sb].astype(F32)
            g = jax.nn.gelu(y)
            gate = jnp.dot(g.astype(BF16), wglu_ref[...], preferred_element_type=F32) + bglu_ref[...]
            out_tb = (g * jax.nn.sigmoid(gate)).reshape(T_SUB, groups, PERM_BATCH, D_SSM)
            for grp in range(groups):
                out_g = out_tb[:, grp].reshape(grows, D_SSM).astype(BF16)
                out_bt = jnp.dot(permt_ref[...], out_g, preferred_element_type=F32).astype(BF16)
                o_ref[grp * PERM_BATCH:(grp + 1) * PERM_BATCH, sb * T_SUB:(sb + 1) * T_SUB] = (
                    out_bt.reshape(PERM_BATCH, T_SUB, D_SSM))

    _pipelined(step, n_blocks, body, (g0_ref, ut0_ref), (g1_ref, ut1_ref))


def _row_permutation():
    rows = PERM_BATCH * T_SUB
    perm = np.zeros((rows, rows), np.float32)
    t, b = np.meshgrid(np.arange(T_SUB), np.arange(PERM_BATCH), indexing="ij")
    dst = (t % 2) * (rows // 2) + (t // 2) * PERM_BATCH + b
    perm[dst.ravel(), (b * T_SUB + t).ravel()] = 1.0
    return perm


def _ssm(u, params, d, wglu, bglu):
    bsz, seq, _ = u.shape
    subs = T_SCAN // T_SUB
    rows = T_SUB * bsz
    half = rows // 2
    nblk = seq // T_SCAN
    assert bsz % PERM_BATCH == 0
    perm = _row_permutation()
    full = lambda shape: pl.BlockSpec(shape, lambda f: (0,) * len(shape))
    g_scratch = pltpu.VMEM((subs, N_SLABS, half, 2 * SLAB_STATE), F32)
    ut_scratch = pltpu.VMEM((subs, rows, D_SSM), BF16)
    return pl.pallas_call(
        functools.partial(_ssm_kernel, bsz=bsz, n_blocks=nblk),
        grid=(nblk + 1,),
        in_specs=[
            pl.BlockSpec((bsz, T_SCAN, D_SSM), lambda f: (0, jnp.minimum(f, nblk - 1), 0)),
            full(perm.shape), full(perm.shape),
            full((2, N_SLABS, 2 * SLAB_CH, SLAB_STATE)), full((2, 1, N_STATE)),
            full((2, N_SLABS, SLAB_STATE, 2 * SLAB_CH)), full((N_SLABS, SLAB_CH, SLAB_CH)),
            full((1, D_SSM)),
            pl.BlockSpec((D_SSM, D_SSM), lambda f: (0, 0), pipeline_mode=pl.Buffered(1)),
            full((1, D_SSM)),
        ],
        out_specs=pl.BlockSpec((bsz, T_SCAN, D_SSM), lambda f: (0, jnp.maximum(f - 1, 0), 0)),
        out_shape=jax.ShapeDtypeStruct((bsz, seq, D_SSM), BF16),
        scratch_shapes=[
            pltpu.VMEM((D_SSM, D_SSM), BF16),
            pltpu.VMEM((bsz, N_STATE), F32),
            pltpu.VMEM((bsz, N_STATE), F32),
            pltpu.VMEM((subs, N_SLABS, half, 2 * SLAB_STATE), BF16),
            pltpu.VMEM((subs, rows, D_SSM), F32),
            g_scratch, ut_scratch, g_scratch, ut_scratch,
        ],
        compiler_params=pltpu.CompilerParams(
            dimension_semantics=("arbitrary",), vmem_limit_bytes=VMEM_LIMIT),
        name="s5_scan",
    )(u, jnp.asarray(perm, BF16), jnp.asarray(perm.T, BF16), *params, d, wglu, bglu)


def _attention_pieces(first, sink_ref, q_ref, k_ref, v_ref, kp_ref, vp_ref, za_ref, bias_ref,
                      half_ref, a_ref):
    low_q = lax.broadcasted_iota(jnp.int32, (WINDOW, LANES), 1) < HEAD_DIM
    kv_cache = {}

    def swap_halves(t):
        packed = pltpu.bitcast(t, jnp.uint32)
        return pltpu.bitcast(pltpu.roll(packed, HEAD_DIM, axis=1), BF16)

    def keys_values(jj, g):
        if jj not in kv_cache:
            rows = slice(jj * WINDOW, (jj + 1) * WINDOW)
            if jj == 0:
                kp, vp = kp_ref[...], vp_ref[...]
            else:
                prev = slice((jj - 1) * WINDOW, jj * WINDOW)
                kp, vp = k_ref[prev, :], v_ref[prev, :]
            kk = jnp.concatenate([kp, k_ref[rows, :]], axis=0)
            vv = jnp.concatenate([vp, v_ref[rows, :]], axis=0)
            kv_cache[jj] = ((kk, swap_halves(kk)), (vv, swap_halves(vv)))
        if (jj, g) not in kv_cache:
            keep_even, keep_odd = half_ref[0], half_ref[1]
            k2, v2 = kv_cache[jj]
            kcat = jnp.concatenate([k2[g] * keep_even, k2[1 - g] * keep_odd], axis=0)
            vcat = jnp.concatenate(
                [jnp.concatenate([v2[g] * keep_even, keep_even], axis=1),
                 jnp.concatenate([v2[1 - g] * keep_odd, keep_odd], axis=1)], axis=0)
            kv_cache[(jj, g)] = (kcat, vcat)
        return kv_cache[(jj, g)]

    items = [(jj, pair) for jj in range(TOK_OUT // WINDOW) for pair in range(N_HEADS // 2)]
    scores, probs = {}, {}

    def where(i):
        jj, pair = items[i]
        return (jj, pair, slice(jj * WINDOW, (jj + 1) * WINDOW),
                slice(pair * LANES, (pair + 1) * LANES))

    def stage_scores(i):
        jj, pair, rows, lanes = where(i)
        kcat, _ = keys_values(jj, pair // (Q_PER_KV // 2))
        s = lax.dot_general(q_ref[rows, lanes], kcat, (((1,), (1,)), ((), ())),
                            preferred_element_type=F32)
        scores[i] = s + (bias_ref[pair + (N_HEADS // 2) * first] if jj == 0 else bias_ref[pair])

    def stage_softmax(i):
        _, pair, _, _ = where(i)
        s = scores.pop(i)
        sink_e, sink_o = sink_ref[2 * pair], sink_ref[2 * pair + 1]
        m_e = jnp.maximum(jnp.max(s[:, :2 * WINDOW], axis=-1, keepdims=True), sink_e)
        m_o = jnp.maximum(jnp.max(s[:, 2 * WINDOW:], axis=-1, keepdims=True), sink_o)
        e = jnp.concatenate([jnp.exp(s[:, :2 * WINDOW] - m_e),
                             jnp.exp(s[:, 2 * WINDOW:] - m_o)], axis=1).astype(BF16)
        probs[i] = (e, jnp.where(low_q, jnp.exp(sink_e - m_e), jnp.exp(sink_o - m_o)))

    def stage_output(i):
        jj, pair, rows, lanes = where(i)
        _, vcat = keys_values(jj, pair // (Q_PER_KV // 2))
        e, sink_term = probs.pop(i)
        o = jnp.dot(e, vcat, preferred_element_type=F32)
        den = o[:, LANES:] + sink_term
        a_ref[rows, lanes] = (o[:, :LANES] / den * za_ref[rows, lanes].astype(F32)).astype(BF16)

    def slot(it):
        if it < len(items):
            stage_scores(it)
        if 0 <= it - 1 < len(items):
            stage_softmax(it - 1)
        if 0 <= it - 2 < len(items):
            stage_output(it - 2)

    return [functools.partial(slot, it) for it in range(len(items) + 2)]


def _attn_out_kernel(sink_ref, x_ref, p_ref, ssm_ref, zs_ref, q_ref, k_ref, v_ref, kp_ref, vp_ref,
                     za_ref, bias_ref, half_ref, wout_f32, pg_ref, wgate_f32, bgate_ref, wproj_f32,
                     o_ref, wout_ref, wgate_ref, wproj_ref, a0_ref, a1_ref,
                     *, blocks_per_seq, n_blocks):
    step = pl.program_id(0)
    seq_block = jnp.minimum(step, n_blocks - 1) % blocks_per_seq
    first = (seq_block == 0).astype(jnp.int32)

    @pl.when(step == 0)
    def _():
        wout_ref[...] = wout_f32[...].astype(BF16)
        wgate_ref[...] = wgate_f32[...].astype(BF16)
        wproj_ref[...] = wproj_f32[...].astype(BF16)

    def body(a_w, a_r, front, back):
        pieces = []
        if front:
            pieces = _attention_pieces(first, sink_ref, q_ref, k_ref, v_ref, kp_ref, vp_ref,
                                       za_ref, bias_ref, half_ref, a_w)
        n_chunks = D_MODEL // OUT_COLS
        per_chunk = max(len(pieces) - 2, 0) // (2 * n_chunks)
        pieces = iter(pieces)

        def run_pieces(count=per_chunk):
            for piece in (list(pieces) if count is None else [p for _, p in zip(range(count), pieces)]):
                piece()

        if not back:
            run_pieces(None)
            return
        run_pieces(2)

        s5 = (ssm_ref[...].astype(F32) * zs_ref[...].astype(F32)).astype(BF16)
        a_prev = a_r[...]
        mixed = []
        for n in range(n_chunks):
            cols = slice(n * OUT_COLS, (n + 1) * OUT_COLS)
            mixed.append(jnp.dot(s5, wout_ref[:D_SSM, cols], preferred_element_type=F32)
                         + jnp.dot(a_prev, wout_ref[D_SSM:, cols], preferred_element_type=F32))
            run_pieces()
        mixed = jnp.concatenate(mixed, axis=1)
        ms = jnp.mean(mixed * mixed, axis=-1, keepdims=True)
        h1 = x_ref[...] + mixed * lax.rsqrt(ms + EPS) * pg_ref[...]
        h1b = h1.astype(BF16)
        pb = p_ref[...].astype(BF16)
        for n in range(n_chunks):
            cols = slice(n * OUT_COLS, (n + 1) * OUT_COLS)
            gate = jax.nn.sigmoid(
                jnp.dot(h1b, wgate_ref[:, cols], preferred_element_type=F32) + bgate_ref[:, cols])
            ple = jnp.dot(pb, wproj_ref[:, cols], preferred_element_type=F32)
            o_ref[:, cols] = h1[:, cols] + gate * ple
            run_pieces()
        run_pieces(None)

    _pipelined(step, n_blocks, body, a0_ref, a1_ref)


def _attn_bias_table():
    q_idx = np.arange(WINDOW)[:, None]
    s_idx = np.arange(2 * WINDOW)[None, :]
    dist = q_idx + WINDOW - s_idx
    valid = (dist >= 0) & (dist < WINDOW)
    slopes = np.exp2(-8.0 * (np.arange(N_HEADS, dtype=np.float32) + 1.0) / N_HEADS).astype(np.float32)
    bias = -slopes[:, None, None] * dist.astype(np.float32)[None]
    normal = np.where(valid[None], bias, -np.inf)
    first = np.where((valid & (s_idx >= WINDOW))[None], bias, -np.inf)
    paired = lambda t: t.reshape(N_HEADS // 2, 2, WINDOW, 2 * WINDOW).transpose(0, 2, 1, 3).reshape(
        N_HEADS // 2, WINDOW, 4 * WINDOW)
    return np.concatenate([paired(normal), paired(first)], axis=0).astype(np.float32)


def _attn_out(sinks, x, p, ssm, zs, q, k, v, za, wout, pg, wgate, bgate, wproj):
    bsz, seq, _ = x.shape
    sub = TOK_OUT // WINDOW
    per_seq = seq // TOK_OUT
    n_blocks = bsz * per_seq

    def cur(f):
        f = jnp.minimum(f, n_blocks - 1)
        return f // per_seq, f % per_seq

    def prv(f):
        f = jnp.maximum(f - 1, 0)
        return f // per_seq, f % per_seq

    front = lambda width: pl.BlockSpec((None, TOK_OUT, width), lambda f: (*cur(f), 0))
    back = lambda width: pl.BlockSpec((None, TOK_OUT, width), lambda f: (*prv(f), 0))
    prev_kv = pl.BlockSpec(
        (None, WINDOW, D_KV),
        lambda f: (cur(f)[0], jnp.maximum(cur(f)[1] * sub - 1, 0), 0))
    full = lambda shape: pl.BlockSpec(shape, lambda f: (0,) * len(shape))
    once = lambda shape: pl.BlockSpec(shape, lambda f: (0,) * len(shape),
                                      pipeline_mode=pl.Buffered(1))
    bias = jnp.asarray(_attn_bias_table())
    lane_is_even = np.broadcast_to(np.arange(LANES) < HEAD_DIM, (2 * WINDOW, LANES))
    half = jnp.asarray(np.stack([lane_is_even, ~lane_is_even]), BF16)
    a_scratch = pltpu.VMEM((TOK_OUT, D_ATTN), BF16)
    return pl.pallas_call(
        functools.partial(_attn_out_kernel, blocks_per_seq=per_seq, n_blocks=n_blocks),
        grid=(n_blocks + 1,),
        in_specs=[
            pl.BlockSpec(memory_space=pltpu.SMEM),
            back(D_MODEL), back(D_PLE), back(D_SSM), back(D_SSM), front(D_ATTN),
            front(D_KV), front(D_KV), prev_kv, prev_kv, front(D_ATTN),
            once(bias.shape), once(half.shape),
            once((D_MODEL, D_MODEL)), full((1, D_MODEL)),
            once((D_MODEL, D_MODEL)), full((1, D_MODEL)), once((D_PLE, D_MODEL)),
        ],
        out_specs=back(D_MODEL),
        out_shape=jax.ShapeDtypeStruct((bsz, seq, D_MODEL), F32),
        scratch_shapes=[pltpu.VMEM((D_MODEL, D_MODEL), BF16), pltpu.VMEM((D_MODEL, D_MODEL), BF16),
                        pltpu.VMEM((D_PLE, D_MODEL), BF16), a_scratch, a_scratch],
        compiler_params=pltpu.CompilerParams(
            dimension_semantics=("arbitrary",), vmem_limit_bytes=VMEM_LIMIT),
        name="attn_out",
    )(sinks, x, p, ssm, zs, q, k, v, k, v, za, bias, half, wout, pg, wgate, bgate, wproj)


def _ssm_params(lam_re, lam_im, log_step, b_re, b_im, c_re, c_im):
    lr, li = lam_re.astype(F32), lam_im.astype(F32)
    step = jnp.exp(log_step.astype(F32))[:, None]
    mag = jnp.exp(lr * step)
    lbr, lbi = mag * jnp.cos(li * step), mag * jnp.sin(li * step)
    den = lr * lr + li * li
    fr = ((lbr - 1.0) * lr + lbi * li) / den
    fi = (lbi * lr - (lbr - 1.0) * li) / den
    br, bi = b_re.astype(F32), b_im.astype(F32)
    bbr = fr[..., None] * br - fi[..., None] * bi
    bbi = fr[..., None] * bi + fi[..., None] * br
    cr, ci = c_re.astype(F32), c_im.astype(F32)
    l2 = jnp.stack([lbr * lbr - lbi * lbi, 2.0 * lbr * lbi])
    lbbr = lbr[..., None] * bbr - lbi[..., None] * bbi
    lbbi = lbr[..., None] * bbi + lbi[..., None] * bbr
    clr = cr * lbr[:, None, :] - ci * lbi[:, None, :]
    cli = cr * lbi[:, None, :] + ci * lbr[:, None, :]
    cb = jnp.sum(cr[:, :, :, None] * bbr[:, None, :, :] - ci[:, :, :, None] * bbi[:, None, :, :],
                 axis=2)
    gps = SLAB_CH // SSM_GROUP_CH
    n, p = SSM_STATE, SSM_GROUP_CH
    same_group = (np.arange(SLAB_CH)[:, None] // p == np.arange(SLAB_STATE)[None, :] // n)
    expand_p = np.tile(np.eye(p, dtype=np.float32), (gps, 1))
    expand_n = np.tile(np.eye(n, dtype=np.float32), (gps, 1))

    w_in = jnp.stack([jnp.stack([lbbr, bbr]), jnp.stack([lbbi, bbi])])
    w_in = w_in.reshape(2, 2, N_SLABS, SLAB_STATE, p)
    w_in = jnp.einsum('rp,absqp->asbrq', expand_p, w_in) * same_group.astype(np.float32)
    w_in = w_in.astype(BF16).reshape(2, N_SLABS, 2 * SLAB_CH, SLAB_STATE)
    w_out = jnp.stack([jnp.stack([cr, clr]), jnp.stack([-ci, -cli])])
    w_out = w_out.reshape(2, 2, N_SLABS, SLAB_CH, n)
    w_out = jnp.einsum('rn,aesqn->aesrq', expand_n, w_out) * same_group.T.astype(np.float32)
    w_out = jnp.concatenate([w_out[:, 0], w_out[:, 1]], axis=-1).astype(BF16)
    same_group_pp = same_group[:, ::n // p]
    w_dir = jnp.einsum('rq,sxq->srx', expand_p, cb.reshape(N_SLABS, SLAB_CH, p))
    w_dir = (w_dir * same_group_pp.astype(np.float32)).astype(BF16)
    return w_in, l2.reshape(2, 1, N_STATE), w_out, w_dir


def _layer(h, p, pre_norm_g, w_in, ssm_lam_re, ssm_lam_im, ssm_log_step, ssm_b_re, ssm_b_im,
           ssm_c_re, ssm_c_im, ssm_d, ssm_w_glu, ssm_b_glu, attn_sinks, w_out, post_norm_g,
           pl_w_proj, pl_w_gate, pl_b_gate):
    row = lambda a: a.astype(F32).reshape(1, -1)
    u, zs, q, k, v, za = _in_proj(h, row(pre_norm_g), w_in.astype(F32))
    params = _ssm_params(
        ssm_lam_re, ssm_lam_im, ssm_log_step, ssm_b_re, ssm_b_im, ssm_c_re, ssm_c_im)
    ssm = _ssm(u, params, row(ssm_d), ssm_w_glu.astype(F32), row(ssm_b_glu))
    return _attn_out(attn_sinks.astype(F32), h, p, ssm, zs, q, k, v, za,
                     w_out.astype(F32), row(post_norm_g), pl_w_gate.astype(F32), row(pl_b_gate),
                     pl_w_proj.astype(F32))


def kernel(x, p, pre_norm_g, w_in, ssm_lam_re, ssm_lam_im, ssm_log_step, ssm_b_re, ssm_b_im,
           ssm_c_re, ssm_c_im, ssm_d, ssm_w_glu, ssm_b_glu, attn_sinks, w_out, post_norm_g,
           pl_w_proj, pl_w_gate, pl_b_gate):
    h = x
    for i in range(p.shape[0]):
        h = _layer(h, p[i], pre_norm_g[i], w_in[i], ssm_lam_re[i], ssm_lam_im[i], ssm_log_step[i],
                   ssm_b_re[i], ssm_b_im[i], ssm_c_re[i], ssm_c_im[i], ssm_d[i], ssm_w_glu[i],
                   ssm_b_glu[i], attn_sinks[i], w_out[i], post_norm_g[i], pl_w_proj[i],
                   pl_w_gate[i], pl_b_gate[i])
    return h
```

```python
import functools
import math

import numpy as np
import jax
import jax.numpy as jnp
from jax import lax
from jax.experimental import pallas as pl
from jax.experimental.pallas import tpu as pltpu

F32 = jnp.float32
BF16 = jnp.bfloat16

D_MODEL = 1024
D_SSM = 512
D_ATTN = 512
SSM_GROUP_CH = 16
SSM_GROUPS = D_SSM // SSM_GROUP_CH
SSM_STATE = 64
N_STATE = SSM_GROUPS * SSM_STATE
HEAD_DIM = 64
N_HEADS = D_ATTN // HEAD_DIM
KV_HEADS = 2
Q_PER_KV = N_HEADS // KV_HEADS
D_KV = KV_HEADS * HEAD_DIM
WINDOW = 128
D_PLE = 256
EPS = 1e-6

LANES = 128
assert D_KV == LANES and 2 * HEAD_DIM == LANES

OFF_U, OFF_ZS, OFF_Q, OFF_K, OFF_V, OFF_ZA, D_IN = 0, 512, 1024, 1536, 1664, 1792, 2304

SLAB_CH = LANES
N_SLABS = D_SSM // SLAB_CH
SLAB_STATE = (SLAB_CH // SSM_GROUP_CH) * SSM_STATE
SCAN_COLS = 256

TOK_IN = 1024
NORM_ROWS = 256
T_SCAN = 64
T_SUB = 32
PIECE_ROWS = 128
PERM_BATCH = 8
TOK_OUT = 1024
PROJ_COLS = 256
OUT_COLS = 256
VMEM_LIMIT = 60 * 1024 * 1024


def _silu(z):
    return z * jax.nn.sigmoid(z)


def _pipelined(step, n_blocks, body, even_bufs, odd_bufs):
    @pl.when(step == 0)
    def _():
        body(even_bufs, odd_bufs, True, False)

    @pl.when((step > 0) & (step < n_blocks) & (step % 2 == 0))
    def _():
        body(even_bufs, odd_bufs, True, True)

    @pl.when((step < n_blocks) & (step % 2 == 1))
    def _():
        body(odd_bufs, even_bufs, True, True)

    @pl.when(step == n_blocks)
    def _():
        last_read = odd_bufs if n_blocks % 2 == 0 else even_bufs
        body(None, last_read, False, True)


def _in_proj_kernel(x_ref, g_ref, w_ref, u_ref, zs_ref, q_ref, k_ref, v_ref, za_ref, w_scr):
    @pl.when((pl.program_id(0) == 0) & (pl.program_id(1) == 0))
    def _():
        w_scr[...] = w_ref[...].astype(BF16)

    def emit(rows, tile, acc):
        lo = tile * PROJ_COLS
        if lo < OFF_ZS:
            u_ref[rows, lo - OFF_U:lo - OFF_U + PROJ_COLS] = acc.astype(BF16)
        elif lo < OFF_Q:
            zs_ref[rows, lo - OFF_ZS:lo - OFF_ZS + PROJ_COLS] = _silu(acc).astype(BF16)
        elif lo < OFF_K:
            q_ref[rows, lo - OFF_Q:lo - OFF_Q + PROJ_COLS] = (
                acc * (1.0 / math.sqrt(HEAD_DIM))).astype(BF16)
        elif lo < OFF_ZA:
            k_ref[rows, :] = acc[:, :D_KV].astype(BF16)
            v_ref[rows, :] = acc[:, D_KV:].astype(BF16)
        else:
            za_ref[rows, lo - OFF_ZA:lo - OFF_ZA + PROJ_COLS] = _silu(acc).astype(BF16)

    gain = g_ref[...]
    for piece in range(TOK_IN // NORM_ROWS):
        rows = slice(piece * NORM_ROWS, (piece + 1) * NORM_ROWS)
        x = x_ref[rows, :]
        ms = jnp.mean(x * x, axis=-1, keepdims=True)
        hn = (x * lax.rsqrt(ms + EPS) * gain).astype(BF16)
        for tile in range(D_IN // PROJ_COLS):
            cols = slice(tile * PROJ_COLS, (tile + 1) * PROJ_COLS)
            emit(rows, tile, jnp.dot(hn, w_scr[:, cols], preferred_element_type=F32))


def _in_proj(x, g, w_in):
    bsz, seq, _ = x.shape
    tok = lambda width: pl.BlockSpec((None, TOK_IN, width), lambda b, i: (b, i, 0))
    widths = (D_SSM, D_SSM, D_ATTN, D_KV, D_KV, D_ATTN)
    return pl.pallas_call(
        _in_proj_kernel,
        grid=(bsz, seq // TOK_IN),
        in_specs=[
            tok(D_MODEL),
            pl.BlockSpec((1, D_MODEL), lambda b, i: (0, 0)),
            pl.BlockSpec((D_MODEL, D_IN), lambda b, i: (0, 0), pipeline_mode=pl.Buffered(1)),
        ],
        out_specs=tuple(tok(w) for w in widths),
        out_shape=tuple(jax.ShapeDtypeStruct((bsz, seq, w), BF16) for w in widths),
        scratch_shapes=[pltpu.VMEM((D_MODEL, D_IN), BF16)],
        compiler_params=pltpu.CompilerParams(
            dimension_semantics=("arbitrary", "arbitrary"), vmem_limit_bytes=VMEM_LIMIT),
        name="in_proj",
    )(x, g, w_in)


def _ssm_kernel(u_ref, perm_ref, permt_ref, win_ref, l2_ref, wout_ref, wdir_ref, d_ref,
                wglu_f32, bglu_ref, o_ref,
                wglu_ref, sre_ref, sim_ref, st_ref, y_ref,
                g0_ref, ut0_ref, g1_ref, ut1_ref, *, bsz, n_blocks):
    subs = T_SCAN // T_SUB
    rows = bsz * T_SUB
    half = rows // 2
    step = pl.program_id(0)

    @pl.when(step == 0)
    def _():
        wglu_ref[...] = wglu_f32[...].astype(BF16)
        sre_ref[...] = jnp.zeros_like(sre_ref)
        sim_ref[...] = jnp.zeros_like(sim_ref)

    def body(write, read, front, back):
        g_w, ut_w = write if front else (None, None)
        g_r, ut_r = read if back else (None, None)
        groups = bsz // PERM_BATCH
        grows = PERM_BATCH * T_SUB
        u_new, u_before = [], []
        for sb in range(subs if front else 0):
            steps = slice(sb * T_SUB, (sb + 1) * T_SUB)
            permuted = [
                jnp.dot(perm_ref[...],
                        u_ref[g * PERM_BATCH:(g + 1) * PERM_BATCH, steps].reshape(grows, D_SSM),
                        preferred_element_type=F32).reshape(T_SUB, 1, PERM_BATCH, D_SSM)
                for g in range(groups)]
            u_new.append(jnp.concatenate(permuted, axis=1).reshape(rows, D_SSM).astype(BF16))
            ut_w[sb] = u_new[sb]
            if sb > 0:
                u_last = u_new[sb - 1][rows - bsz:]
            else:
                u_last = ut_r[subs - 1, rows - bsz:, :] if back else jnp.zeros((bsz, D_SSM), BF16)
            u_before.append(jnp.concatenate([u_last, u_new[sb][half:rows - bsz]], axis=0))
        for j in range(N_SLABS):
            ch = slice(j * SLAB_CH, (j + 1) * SLAB_CH)
            drive_in = [jnp.concatenate([u_before[sb][:, ch], u_new[sb][:half, ch]], axis=1)
                        for sb in range(subs if front else 0)]
            for c in range(SLAB_STATE // SCAN_COLS):
                lo = j * SLAB_STATE + c * SCAN_COLS
                cre = slice(c * SCAN_COLS, (c + 1) * SCAN_COLS)
                cim = slice(SLAB_STATE + c * SCAN_COLS, SLAB_STATE + (c + 1) * SCAN_COLS)
                if back:
                    l2r = jnp.broadcast_to(l2_ref[0, :, lo:lo + SCAN_COLS], (bsz, SCAN_COLS))
                    l2i = jnp.broadcast_to(l2_ref[1, :, lo:lo + SCAN_COLS], (bsz, SCAN_COLS))
                    sr = sre_ref[:, lo:lo + SCAN_COLS]
                    si = sim_ref[:, lo:lo + SCAN_COLS]
                for sb, piece in [(sb, piece) for sb in range(subs)
                                  for piece in range(half // PIECE_ROWS)]:
                    rk = slice(piece * PIECE_ROWS, (piece + 1) * PIECE_ROWS)
                    if front:
                        g_w[sb, j, rk, cre] = jnp.dot(drive_in[sb][rk], win_ref[0, j, :, cre],
                                                      preferred_element_type=F32)
                        g_w[sb, j, rk, cim] = jnp.dot(drive_in[sb][rk], win_ref[1, j, :, cre],
                                                      preferred_element_type=F32)
                    if not back:
                        continue
                    for k in range(rk.start // bsz, rk.stop // bsz):
                        r = slice(k * bsz, (k + 1) * bsz)
                        sr, si = (l2r * sr - l2i * si + g_r[sb, j, r, cre],
                                  l2r * si + l2i * sr + g_r[sb, j, r, cim])
                        st_ref[sb, j, r, cre] = sr.astype(BF16)
                        st_ref[sb, j, r, cim] = si.astype(BF16)
                    yk = (jnp.dot(st_ref[sb, j, rk, cre], wout_ref[0, j, cre, :],
                                  preferred_element_type=F32)
                          + jnp.dot(st_ref[sb, j, rk, cim], wout_ref[1, j, cre, :],
                                    preferred_element_type=F32))
                    odd_rows = slice(half + rk.start, half + rk.stop)
                    if c == 0:
                        y_ref[sb, rk, ch] = yk[:, :SLAB_CH]
                        y_ref[sb, odd_rows, ch] = yk[:, SLAB_CH:] + jnp.dot(
                            ut_r[sb, odd_rows, ch], wdir_ref[j], preferred_element_type=F32)
                    else:
                        y_ref[sb, rk, ch] += yk[:, :SLAB_CH]
                        y_ref[sb, odd_rows, ch] += yk[:, SLAB_CH:]
                if back:
                    sre_ref[:, lo:lo + SCAN_COLS] = sr
                    sim_ref[:, lo:lo + SCAN_COLS] = si
        if not back:
            return

        for sb in range(subs):
            y = y_ref[sb] + d_ref[...] * ut_r[sb].astype(F32)
            g = jax.nn.gelu(y)
            gate = jnp.dot(g.astype(BF16), wglu_ref[...], preferred_element_type=F32) + bglu_ref[...]
            out_tb = (g * jax.nn.sigmoid(gate)).reshape(T_SUB, groups, PERM_BATCH, D_SSM)
            for grp in range(groups):
                out_g = out_tb[:, grp].reshape(grows, D_SSM).astype(BF16)
                out_bt = jnp.dot(permt_ref[...], out_g, preferred_element_type=F32).astype(BF16)
                o_ref[grp * PERM_BATCH:(grp + 1) * PERM_BATCH, sb * T_SUB:(sb + 1) * T_SUB] = (
                    out_bt.reshape(PERM_BATCH, T_SUB, D_SSM))

    _pipelined(step, n_blocks, body, (g0_ref, ut0_ref), (g1_ref, ut1_ref))


def _row_permutation():
    rows = PERM_BATCH * T_SUB
    perm = np.zeros((rows, rows), np.float32)
    t, b = np.meshgrid(np.arange(T_SUB), np.arange(PERM_BATCH), indexing="ij")
    dst = (t % 2) * (rows // 2) + (t // 2) * PERM_BATCH + b
    perm[dst.ravel(), (b * T_SUB + t).ravel()] = 1.0
    return perm


def _ssm(u, params, d, wglu, bglu):
    bsz, seq, _ = u.shape
    subs = T_SCAN // T_SUB
    rows = T_SUB * bsz
    half = rows // 2
    nblk = seq // T_SCAN
    assert bsz % PERM_BATCH == 0
    perm = _row_permutation()
    full = lambda shape: pl.BlockSpec(shape, lambda f: (0,) * len(shape))
    g_scratch = pltpu.VMEM((subs, N_SLABS, half, 2 * SLAB_STATE), F32)
    ut_scratch = pltpu.VMEM((subs, rows, D_SSM), BF16)
    return pl.pallas_call(
        functools.partial(_ssm_kernel, bsz=bsz, n_blocks=nblk),
        grid=(nblk + 1,),
        in_specs=[
            pl.BlockSpec((bsz, T_SCAN, D_SSM), lambda f: (0, jnp.minimum(f, nblk - 1), 0)),
            full(perm.shape), full(perm.shape),
            full((2, N_SLABS, 2 * SLAB_CH, SLAB_STATE)), full((2, 1, N_STATE)),
            full((2, N_SLABS, SLAB_STATE, 2 * SLAB_CH)), full((N_SLABS, SLAB_CH, SLAB_CH)),
            full((1, D_SSM)),
            pl.BlockSpec((D_SSM, D_SSM), lambda f: (0, 0), pipeline_mode=pl.Buffered(1)),
            full((1, D_SSM)),
        ],
        out_specs=pl.BlockSpec((bsz, T_SCAN, D_SSM), lambda f: (0, jnp.maximum(f - 1, 0), 0)),
        out_shape=jax.ShapeDtypeStruct((bsz, seq, D_SSM), BF16),
        scratch_shapes=[
            pltpu.VMEM((D_SSM, D_SSM), BF16),
            pltpu.VMEM((bsz, N_STATE), F32),
            pltpu.VMEM((bsz, N_STATE), F32),
            pltpu.VMEM((subs, N_SLABS, half, 2 * SLAB_STATE), BF16),
            pltpu.VMEM((subs, rows, D_SSM), F32),
            g_scratch, ut_scratch, g_scratch, ut_scratch,
        ],
        compiler_params=pltpu.CompilerParams(
            dimension_semantics=("arbitrary",), vmem_limit_bytes=VMEM_LIMIT),
        name="s5_scan",
    )(u, jnp.asarray(perm, BF16), jnp.asarray(perm.T, BF16), *params, d, wglu, bglu)


def _attention_pieces(first, sink_ref, q_ref, k_ref, v_ref, kp_ref, vp_ref, za_ref, bias_ref,
                      half_ref, a_ref):
    low_q = lax.broadcasted_iota(jnp.int32, (WINDOW, LANES), 1) < HEAD_DIM
    kv_cache = {}

    def swap_halves(t):
        packed = pltpu.bitcast(t, jnp.uint32)
        return pltpu.bitcast(pltpu.roll(packed, HEAD_DIM, axis=1), BF16)

    def keys_values(jj, g):
        if jj not in kv_cache:
            rows = slice(jj * WINDOW, (jj + 1) * WINDOW)
            if jj == 0:
                kp, vp = kp_ref[...], vp_ref[...]
            else:
                prev = slice((jj - 1) * WINDOW, jj * WINDOW)
                kp, vp = k_ref[prev, :], v_ref[prev, :]
            kk = jnp.concatenate([kp, k_ref[rows, :]], axis=0)
            vv = jnp.concatenate([vp, v_ref[rows, :]], axis=0)
            kv_cache[jj] = ((kk, swap_halves(kk)), (vv, swap_halves(vv)))
        if (jj, g) not in kv_cache:
            keep_even, keep_odd = half_ref[0], half_ref[1]
            k2, v2 = kv_cache[jj]
            kcat = jnp.concatenate([k2[g] * keep_even, k2[1 - g] * keep_odd], axis=0)
            vcat = jnp.concatenate(
                [jnp.concatenate([v2[g] * keep_even, keep_even], axis=1),
                 jnp.concatenate([v2[1 - g] * keep_odd, keep_odd], axis=1)], axis=0)
            kv_cache[(jj, g)] = (kcat, vcat)
        return kv_cache[(jj, g)]

    items = [(jj, pair) for jj in range(TOK_OUT // WINDOW) for pair in range(N_HEADS // 2)]
    scores, probs = {}, {}

    def where(i):
        jj, pair = items[i]
        return (jj, pair, slice(jj * WINDOW, (jj + 1) * WINDOW),
                slice(pair * LANES, (pair + 1) * LANES))

    def stage_scores(i):
        jj, pair, rows, lanes = where(i)
        kcat, _ = keys_values(jj, pair // (Q_PER_KV // 2))
        s = lax.dot_general(q_ref[rows, lanes], kcat, (((1,), (1,)), ((), ())),
                            preferred_element_type=F32)
        scores[i] = s + (bias_ref[pair + (N_HEADS // 2) * first] if jj == 0 else bias_ref[pair])

    def stage_softmax(i):
        _, pair, _, _ = where(i)
        s = scores.pop(i)
        sink_e, sink_o = sink_ref[2 * pair], sink_ref[2 * pair + 1]
        m_e = jnp.maximum(jnp.max(s[:, :2 * WINDOW], axis=-1, keepdims=True), sink_e)
        m_o = jnp.maximum(jnp.max(s[:, 2 * WINDOW:], axis=-1, keepdims=True), sink_o)
        e = jnp.concatenate([jnp.exp(s[:, :2 * WINDOW] - m_e),
                             jnp.exp(s[:, 2 * WINDOW:] - m_o)], axis=1).astype(BF16)
        probs[i] = (e, jnp.where(low_q, jnp.exp(sink_e - m_e), jnp.exp(sink_o - m_o)))

    def stage_output(i):
        jj, pair, rows, lanes = where(i)
        _, vcat = keys_values(jj, pair // (Q_PER_KV // 2))
        e, sink_term = probs.pop(i)
        o = jnp.dot(e, vcat, preferred_element_type=F32)
        den = o[:, LANES:] + sink_term
        a_ref[rows, lanes] = (o[:, :LANES] / den * za_ref[rows, lanes].astype(F32)).astype(BF16)

    def slot(it):
        if it < len(items):
            stage_scores(it)
        if 0 <= it - 1 < len(items):
            stage_softmax(it - 1)
        if 0 <= it - 2 < len(items):
            stage_output(it - 2)

    return [functools.partial(slot, it) for it in range(len(items) + 2)]


def _attn_out_kernel(sink_ref, x_ref, p_ref, ssm_ref, zs_ref, q_ref, k_ref, v_ref, kp_ref, vp_ref,
                     za_ref, bias_ref, half_ref, wout_f32, pg_ref, wgate_f32, bgate_ref, wproj_f32,
                     o_ref, wout_ref, wgate_ref, wproj_ref, a0_ref, a1_ref,
                     *, blocks_per_seq, n_blocks):
    step = pl.program_id(0)
    seq_block = jnp.minimum(step, n_blocks - 1) % blocks_per_seq
    first = (seq_block == 0).astype(jnp.int32)

    @pl.when(step == 0)
    def _():
        wout_ref[...] = wout_f32[...].astype(BF16)
        wgate_ref[...] = wgate_f32[...].astype(BF16)
        wproj_ref[...] = wproj_f32[...].astype(BF16)

    def body(a_w, a_r, front, back):
        pieces = []
        if front:
            pieces = _attention_pieces(first, sink_ref, q_ref, k_ref, v_ref, kp_ref, vp_ref,
                                       za_ref, bias_ref, half_ref, a_w)
        n_chunks = D_MODEL // OUT_COLS
        per_chunk = max(len(pieces) - 2, 0) // (2 * n_chunks)
        pieces = iter(pieces)

        def run_pieces(count=per_chunk):
            for piece in (list(pieces) if count is None else [p for _, p in zip(range(count), pieces)]):
                piece()

        if not back:
            run_pieces(None)
            return
        run_pieces(2)

        s5 = (ssm_ref[...].astype(F32) * zs_ref[...].astype(F32)).astype(BF16)
        a_prev = a_r[...]
        mixed = []
        for n in range(n_chunks):
            cols = slice(n * OUT_COLS, (n + 1) * OUT_COLS)
            mixed.append(jnp.dot(s5, wout_ref[:D_SSM, cols], preferred_element_type=F32)
                         + jnp.dot(a_prev, wout_ref[D_SSM:, cols], preferred_element_type=F32))
            run_pieces()
        mixed = jnp.concatenate(mixed, axis=1)
        ms = jnp.mean(mixed * mixed, axis=-1, keepdims=True)
        h1 = x_ref[...] + mixed * lax.rsqrt(ms + EPS) * pg_ref[...]
        h1b = h1.astype(BF16)
        pb = p_ref[...].astype(BF16)
        for n in range(n_chunks):
            cols = slice(n * OUT_COLS, (n + 1) * OUT_COLS)
            gate = jax.nn.sigmoid(
                jnp.dot(h1b, wgate_ref[:, cols], preferred_element_type=F32) + bgate_ref[:, cols])
            ple = jnp.dot(pb, wproj_ref[:, cols], preferred_element_type=F32)
            o_ref[:, cols] = h1[:, cols] + gate * ple
            run_pieces()
        run_pieces(None)

    _pipelined(step, n_blocks, body, a0_ref, a1_ref)


def _attn_bias_table():
    q_idx = np.arange(WINDOW)[:, None]
    s_idx = np.arange(2 * WINDOW)[None, :]
    dist = q_idx + WINDOW - s_idx
    valid = (dist >= 0) & (dist < WINDOW)
    slopes = np.exp2(-8.0 * (np.arange(N_HEADS, dtype=np.float32) + 1.0) / N_HEADS).astype(np.float32)
    bias = -slopes[:, None, None] * dist.astype(np.float32)[None]
    normal = np.where(valid[None], bias, -np.inf)
    first = np.where((valid & (s_idx >= WINDOW))[None], bias, -np.inf)
    paired = lambda t: t.reshape(N_HEADS // 2, 2, WINDOW, 2 * WINDOW).transpose(0, 2, 1, 3).reshape(
        N_HEADS // 2, WINDOW, 4 * WINDOW)
    return np.concatenate([paired(normal), paired(first)], axis=0).astype(np.float32)


def _attn_out(sinks, x, p, ssm, zs, q, k, v, za, wout, pg, wgate, bgate, wproj):
    bsz, seq, _ = x.shape
    sub = TOK_OUT // WINDOW
    per_seq = seq // TOK_OUT
    n_blocks = bsz * per_seq

    def cur(f):
        f = jnp.minimum(f, n_blocks - 1)
        return f // per_seq, f % per_seq

    def prv(f):
        f = jnp.maximum(f - 1, 0)
        return f // per_seq, f % per_seq

    front = lambda width: pl.BlockSpec((None, TOK_OUT, width), lambda f: (*cur(f), 0))
    back = lambda width: pl.BlockSpec((None, TOK_OUT, width), lambda f: (*prv(f), 0))
    prev_kv = pl.BlockSpec(
        (None, WINDOW, D_KV),
        lambda f: (cur(f)[0], jnp.maximum(cur(f)[1] * sub - 1, 0), 0))
    full = lambda shape: pl.BlockSpec(shape, lambda f: (0,) * len(shape))
    once = lambda shape: pl.BlockSpec(shape, lambda f: (0,) * len(shape),
                                      pipeline_mode=pl.Buffered(1))
    bias = jnp.asarray(_attn_bias_table())
    lane_is_even = np.broadcast_to(np.arange(LANES) < HEAD_DIM, (2 * WINDOW, LANES))
    half = jnp.asarray(np.stack([lane_is_even, ~lane_is_even]), BF16)
    a_scratch = pltpu.VMEM((TOK_OUT, D_ATTN), BF16)
    return pl.pallas_call(
        functools.partial(_attn_out_kernel, blocks_per_seq=per_seq, n_blocks=n_blocks),
        grid=(n_blocks + 1,),
        in_specs=[
            pl.BlockSpec(memory_space=pltpu.SMEM),
            back(D_MODEL), back(D_PLE), back(D_SSM), back(D_SSM), front(D_ATTN),
            front(D_KV), front(D_KV), prev_kv, prev_kv, front(D_ATTN),
            once(bias.shape), once(half.shape),
            once((D_MODEL, D_MODEL)), full((1, D_MODEL)),
            once((D_MODEL, D_MODEL)), full((1, D_MODEL)), once((D_PLE, D_MODEL)),
        ],
        out_specs=back(D_MODEL),
        out_shape=jax.ShapeDtypeStruct((bsz, seq, D_MODEL), F32),
        scratch_shapes=[pltpu.VMEM((D_MODEL, D_MODEL), BF16), pltpu.VMEM((D_MODEL, D_MODEL), BF16),
                        pltpu.VMEM((D_PLE, D_MODEL), BF16), a_scratch, a_scratch],
        compiler_params=pltpu.CompilerParams(
            dimension_semantics=("arbitrary",), vmem_limit_bytes=VMEM_LIMIT),
        name="attn_out",
    )(sinks, x, p, ssm, zs, q, k, v, k, v, za, bias, half, wout, pg, wgate, bgate, wproj)


def _ssm_params(lam_re, lam_im, log_step, b_re, b_im, c_re, c_im):
    lr, li = lam_re.astype(F32), lam_im.astype(F32)
    step = jnp.exp(log_step.astype(F32))[:, None]
    mag = jnp.exp(lr * step)
    lbr, lbi = mag * jnp.cos(li * step), mag * jnp.sin(li * step)
    den = lr * lr + li * li
    fr = ((lbr - 1.0) * lr + lbi * li) / den
    fi = (lbi * lr - (lbr - 1.0) * li) / den
    br, bi = b_re.astype(F32), b_im.astype(F32)
    bbr = fr[..., None] * br - fi[..., None] * bi
    bbi = fr[..., None] * bi + fi[..., None] * br
    cr, ci = c_re.astype(F32), c_im.astype(F32)
    l2 = jnp.stack([lbr * lbr - lbi * lbi, 2.0 * lbr * lbi])
    lbbr = lbr[..., None] * bbr - lbi[..., None] * bbi
    lbbi = lbr[..., None] * bbi + lbi[..., None] * bbr
    clr = cr * lbr[:, None, :] - ci * lbi[:, None, :]
    cli = cr * lbi[:, None, :] + ci * lbr[:, None, :]
    cb = jnp.sum(cr[:, :, :, None] * bbr[:, None, :, :] - ci[:, :, :, None] * bbi[:, None, :, :],
                 axis=2)
    gps = SLAB_CH // SSM_GROUP_CH
    n, p = SSM_STATE, SSM_GROUP_CH
    same_group = (np.arange(SLAB_CH)[:, None] // p == np.arange(SLAB_STATE)[None, :] // n)
    expand_p = np.tile(np.eye(p, dtype=np.float32), (gps, 1))
    expand_n = np.tile(np.eye(n, dtype=np.float32), (gps, 1))

    w_in = jnp.stack([jnp.stack([lbbr, bbr]), jnp.stack([lbbi, bbi])])
    w_in = w_in.reshape(2, 2, N_SLABS, SLAB_STATE, p)
    w_in = jnp.einsum('rp,absqp->asbrq', expand_p, w_in) * same_group.astype(np.float32)
    w_in = w_in.astype(BF16).reshape(2, N_SLABS, 2 * SLAB_CH, SLAB_STATE)
    w_out = jnp.stack([jnp.stack([cr, clr]), jnp.stack([-ci, -cli])])
    w_out = w_out.reshape(2, 2, N_SLABS, SLAB_CH, n)
    w_out = jnp.einsum('rn,aesqn->aesrq', expand_n, w_out) * same_group.T.astype(np.float32)
    w_out = jnp.concatenate([w_out[:, 0], w_out[:, 1]], axis=-1).astype(BF16)
    same_group_pp = same_group[:, ::n // p]
    w_dir = jnp.einsum('rq,sxq->srx', expand_p, cb.reshape(N_SLABS, SLAB_CH, p))
    w_dir = (w_dir * same_group_pp.astype(np.float32)).astype(BF16)
    return w_in, l2.reshape(2, 1, N_STATE), w_out, w_dir


def _layer(h, p, pre_norm_g, w_in, ssm_lam_re, ssm_lam_im, ssm_log_step, ssm_b_re, ssm_b_im,
           ssm_c_re, ssm_c_im, ssm_d, ssm_w_glu, ssm_b_glu, attn_sinks, w_out, post_norm_g,
           pl_w_proj, pl_w_gate, pl_b_gate):
    row = lambda a: a.astype(F32).reshape(1, -1)
    u, zs, q, k, v, za = _in_proj(h, row(pre_norm_g), w_in.astype(F32))
    params = _ssm_params(
        ssm_lam_re, ssm_lam_im, ssm_log_step, ssm_b_re, ssm_b_im, ssm_c_re, ssm_c_im)
    ssm = _ssm(u, params, row(ssm_d), ssm_w_glu.astype(F32), row(ssm_b_glu))
    return _attn_out(attn_sinks.astype(F32), h, p, ssm, zs, q, k, v, za,
                     w_out.astype(F32), row(post_norm_g), pl_w_gate.astype(F32), row(pl_b_gate),
                     pl_w_proj.astype(F32))


def kernel(x, p, pre_norm_g, w_in, ssm_lam_re, ssm_lam_im, ssm_log_step, ssm_b_re, ssm_b_im,
           ssm_c_re, ssm_c_im, ssm_d, ssm_w_glu, ssm_b_glu, attn_sinks, w_out, post_norm_g,
           pl_w_proj, pl_w_gate, pl_b_gate):
    h = x
    for i in range(p.shape[0]):
        h = _layer(h, p[i], pre_norm_g[i], w_in[i], ssm_lam_re[i], ssm_lam_im[i], ssm_log_step[i],
                   ssm_b_re[i], ssm_b_im[i], ssm_c_re[i], ssm_c_im[i], ssm_d[i], ssm_w_glu[i],
                   ssm_b_glu[i], attn_sinks[i], w_out[i], post_norm_g[i], pl_w_proj[i],
                   pl_w_gate[i], pl_b_gate[i])
    return h
```

```python
import functools
import math

import numpy as np
import jax
import jax.numpy as jnp
from jax import lax
from jax.experimental import pallas as pl
from jax.experimental.pallas import tpu as pltpu

F32 = jnp.float32
BF16 = jnp.bfloat16

D_MODEL = 1024
D_SSM = 512
D_ATTN = 512
SSM_GROUP_CH = 16
SSM_GROUPS = D_SSM // SSM_GROUP_CH
SSM_STATE = 64
N_STATE = SSM_GROUPS * SSM_STATE
HEAD_DIM = 64
N_HEADS = D_ATTN // HEAD_DIM
KV_HEADS = 2
Q_PER_KV = N_HEADS // KV_HEADS
D_KV = KV_HEADS * HEAD_DIM
WINDOW = 128
D_PLE = 256
EPS = 1e-6

LANES = 128
assert D_KV == LANES and 2 * HEAD_DIM == LANES

OFF_U, OFF_ZS, OFF_Q, OFF_K, OFF_V, OFF_ZA, D_IN = 0, 512, 1024, 1536, 1664, 1792, 2304

SLAB_CH = LANES
N_SLABS = D_SSM // SLAB_CH
SLAB_STATE = (SLAB_CH // SSM_GROUP_CH) * SSM_STATE
SCAN_COLS = 256

TOK_IN = 1024
NORM_ROWS = 256
T_SCAN = 64
T_SUB = 32
PIECE_ROWS = 128
PERM_BATCH = 8
TOK_OUT = 1024
PROJ_COLS = 256
OUT_COLS = 256
MIB = 1024 * 1024
VMEM_IN_PROJ, VMEM_S5, VMEM_ATTN_OUT = 44 * MIB, 40 * MIB, 58 * MIB


def _silu(z):
    return z * jax.nn.sigmoid(z)


def _pipelined(step, n_blocks, body, even_bufs, odd_bufs):
    @pl.when(step == 0)
    def _():
        body(even_bufs, odd_bufs, True, False)

    @pl.when((step > 0) & (step < n_blocks) & (step % 2 == 0))
    def _():
        body(even_bufs, odd_bufs, True, True)

    @pl.when((step < n_blocks) & (step % 2 == 1))
    def _():
        body(odd_bufs, even_bufs, True, True)

    @pl.when(step == n_blocks)
    def _():
        last_read = odd_bufs if n_blocks % 2 == 0 else even_bufs
        body(None, last_read, False, True)


def _in_proj_kernel(x_ref, g_ref, w_ref, u_ref, zs_ref, q_ref, k_ref, v_ref, za_ref, w_scr):
    @pl.when((pl.program_id(0) == 0) & (pl.program_id(1) == 0))
    def _():
        w_scr[...] = w_ref[...].astype(BF16)

    def emit(rows, tile, acc):
        lo = tile * PROJ_COLS
        if lo < OFF_ZS:
            u_ref[rows, lo - OFF_U:lo - OFF_U + PROJ_COLS] = acc.astype(BF16)
        elif lo < OFF_Q:
            zs_ref[rows, lo - OFF_ZS:lo - OFF_ZS + PROJ_COLS] = _silu(acc).astype(BF16)
        elif lo < OFF_K:
            q_ref[rows, lo - OFF_Q:lo - OFF_Q + PROJ_COLS] = (
                acc * (1.0 / math.sqrt(HEAD_DIM))).astype(BF16)
        elif lo < OFF_ZA:
            k_ref[rows, :] = acc[:, :D_KV].astype(BF16)
            v_ref[rows, :] = acc[:, D_KV:].astype(BF16)
        else:
            za_ref[rows, lo - OFF_ZA:lo - OFF_ZA + PROJ_COLS] = _silu(acc).astype(BF16)

    gain = g_ref[...]
    for piece in range(TOK_IN // NORM_ROWS):
        rows = slice(piece * NORM_ROWS, (piece + 1) * NORM_ROWS)
        x = x_ref[rows, :]
        ms = jnp.mean(x * x, axis=-1, keepdims=True)
        hn = (x * lax.rsqrt(ms + EPS) * gain).astype(BF16)
        for tile in range(D_IN // PROJ_COLS):
            cols = slice(tile * PROJ_COLS, (tile + 1) * PROJ_COLS)
            emit(rows, tile, jnp.dot(hn, w_scr[:, cols], preferred_element_type=F32))


def _in_proj(x, g, w_in):
    bsz, seq, _ = x.shape
    tok = lambda width: pl.BlockSpec((None, TOK_IN, width), lambda b, i: (b, i, 0))
    widths = (D_SSM, D_SSM, D_ATTN, D_KV, D_KV, D_ATTN)
    return pl.pallas_call(
        _in_proj_kernel,
        grid=(bsz, seq // TOK_IN),
        in_specs=[
            tok(D_MODEL),
            pl.BlockSpec((1, D_MODEL), lambda b, i: (0, 0)),
            pl.BlockSpec((D_MODEL, D_IN), lambda b, i: (0, 0), pipeline_mode=pl.Buffered(1)),
        ],
        out_specs=tuple(tok(w) for w in widths),
        out_shape=tuple(jax.ShapeDtypeStruct((bsz, seq, w), BF16) for w in widths),
        scratch_shapes=[pltpu.VMEM((D_MODEL, D_IN), BF16)],
        compiler_params=pltpu.CompilerParams(
            dimension_semantics=("arbitrary", "arbitrary"), vmem_limit_bytes=VMEM_IN_PROJ),
        name="in_proj",
    )(x, g, w_in)


def _ssm_kernel(u_ref, perm_ref, permt_ref, win_ref, l2_ref, wout_ref, wdir_ref, d_ref,
                wglu_f32, bglu_ref, o_ref,
                wglu_ref, sre_ref, sim_ref, st_ref, y_ref,
                g0_ref, ut0_ref, g1_ref, ut1_ref, *, bsz, n_blocks):
    subs = T_SCAN // T_SUB
    rows = bsz * T_SUB
    half = rows // 2
    step = pl.program_id(0)

    @pl.when(step == 0)
    def _():
        wglu_ref[...] = wglu_f32[...].astype(BF16)
        sre_ref[...] = jnp.zeros_like(sre_ref)
        sim_ref[...] = jnp.zeros_like(sim_ref)

    def body(write, read, front, back):
        g_w, ut_w = write if front else (None, None)
        g_r, ut_r = read if back else (None, None)
        groups = bsz // PERM_BATCH
        grows = PERM_BATCH * T_SUB
        u_new, u_before = [], []
        for sb in range(subs if front else 0):
            steps = slice(sb * T_SUB, (sb + 1) * T_SUB)
            permuted = [
                jnp.dot(perm_ref[...],
                        u_ref[g * PERM_BATCH:(g + 1) * PERM_BATCH, steps].reshape(grows, D_SSM),
                        preferred_element_type=F32).reshape(T_SUB, 1, PERM_BATCH, D_SSM)
                for g in range(groups)]
            u_new.append(jnp.concatenate(permuted, axis=1).reshape(rows, D_SSM).astype(BF16))
            ut_w[sb] = u_new[sb]
            if sb > 0:
                u_last = u_new[sb - 1][rows - bsz:]
            else:
                u_last = ut_r[subs - 1, rows - bsz:, :] if back else jnp.zeros((bsz, D_SSM), BF16)
            u_before.append(jnp.concatenate([u_last, u_new[sb][half:rows - bsz]], axis=0))
        for j in range(N_SLABS):
            ch = slice(j * SLAB_CH, (j + 1) * SLAB_CH)
            drive_in = [jnp.concatenate([u_before[sb][:, ch], u_new[sb][:half, ch]], axis=1)
                        for sb in range(subs if front else 0)]
            for c in range(SLAB_STATE // SCAN_COLS):
                lo = j * SLAB_STATE + c * SCAN_COLS
                cre = slice(c * SCAN_COLS, (c + 1) * SCAN_COLS)
                cim = slice(SLAB_STATE + c * SCAN_COLS, SLAB_STATE + (c + 1) * SCAN_COLS)
                if back:
                    l2r = jnp.broadcast_to(l2_ref[0, :, lo:lo + SCAN_COLS], (bsz, SCAN_COLS))
                    l2i = jnp.broadcast_to(l2_ref[1, :, lo:lo + SCAN_COLS], (bsz, SCAN_COLS))
                    sr = sre_ref[:, lo:lo + SCAN_COLS]
                    si = sim_ref[:, lo:lo + SCAN_COLS]
                for sb, piece in [(sb, piece) for sb in range(subs)
                                  for piece in range(half // PIECE_ROWS)]:
                    rk = slice(piece * PIECE_ROWS, (piece + 1) * PIECE_ROWS)
                    if front:
                        g_w[sb, j, rk, cre] = jnp.dot(drive_in[sb][rk], win_ref[0, j, :, cre],
                                                      preferred_element_type=F32)
                        g_w[sb, j, rk, cim] = jnp.dot(drive_in[sb][rk], win_ref[1, j, :, cre],
                                                      preferred_element_type=F32)
                    if not back:
                        continue
                    for k in range(rk.start // bsz, rk.stop // bsz):
                        r = slice(k * bsz, (k + 1) * bsz)
                        sr, si = (l2r * sr - l2i * si + g_r[sb, j, r, cre],
                                  l2r * si + l2i * sr + g_r[sb, j, r, cim])
                        st_ref[sb, j, r, cre] = sr.astype(BF16)
                        st_ref[sb, j, r, cim] = si.astype(BF16)
                    yk = (jnp.dot(st_ref[sb, j, rk, cre], wout_ref[0, j, cre, :],
                                  preferred_element_type=F32)
                          + jnp.dot(st_ref[sb, j, rk, cim], wout_ref[1, j, cre, :],
                                    preferred_element_type=F32))
                    odd_rows = slice(half + rk.start, half + rk.stop)
                    if c == 0:
                        y_ref[sb, rk, ch] = yk[:, :SLAB_CH]
                        y_ref[sb, odd_rows, ch] = yk[:, SLAB_CH:] + jnp.dot(
                            ut_r[sb, odd_rows, ch], wdir_ref[j], preferred_element_type=F32)
                    else:
                        y_ref[sb, rk, ch] += yk[:, :SLAB_CH]
                        y_ref[sb, odd_rows, ch] += yk[:, SLAB_CH:]
                if back:
                    sre_ref[:, lo:lo + SCAN_COLS] = sr
                    sim_ref[:, lo:lo + SCAN_COLS] = si
        if not back:
            return

        for sb in range(subs):
            y = y_ref[sb] + d_ref[...] * ut_r[sb].astype(F32)
            g = jax.nn.gelu(y)
            gate = jnp.dot(g.astype(BF16), wglu_ref[...], preferred_element_type=F32) + bglu_ref[...]
            out_tb = (g * jax.nn.sigmoid(gate)).reshape(T_SUB, groups, PERM_BATCH, D_SSM)
            for grp in range(groups):
                out_g = out_tb[:, grp].reshape(grows, D_SSM).astype(BF16)
                out_bt = jnp.dot(permt_ref[...], out_g, preferred_element_type=F32).astype(BF16)
                o_ref[grp * PERM_BATCH:(grp + 1) * PERM_BATCH, sb * T_SUB:(sb + 1) * T_SUB] = (
                    out_bt.reshape(PERM_BATCH, T_SUB, D_SSM))

    _pipelined(step, n_blocks, body, (g0_ref, ut0_ref), (g1_ref, ut1_ref))


def _row_permutation():
    rows = PERM_BATCH * T_SUB
    perm = np.zeros((rows, rows), np.float32)
    t, b = np.meshgrid(np.arange(T_SUB), np.arange(PERM_BATCH), indexing="ij")
    dst = (t % 2) * (rows // 2) + (t // 2) * PERM_BATCH + b
    perm[dst.ravel(), (b * T_SUB + t).ravel()] = 1.0
    return perm


def _ssm(u, params, d, wglu, bglu):
    bsz, seq, _ = u.shape
    subs = T_SCAN // T_SUB
    rows = T_SUB * bsz
    half = rows // 2
    nblk = seq // T_SCAN
    assert bsz % PERM_BATCH == 0
    perm = _row_permutation()
    full = lambda shape: pl.BlockSpec(shape, lambda f: (0,) * len(shape))
    g_scratch = pltpu.VMEM((subs, N_SLABS, half, 2 * SLAB_STATE), F32)
    ut_scratch = pltpu.VMEM((subs, rows, D_SSM), BF16)
    return pl.pallas_call(
        functools.partial(_ssm_kernel, bsz=bsz, n_blocks=nblk),
        grid=(nblk + 1,),
        in_specs=[
            pl.BlockSpec((bsz, T_SCAN, D_SSM), lambda f: (0, jnp.minimum(f, nblk - 1), 0)),
            full(perm.shape), full(perm.shape),
            full((2, N_SLABS, 2 * SLAB_CH, SLAB_STATE)), full((2, 1, N_STATE)),
            full((2, N_SLABS, SLAB_STATE, 2 * SLAB_CH)), full((N_SLABS, SLAB_CH, SLAB_CH)),
            full((1, D_SSM)),
            pl.BlockSpec((D_SSM, D_SSM), lambda f: (0, 0), pipeline_mode=pl.Buffered(1)),
            full((1, D_SSM)),
        ],
        out_specs=pl.BlockSpec((bsz, T_SCAN, D_SSM), lambda f: (0, jnp.maximum(f - 1, 0), 0)),
        out_shape=jax.ShapeDtypeStruct((bsz, seq, D_SSM), BF16),
        scratch_shapes=[
            pltpu.VMEM((D_SSM, D_SSM), BF16),
            pltpu.VMEM((bsz, N_STATE), F32),
            pltpu.VMEM((bsz, N_STATE), F32),
            pltpu.VMEM((subs, N_SLABS, half, 2 * SLAB_STATE), BF16),
            pltpu.VMEM((subs, rows, D_SSM), F32),
            g_scratch, ut_scratch, g_scratch, ut_scratch,
        ],
        compiler_params=pltpu.CompilerParams(
            dimension_semantics=("arbitrary",), vmem_limit_bytes=VMEM_S5),
        name="s5_scan",
    )(u, jnp.asarray(perm, BF16), jnp.asarray(perm.T, BF16), *params, d, wglu, bglu)


def _attention_pieces(first, sink_ref, q_ref, k_ref, v_ref, kp_ref, vp_ref, za_ref, bias_ref,
                      half_ref, a_ref):
    low_q = lax.broadcasted_iota(jnp.int32, (WINDOW, LANES), 1) < HEAD_DIM
    kv_cache = {}

    def swap_halves(t):
        packed = pltpu.bitcast(t, jnp.uint32)
        return pltpu.bitcast(pltpu.roll(packed, HEAD_DIM, axis=1), BF16)

    def keys_values(jj, g):
        if jj not in kv_cache:
            rows = slice(jj * WINDOW, (jj + 1) * WINDOW)
            if jj == 0:
                kp, vp = kp_ref[...], vp_ref[...]
            else:
                prev = slice((jj - 1) * WINDOW, jj * WINDOW)
                kp, vp = k_ref[prev, :], v_ref[prev, :]
            kk = jnp.concatenate([kp, k_ref[rows, :]], axis=0)
            vv = jnp.concatenate([vp, v_ref[rows, :]], axis=0)
            kv_cache[jj] = ((kk, swap_halves(kk)), (vv, swap_halves(vv)))
        if (jj, g) not in kv_cache:
            keep_even, keep_odd = half_ref[0], half_ref[1]
            k2, v2 = kv_cache[jj]
            kcat = jnp.concatenate([k2[g] * keep_even, k2[1 - g] * keep_odd], axis=0)
            vcat = jnp.concatenate(
                [jnp.concatenate([v2[g] * keep_even, keep_even], axis=1),
                 jnp.concatenate([v2[1 - g] * keep_odd, keep_odd], axis=1)], axis=0)
            kv_cache[(jj, g)] = (kcat, vcat)
        return kv_cache[(jj, g)]

    items = [(jj, pair) for jj in range(TOK_OUT // WINDOW) for pair in range(N_HEADS // 2)]
    scores, probs = {}, {}

    def where(i):
        jj, pair = items[i]
        return (jj, pair, slice(jj * WINDOW, (jj + 1) * WINDOW),
                slice(pair * LANES, (pair + 1) * LANES))

    def stage_scores(i):
        jj, pair, rows, lanes = where(i)
        kcat, _ = keys_values(jj, pair // (Q_PER_KV // 2))
        s = lax.dot_general(q_ref[rows, lanes], kcat, (((1,), (1,)), ((), ())),
                            preferred_element_type=F32)
        scores[i] = s + (bias_ref[pair + (N_HEADS // 2) * first] if jj == 0 else bias_ref[pair])

    def stage_softmax(i):
        _, pair, _, _ = where(i)
        s = scores.pop(i)
        sink_e, sink_o = sink_ref[2 * pair], sink_ref[2 * pair + 1]
        m_e = jnp.maximum(jnp.max(s[:, :2 * WINDOW], axis=-1, keepdims=True), sink_e)
        m_o = jnp.maximum(jnp.max(s[:, 2 * WINDOW:], axis=-1, keepdims=True), sink_o)
        e = jnp.concatenate([jnp.exp(s[:, :2 * WINDOW] - m_e),
                             jnp.exp(s[:, 2 * WINDOW:] - m_o)], axis=1).astype(BF16)
        probs[i] = (e, jnp.where(low_q, jnp.exp(sink_e - m_e), jnp.exp(sink_o - m_o)))

    def stage_output(i):
        jj, pair, rows, lanes = where(i)
        _, vcat = keys_values(jj, pair // (Q_PER_KV // 2))
        e, sink_term = probs.pop(i)
        o = jnp.dot(e, vcat, preferred_element_type=F32)
        den = o[:, LANES:] + sink_term
        a_ref[rows, lanes] = (o[:, :LANES] / den * za_ref[rows, lanes].astype(F32)).astype(BF16)

    def slot(it):
        if it < len(items):
            stage_scores(it)
        if 0 <= it - 1 < len(items):
            stage_softmax(it - 1)
        if 0 <= it - 2 < len(items):
            stage_output(it - 2)

    return [functools.partial(slot, it) for it in range(len(items) + 2)]


def _attn_out_kernel(sink_ref, x_ref, p_ref, ssm_ref, zs_ref, q_ref, k_ref, v_ref, kp_ref, vp_ref,
                     za_ref, bias_ref, half_ref, wout_f32, pg_ref, wgate_f32, bgate_ref, wproj_f32,
                     o_ref, wout_ref, wgate_ref, wproj_ref, a0_ref, a1_ref,
                     *, blocks_per_seq, n_blocks):
    step = pl.program_id(0)
    seq_block = jnp.minimum(step, n_blocks - 1) % blocks_per_seq
    first = (seq_block == 0).astype(jnp.int32)

    @pl.when(step == 0)
    def _():
        wout_ref[...] = wout_f32[...].astype(BF16)
        wgate_ref[...] = wgate_f32[...].astype(BF16)
        wproj_ref[...] = wproj_f32[...].astype(BF16)

    def body(a_w, a_r, front, back):
        pieces = []
        if front:
            pieces = _attention_pieces(first, sink_ref, q_ref, k_ref, v_ref, kp_ref, vp_ref,
                                       za_ref, bias_ref, half_ref, a_w)
        n_chunks = D_MODEL // OUT_COLS
        per_chunk = max(len(pieces) - 2, 0) // (2 * n_chunks)
        pieces = iter(pieces)

        def run_pieces(count=per_chunk):
            for piece in (list(pieces) if count is None else [p for _, p in zip(range(count), pieces)]):
                piece()

        if not back:
            run_pieces(None)
            return
        run_pieces(2)

        s5 = (ssm_ref[...].astype(F32) * zs_ref[...].astype(F32)).astype(BF16)
        a_prev = a_r[...]
        mixed = []
        for n in range(n_chunks):
            cols = slice(n * OUT_COLS, (n + 1) * OUT_COLS)
            mixed.append(jnp.dot(s5, wout_ref[:D_SSM, cols], preferred_element_type=F32)
                         + jnp.dot(a_prev, wout_ref[D_SSM:, cols], preferred_element_type=F32))
            run_pieces()
        mixed = jnp.concatenate(mixed, axis=1)
        ms = jnp.mean(mixed * mixed, axis=-1, keepdims=True)
        h1 = x_ref[...] + mixed * lax.rsqrt(ms + EPS) * pg_ref[...]
        h1b = h1.astype(BF16)
        pb = p_ref[...].astype(BF16)
        for n in range(n_chunks):
            cols = slice(n * OUT_COLS, (n + 1) * OUT_COLS)
            gate = jax.nn.sigmoid(
                jnp.dot(h1b, wgate_ref[:, cols], preferred_element_type=F32) + bgate_ref[:, cols])
            ple = jnp.dot(pb, wproj_ref[:, cols], preferred_element_type=F32)
            o_ref[:, cols] = h1[:, cols] + gate * ple
            run_pieces()
        run_pieces(None)

    _pipelined(step, n_blocks, body, a0_ref, a1_ref)


def _attn_bias_table():
    q_idx = np.arange(WINDOW)[:, None]
    s_idx = np.arange(2 * WINDOW)[None, :]
    dist = q_idx + WINDOW - s_idx
    valid = (dist >= 0) & (dist < WINDOW)
    slopes = np.exp2(-8.0 * (np.arange(N_HEADS, dtype=np.float32) + 1.0) / N_HEADS).astype(np.float32)
    bias = -slopes[:, None, None] * dist.astype(np.float32)[None]
    normal = np.where(valid[None], bias, -np.inf)
    first = np.where((valid & (s_idx >= WINDOW))[None], bias, -np.inf)
    paired = lambda t: t.reshape(N_HEADS // 2, 2, WINDOW, 2 * WINDOW).transpose(0, 2, 1, 3).reshape(
        N_HEADS // 2, WINDOW, 4 * WINDOW)
    return np.concatenate([paired(normal), paired(first)], axis=0).astype(np.float32)


def _attn_out(sinks, x, p, ssm, zs, q, k, v, za, wout, pg, wgate, bgate, wproj):
    bsz, seq, _ = x.shape
    sub = TOK_OUT // WINDOW
    per_seq = seq // TOK_OUT
    n_blocks = bsz * per_seq

    def cur(f):
        f = jnp.minimum(f, n_blocks - 1)
        return f // per_seq, f % per_seq

    def prv(f):
        f = jnp.maximum(f - 1, 0)
        return f // per_seq, f % per_seq

    front = lambda width: pl.BlockSpec((None, TOK_OUT, width), lambda f: (*cur(f), 0))
    back = lambda width: pl.BlockSpec((None, TOK_OUT, width), lambda f: (*prv(f), 0))
    prev_kv = pl.BlockSpec(
        (None, WINDOW, D_KV),
        lambda f: (cur(f)[0], jnp.maximum(cur(f)[1] * sub - 1, 0), 0))
    full = lambda shape: pl.BlockSpec(shape, lambda f: (0,) * len(shape))
    once = lambda shape: pl.BlockSpec(shape, lambda f: (0,) * len(shape),
                                      pipeline_mode=pl.Buffered(1))
    bias = jnp.asarray(_attn_bias_table())
    lane_is_even = np.broadcast_to(np.arange(LANES) < HEAD_DIM, (2 * WINDOW, LANES))
    half = jnp.asarray(np.stack([lane_is_even, ~lane_is_even]), BF16)
    a_scratch = pltpu.VMEM((TOK_OUT, D_ATTN), BF16)
    return pl.pallas_call(
        functools.partial(_attn_out_kernel, blocks_per_seq=per_seq, n_blocks=n_blocks),
        grid=(n_blocks + 1,),
        in_specs=[
            pl.BlockSpec(memory_space=pltpu.SMEM),
            back(D_MODEL), back(D_PLE), back(D_SSM), back(D_SSM), front(D_ATTN),
            front(D_KV), front(D_KV), prev_kv, prev_kv, front(D_ATTN),
            once(bias.shape), once(half.shape),
            once((D_MODEL, D_MODEL)), full((1, D_MODEL)),
            once((D_MODEL, D_MODEL)), full((1, D_MODEL)), once((D_PLE, D_MODEL)),
        ],
        out_specs=back(D_MODEL),
        out_shape=jax.ShapeDtypeStruct((bsz, seq, D_MODEL), F32),
        scratch_shapes=[pltpu.VMEM((D_MODEL, D_MODEL), BF16), pltpu.VMEM((D_MODEL, D_MODEL), BF16),
                        pltpu.VMEM((D_PLE, D_MODEL), BF16), a_scratch, a_scratch],
        compiler_params=pltpu.CompilerParams(
            dimension_semantics=("arbitrary",), vmem_limit_bytes=VMEM_ATTN_OUT),
        name="attn_out",
    )(sinks, x, p, ssm, zs, q, k, v, k, v, za, bias, half, wout, pg, wgate, bgate, wproj)


def _ssm_params(lam_re, lam_im, log_step, b_re, b_im, c_re, c_im):
    lr, li = lam_re.astype(F32), lam_im.astype(F32)
    step = jnp.exp(log_step.astype(F32))[:, None]
    mag = jnp.exp(lr * step)
    lbr, lbi = mag * jnp.cos(li * step), mag * jnp.sin(li * step)
    den = lr * lr + li * li
    fr = ((lbr - 1.0) * lr + lbi * li) / den
    fi = (lbi * lr - (lbr - 1.0) * li) / den
    br, bi = b_re.astype(F32), b_im.astype(F32)
    bbr = fr[..., None] * br - fi[..., None] * bi
    bbi = fr[..., None] * bi + fi[..., None] * br
    cr, ci = c_re.astype(F32), c_im.astype(F32)
    l2 = jnp.stack([lbr * lbr - lbi * lbi, 2.0 * lbr * lbi])
    lbbr = lbr[..., None] * bbr - lbi[..., None] * bbi
    lbbi = lbr[..., None] * bbi + lbi[..., None] * bbr
    clr = cr * lbr[:, None, :] - ci * lbi[:, None, :]
    cli = cr * lbi[:, None, :] + ci * lbr[:, None, :]
    cb = jnp.sum(cr[:, :, :, None] * bbr[:, None, :, :] - ci[:, :, :, None] * bbi[:, None, :, :],
                 axis=2)
    gps = SLAB_CH // SSM_GROUP_CH
    n, p = SSM_STATE, SSM_GROUP_CH
    same_group = (np.arange(SLAB_CH)[:, None] // p == np.arange(SLAB_STATE)[None, :] // n)
    expand_p = np.tile(np.eye(p, dtype=np.float32), (gps, 1))
    expand_n = np.tile(np.eye(n, dtype=np.float32), (gps, 1))

    w_in = jnp.stack([jnp.stack([lbbr, bbr]), jnp.stack([lbbi, bbi])])
    w_in = w_in.reshape(2, 2, N_SLABS, SLAB_STATE, p)
    w_in = jnp.einsum('rp,absqp->asbrq', expand_p, w_in) * same_group.astype(np.float32)
    w_in = w_in.astype(BF16).reshape(2, N_SLABS, 2 * SLAB_CH, SLAB_STATE)
    w_out = jnp.stack([jnp.stack([cr, clr]), jnp.stack([-ci, -cli])])
    w_out = w_out.reshape(2, 2, N_SLABS, SLAB_CH, n)
    w_out = jnp.einsum('rn,aesqn->aesrq', expand_n, w_out) * same_group.T.astype(np.float32)
    w_out = jnp.concatenate([w_out[:, 0], w_out[:, 1]], axis=-1).astype(BF16)
    same_group_pp = same_group[:, ::n // p]
    w_dir = jnp.einsum('rq,sxq->srx', expand_p, cb.reshape(N_SLABS, SLAB_CH, p))
    w_dir = (w_dir * same_group_pp.astype(np.float32)).astype(BF16)
    return w_in, l2.reshape(2, 1, N_STATE), w_out, w_dir


def _layer(h, p, pre_norm_g, w_in, ssm_lam_re, ssm_lam_im, ssm_log_step, ssm_b_re, ssm_b_im,
           ssm_c_re, ssm_c_im, ssm_d, ssm_w_glu, ssm_b_glu, attn_sinks, w_out, post_norm_g,
           pl_w_proj, pl_w_gate, pl_b_gate):
    row = lambda a: a.astype(F32).reshape(1, -1)
    u, zs, q, k, v, za = _in_proj(h, row(pre_norm_g), w_in.astype(F32))
    params = _ssm_params(
        ssm_lam_re, ssm_lam_im, ssm_log_step, ssm_b_re, ssm_b_im, ssm_c_re, ssm_c_im)
    ssm = _ssm(u, params, row(ssm_d), ssm_w_glu.astype(F32), row(ssm_b_glu))
    return _attn_out(attn_sinks.astype(F32), h, p, ssm, zs, q, k, v, za,
                     w_out.astype(F32), row(post_norm_g), pl_w_gate.astype(F32), row(pl_b_gate),
                     pl_w_proj.astype(F32))


def kernel(x, p, pre_norm_g, w_in, ssm_lam_re, ssm_lam_im, ssm_log_step, ssm_b_re, ssm_b_im,
           ssm_c_re, ssm_c_im, ssm_d, ssm_w_glu, ssm_b_glu, attn_sinks, w_out, post_norm_g,
           pl_w_proj, pl_w_gate, pl_b_gate):
    h = x
    for i in range(p.shape[0]):
        h = _layer(h, p[i], pre_norm_g[i], w_in[i], ssm_lam_re[i], ssm_lam_im[i], ssm_log_step[i],
                   ssm_b_re[i], ssm_b_im[i], ssm_c_re[i], ssm_c_im[i], ssm_d[i], ssm_w_glu[i],
                   ssm_b_glu[i], attn_sinks[i], w_out[i], post_norm_g[i], pl_w_proj[i],
                   pl_w_gate[i], pl_b_gate[i])
    return h
```

```python
import functools
import math

import numpy as np
import jax
import jax.numpy as jnp
from jax import lax
from jax.experimental import pallas as pl
from jax.experimental.pallas import tpu as pltpu

F32 = jnp.float32
BF16 = jnp.bfloat16

D_MODEL = 1024
D_SSM = 512
D_ATTN = 512
SSM_GROUP_CH = 16
SSM_GROUPS = D_SSM // SSM_GROUP_CH
SSM_STATE = 64
N_STATE = SSM_GROUPS * SSM_STATE
HEAD_DIM = 64
N_HEADS = D_ATTN // HEAD_DIM
KV_HEADS = 2
Q_PER_KV = N_HEADS // KV_HEADS
D_KV = KV_HEADS * HEAD_DIM
WINDOW = 128
D_PLE = 256
EPS = 1e-6

LANES = 128
assert D_KV == LANES and 2 * HEAD_DIM == LANES

OFF_U, OFF_ZS, OFF_Q, OFF_K, OFF_V, OFF_ZA, D_IN = 0, 512, 1024, 1536, 1664, 1792, 2304

SLAB_CH = LANES
N_SLABS = D_SSM // SLAB_CH
SLAB_STATE = (SLAB_CH // SSM_GROUP_CH) * SSM_STATE
SCAN_COLS = 256

TOK_IN = 2048
NORM_ROWS = 256
T_SCAN = 64
T_SUB = 32
PIECE_ROWS = 128
PERM_BATCH = 8
TOK_OUT = 1024
PROJ_COLS = 256
OUT_COLS = 256
VMEM_LIMIT = 63 * 1024 * 1024


def _silu(z):
    return z * jax.nn.sigmoid(z)


def _pipelined(step, n_blocks, body, even_bufs, odd_bufs):
    @pl.when(step == 0)
    def _():
        body(even_bufs, odd_bufs, True, False)

    @pl.when((step > 0) & (step < n_blocks) & (step % 2 == 0))
    def _():
        body(even_bufs, odd_bufs, True, True)

    @pl.when((step < n_blocks) & (step % 2 == 1))
    def _():
        body(odd_bufs, even_bufs, True, True)

    @pl.when(step == n_blocks)
    def _():
        last_read = odd_bufs if n_blocks % 2 == 0 else even_bufs
        body(None, last_read, False, True)


def _in_proj_kernel(x_ref, g_ref, w_ref, u_ref, zs_ref, q_ref, k_ref, v_ref, za_ref, w_scr):
    @pl.when((pl.program_id(0) == 0) & (pl.program_id(1) == 0))
    def _():
        w_scr[...] = w_ref[...].astype(BF16)

    def emit(rows, tile, acc):
        lo = tile * PROJ_COLS
        if lo < OFF_ZS:
            u_ref[rows, lo - OFF_U:lo - OFF_U + PROJ_COLS] = acc.astype(BF16)
        elif lo < OFF_Q:
            zs_ref[rows, lo - OFF_ZS:lo - OFF_ZS + PROJ_COLS] = _silu(acc).astype(BF16)
        elif lo < OFF_K:
            q_ref[rows, lo - OFF_Q:lo - OFF_Q + PROJ_COLS] = (
                acc * (1.0 / math.sqrt(HEAD_DIM))).astype(BF16)
        elif lo < OFF_ZA:
            k_ref[rows, :] = acc[:, :D_KV].astype(BF16)
            v_ref[rows, :] = acc[:, D_KV:].astype(BF16)
        else:
            za_ref[rows, lo - OFF_ZA:lo - OFF_ZA + PROJ_COLS] = _silu(acc).astype(BF16)

    gain = g_ref[...]
    for piece in range(TOK_IN // NORM_ROWS):
        rows = slice(piece * NORM_ROWS, (piece + 1) * NORM_ROWS)
        x = x_ref[rows, :]
        ms = jnp.mean(x * x, axis=-1, keepdims=True)
        hn = (x * lax.rsqrt(ms + EPS) * gain).astype(BF16)
        for tile in range(D_IN // PROJ_COLS):
            cols = slice(tile * PROJ_COLS, (tile + 1) * PROJ_COLS)
            emit(rows, tile, jnp.dot(hn, w_scr[:, cols], preferred_element_type=F32))


def _in_proj(x, g, w_in):
    bsz, seq, _ = x.shape
    tok = lambda width: pl.BlockSpec((None, TOK_IN, width), lambda b, i: (b, i, 0))
    widths = (D_SSM, D_SSM, D_ATTN, D_KV, D_KV, D_ATTN)
    return pl.pallas_call(
        _in_proj_kernel,
        grid=(bsz, seq // TOK_IN),
        in_specs=[
            tok(D_MODEL),
            pl.BlockSpec((1, D_MODEL), lambda b, i: (0, 0)),
            pl.BlockSpec((D_MODEL, D_IN), lambda b, i: (0, 0), pipeline_mode=pl.Buffered(1)),
        ],
        out_specs=tuple(tok(w) for w in widths),
        out_shape=tuple(jax.ShapeDtypeStruct((bsz, seq, w), BF16) for w in widths),
        scratch_shapes=[pltpu.VMEM((D_MODEL, D_IN), BF16)],
        compiler_params=pltpu.CompilerParams(
            dimension_semantics=("arbitrary", "arbitrary"), vmem_limit_bytes=VMEM_LIMIT),
        name="in_proj",
    )(x, g, w_in)


def _ssm_kernel(u_ref, perm_ref, permt_ref, win_ref, l2_ref, wout_ref, wdir_ref, d_ref,
                wglu_f32, bglu_ref, o_ref,
                wglu_ref, sre_ref, sim_ref, st_ref, y_ref,
                g0_ref, ut0_ref, g1_ref, ut1_ref, *, bsz, n_blocks):
    subs = T_SCAN // T_SUB
    rows = bsz * T_SUB
    half = rows // 2
    step = pl.program_id(0)

    @pl.when(step == 0)
    def _():
        wglu_ref[...] = wglu_f32[...].astype(BF16)
        sre_ref[...] = jnp.zeros_like(sre_ref)
        sim_ref[...] = jnp.zeros_like(sim_ref)

    def body(write, read, front, back):
        g_w, ut_w = write if front else (None, None)
        g_r, ut_r = read if back else (None, None)
        groups = bsz // PERM_BATCH
        grows = PERM_BATCH * T_SUB
        u_new, u_before = [], []
        for sb in range(subs if front else 0):
            steps = slice(sb * T_SUB, (sb + 1) * T_SUB)
            permuted = [
                jnp.dot(perm_ref[...],
                        u_ref[g * PERM_BATCH:(g + 1) * PERM_BATCH, steps].reshape(grows, D_SSM),
                        preferred_element_type=F32).reshape(T_SUB, 1, PERM_BATCH, D_SSM)
                for g in range(groups)]
            u_new.append(jnp.concatenate(permuted, axis=1).reshape(rows, D_SSM).astype(BF16))
            ut_w[sb] = u_new[sb]
            if sb > 0:
                u_last = u_new[sb - 1][rows - bsz:]
            else:
                u_last = ut_r[subs - 1, rows - bsz:, :] if back else jnp.zeros((bsz, D_SSM), BF16)
            u_before.append(jnp.concatenate([u_last, u_new[sb][half:rows - bsz]], axis=0))
        for j in range(N_SLABS):
            ch = slice(j * SLAB_CH, (j + 1) * SLAB_CH)
            drive_in = [jnp.concatenate([u_before[sb][:, ch], u_new[sb][:half, ch]], axis=1)
                        for sb in range(subs if front else 0)]
            for c in range(SLAB_STATE // SCAN_COLS):
                lo = j * SLAB_STATE + c * SCAN_COLS
                cre = slice(c * SCAN_COLS, (c + 1) * SCAN_COLS)
                cim = slice(SLAB_STATE + c * SCAN_COLS, SLAB_STATE + (c + 1) * SCAN_COLS)
                if back:
                    l2r = jnp.broadcast_to(l2_ref[0, :, lo:lo + SCAN_COLS], (bsz, SCAN_COLS))
                    l2i = jnp.broadcast_to(l2_ref[1, :, lo:lo + SCAN_COLS], (bsz, SCAN_COLS))
                    sr = sre_ref[:, lo:lo + SCAN_COLS]
                    si = sim_ref[:, lo:lo + SCAN_COLS]
                for sb, piece in [(sb, piece) for sb in range(subs)
                                  for piece in range(half // PIECE_ROWS)]:
                    rk = slice(piece * PIECE_ROWS, (piece + 1) * PIECE_ROWS)
                    if front:
                        g_w[sb, j, rk, cre] = jnp.dot(drive_in[sb][rk], win_ref[0, j, :, cre],
                                                      preferred_element_type=F32)
                        g_w[sb, j, rk, cim] = jnp.dot(drive_in[sb][rk], win_ref[1, j, :, cre],
                                                      preferred_element_type=F32)
                    if not back:
                        continue
                    for k in range(rk.start // bsz, rk.stop // bsz):
                        r = slice(k * bsz, (k + 1) * bsz)
                        sr, si = (l2r * sr - l2i * si + g_r[sb, j, r, cre],
                                  l2r * si + l2i * sr + g_r[sb, j, r, cim])
                        st_ref[sb, j, r, cre] = sr.astype(BF16)
                        st_ref[sb, j, r, cim] = si.astype(BF16)
                    yk = (jnp.dot(st_ref[sb, j, rk, cre], wout_ref[0, j, cre, :],
                                  preferred_element_type=F32)
                          + jnp.dot(st_ref[sb, j, rk, cim], wout_ref[1, j, cre, :],
                                    preferred_element_type=F32))
                    odd_rows = slice(half + rk.start, half + rk.stop)
                    if c == 0:
                        y_ref[sb, rk, ch] = yk[:, :SLAB_CH]
                        y_ref[sb, odd_rows, ch] = yk[:, SLAB_CH:] + jnp.dot(
                            ut_r[sb, odd_rows, ch], wdir_ref[j], preferred_element_type=F32)
                    else:
                        y_ref[sb, rk, ch] += yk[:, :SLAB_CH]
                        y_ref[sb, odd_rows, ch] += yk[:, SLAB_CH:]
                if back:
                    sre_ref[:, lo:lo + SCAN_COLS] = sr
                    sim_ref[:, lo:lo + SCAN_COLS] = si
        if not back:
            return

        for sb in range(subs):
            y = y_ref[sb] + d_ref[...] * ut_r[sb].astype(F32)
            g = jax.nn.gelu(y)
            gate = jnp.dot(g.astype(BF16), wglu_ref[...], preferred_element_type=F32) + bglu_ref[...]
            out_tb = (g * jax.nn.sigmoid(gate)).reshape(T_SUB, groups, PERM_BATCH, D_SSM)
            for grp in range(groups):
                out_g = out_tb[:, grp].reshape(grows, D_SSM).astype(BF16)
                out_bt = jnp.dot(permt_ref[...], out_g, preferred_element_type=F32).astype(BF16)
                o_ref[grp * PERM_BATCH:(grp + 1) * PERM_BATCH, sb * T_SUB:(sb + 1) * T_SUB] = (
                    out_bt.reshape(PERM_BATCH, T_SUB, D_SSM))

    _pipelined(step, n_blocks, body, (g0_ref, ut0_ref), (g1_ref, ut1_ref))


def _row_permutation():
    rows = PERM_BATCH * T_SUB
    perm = np.zeros((rows, rows), np.float32)
    t, b = np.meshgrid(np.arange(T_SUB), np.arange(PERM_BATCH), indexing="ij")
    dst = (t % 2) * (rows // 2) + (t // 2) * PERM_BATCH + b
    perm[dst.ravel(), (b * T_SUB + t).ravel()] = 1.0
    return perm


def _ssm(u, params, d, wglu, bglu):
    bsz, seq, _ = u.shape
    subs = T_SCAN // T_SUB
    rows = T_SUB * bsz
    half = rows // 2
    nblk = seq // T_SCAN
    assert bsz % PERM_BATCH == 0
    perm = _row_permutation()
    full = lambda shape: pl.BlockSpec(shape, lambda f: (0,) * len(shape))
    g_scratch = pltpu.VMEM((subs, N_SLABS, half, 2 * SLAB_STATE), F32)
    ut_scratch = pltpu.VMEM((subs, rows, D_SSM), BF16)
    return pl.pallas_call(
        functools.partial(_ssm_kernel, bsz=bsz, n_blocks=nblk),
        grid=(nblk + 1,),
        in_specs=[
            pl.BlockSpec((bsz, T_SCAN, D_SSM), lambda f: (0, jnp.minimum(f, nblk - 1), 0)),
            full(perm.shape), full(perm.shape),
            full((2, N_SLABS, 2 * SLAB_CH, SLAB_STATE)), full((2, 1, N_STATE)),
            full((2, N_SLABS, SLAB_STATE, 2 * SLAB_CH)), full((N_SLABS, SLAB_CH, SLAB_CH)),
            full((1, D_SSM)),
            pl.BlockSpec((D_SSM, D_SSM), lambda f: (0, 0), pipeline_mode=pl.Buffered(1)),
            full((1, D_SSM)),
        ],
        out_specs=pl.BlockSpec((bsz, T_SCAN, D_SSM), lambda f: (0, jnp.maximum(f - 1, 0), 0)),
        out_shape=jax.ShapeDtypeStruct((bsz, seq, D_SSM), BF16),
        scratch_shapes=[
            pltpu.VMEM((D_SSM, D_SSM), BF16),
            pltpu.VMEM((bsz, N_STATE), F32),
            pltpu.VMEM((bsz, N_STATE), F32),
            pltpu.VMEM((subs, N_SLABS, half, 2 * SLAB_STATE), BF16),
            pltpu.VMEM((subs, rows, D_SSM), F32),
            g_scratch, ut_scratch, g_scratch, ut_scratch,
        ],
        compiler_params=pltpu.CompilerParams(
            dimension_semantics=("arbitrary",), vmem_limit_bytes=VMEM_LIMIT),
        name="s5_scan",
    )(u, jnp.asarray(perm, BF16), jnp.asarray(perm.T, BF16), *params, d, wglu, bglu)


def _attention_pieces(first, sink_ref, q_ref, k_ref, v_ref, kp_ref, vp_ref, za_ref, bias_ref,
                      half_ref, a_ref):
    low_q = lax.broadcasted_iota(jnp.int32, (WINDOW, LANES), 1) < HEAD_DIM
    kv_cache = {}

    def swap_halves(t):
        packed = pltpu.bitcast(t, jnp.uint32)
        return pltpu.bitcast(pltpu.roll(packed, HEAD_DIM, axis=1), BF16)

    def keys_values(jj, g):
        if jj not in kv_cache:
            rows = slice(jj * WINDOW, (jj + 1) * WINDOW)
            if jj == 0:
                kp, vp = kp_ref[...], vp_ref[...]
            else:
                prev = slice((jj - 1) * WINDOW, jj * WINDOW)
                kp, vp = k_ref[prev, :], v_ref[prev, :]
            kk = jnp.concatenate([kp, k_ref[rows, :]], axis=0)
            vv = jnp.concatenate([vp, v_ref[rows, :]], axis=0)
            kv_cache[jj] = ((kk, swap_halves(kk)), (vv, swap_halves(vv)))
        if (jj, g) not in kv_cache:
            keep_even, keep_odd = half_ref[0], half_ref[1]
            k2, v2 = kv_cache[jj]
            kcat = jnp.concatenate([k2[g] * keep_even, k2[1 - g] * keep_odd], axis=0)
            vcat = jnp.concatenate(
                [jnp.concatenate([v2[g] * keep_even, keep_even], axis=1),
                 jnp.concatenate([v2[1 - g] * keep_odd, keep_odd], axis=1)], axis=0)
            kv_cache[(jj, g)] = (kcat, vcat)
        return kv_cache[(jj, g)]

    items = [(jj, pair) for jj in range(TOK_OUT // WINDOW) for pair in range(N_HEADS // 2)]
    scores, probs = {}, {}

    def where(i):
        jj, pair = items[i]
        return (jj, pair, slice(jj * WINDOW, (jj + 1) * WINDOW),
                slice(pair * LANES, (pair + 1) * LANES))

    def stage_scores(i):
        jj, pair, rows, lanes = where(i)
        kcat, _ = keys_values(jj, pair // (Q_PER_KV // 2))
        s = lax.dot_general(q_ref[rows, lanes], kcat, (((1,), (1,)), ((), ())),
                            preferred_element_type=F32)
        scores[i] = s + (bias_ref[pair + (N_HEADS // 2) * first] if jj == 0 else bias_ref[pair])

    def stage_softmax(i):
        _, pair, _, _ = where(i)
        s = scores.pop(i)
        sink_e, sink_o = sink_ref[2 * pair], sink_ref[2 * pair + 1]
        m_e = jnp.maximum(jnp.max(s[:, :2 * WINDOW], axis=-1, keepdims=True), sink_e)
        m_o = jnp.maximum(jnp.max(s[:, 2 * WINDOW:], axis=-1, keepdims=True), sink_o)
        e = jnp.concatenate([jnp.exp(s[:, :2 * WINDOW] - m_e),
                             jnp.exp(s[:, 2 * WINDOW:] - m_o)], axis=1).astype(BF16)
        probs[i] = (e, jnp.where(low_q, jnp.exp(sink_e - m_e), jnp.exp(sink_o - m_o)))

    def stage_output(i):
        jj, pair, rows, lanes = where(i)
        _, vcat = keys_values(jj, pair // (Q_PER_KV // 2))
        e, sink_term = probs.pop(i)
        o = jnp.dot(e, vcat, preferred_element_type=F32)
        den = o[:, LANES:] + sink_term
        a_ref[rows, lanes] = (o[:, :LANES] / den * za_ref[rows, lanes].astype(F32)).astype(BF16)

    def slot(it):
        if it < len(items):
            stage_scores(it)
        if 0 <= it - 1 < len(items):
            stage_softmax(it - 1)
        if 0 <= it - 2 < len(items):
            stage_output(it - 2)

    return [functools.partial(slot, it) for it in range(len(items) + 2)]


def _attn_out_kernel(sink_ref, x_ref, p_ref, ssm_ref, zs_ref, q_ref, k_ref, v_ref, kp_ref, vp_ref,
                     za_ref, bias_ref, half_ref, wout_f32, pg_ref, wgate_f32, bgate_ref, wproj_f32,
                     o_ref, wout_ref, wgate_ref, wproj_ref, a0_ref, a1_ref,
                     *, blocks_per_seq, n_blocks):
    step = pl.program_id(0)
    seq_block = jnp.minimum(step, n_blocks - 1) % blocks_per_seq
    first = (seq_block == 0).astype(jnp.int32)

    @pl.when(step == 0)
    def _():
        wout_ref[...] = wout_f32[...].astype(BF16)
        wgate_ref[...] = wgate_f32[...].astype(BF16)
        wproj_ref[...] = wproj_f32[...].astype(BF16)

    def body(a_w, a_r, front, back):
        pieces = []
        if front:
            pieces = _attention_pieces(first, sink_ref, q_ref, k_ref, v_ref, kp_ref, vp_ref,
                                       za_ref, bias_ref, half_ref, a_w)
        n_chunks = D_MODEL // OUT_COLS
        per_chunk = max(len(pieces) - 2, 0) // (2 * n_chunks)
        pieces = iter(pieces)

        def run_pieces(count=per_chunk):
            for piece in (list(pieces) if count is None else [p for _, p in zip(range(count), pieces)]):
                piece()

        if not back:
            run_pieces(None)
            return
        run_pieces(2)

        s5 = (ssm_ref[...].astype(F32) * zs_ref[...].astype(F32)).astype(BF16)
        a_prev = a_r[...]
        mixed = []
        for n in range(n_chunks):
            cols = slice(n * OUT_COLS, (n + 1) * OUT_COLS)
            mixed.append(jnp.dot(s5, wout_ref[:D_SSM, cols], preferred_element_type=F32)
                         + jnp.dot(a_prev, wout_ref[D_SSM:, cols], preferred_element_type=F32))
            run_pieces()
        mixed = jnp.concatenate(mixed, axis=1)
        ms = jnp.mean(mixed * mixed, axis=-1, keepdims=True)
        h1 = x_ref[...] + mixed * lax.rsqrt(ms + EPS) * pg_ref[...]
        h1b = h1.astype(BF16)
        pb = p_ref[...].astype(BF16)
        for n in range(n_chunks):
            cols = slice(n * OUT_COLS, (n + 1) * OUT_COLS)
            gate = jax.nn.sigmoid(
                jnp.dot(h1b, wgate_ref[:, cols], preferred_element_type=F32) + bgate_ref[:, cols])
            ple = jnp.dot(pb, wproj_ref[:, cols], preferred_element_type=F32)
            o_ref[:, cols] = h1[:, cols] + gate * ple
            run_pieces()
        run_pieces(None)

    _pipelined(step, n_blocks, body, a0_ref, a1_ref)


def _attn_bias_table():
    q_idx = np.arange(WINDOW)[:, None]
    s_idx = np.arange(2 * WINDOW)[None, :]
    dist = q_idx + WINDOW - s_idx
    valid = (dist >= 0) & (dist < WINDOW)
    slopes = np.exp2(-8.0 * (np.arange(N_HEADS, dtype=np.float32) + 1.0) / N_HEADS).astype(np.float32)
    bias = -slopes[:, None, None] * dist.astype(np.float32)[None]
    normal = np.where(valid[None], bias, -np.inf)
    first = np.where((valid & (s_idx >= WINDOW))[None], bias, -np.inf)
    paired = lambda t: t.reshape(N_HEADS // 2, 2, WINDOW, 2 * WINDOW).transpose(0, 2, 1, 3).reshape(
        N_HEADS // 2, WINDOW, 4 * WINDOW)
    return np.concatenate([paired(normal), paired(first)], axis=0).astype(np.float32)


def _attn_out(sinks, x, p, ssm, zs, q, k, v, za, wout, pg, wgate, bgate, wproj):
    bsz, seq, _ = x.shape
    sub = TOK_OUT // WINDOW
    per_seq = seq // TOK_OUT
    n_blocks = bsz * per_seq

    def cur(f):
        f = jnp.minimum(f, n_blocks - 1)
        return f // per_seq, f % per_seq

    def prv(f):
        f = jnp.maximum(f - 1, 0)
        return f // per_seq, f % per_seq

    front = lambda width: pl.BlockSpec((None, TOK_OUT, width), lambda f: (*cur(f), 0))
    back = lambda width: pl.BlockSpec((None, TOK_OUT, width), lambda f: (*prv(f), 0))
    prev_kv = pl.BlockSpec(
        (None, WINDOW, D_KV),
        lambda f: (cur(f)[0], jnp.maximum(cur(f)[1] * sub - 1, 0), 0))
    full = lambda shape: pl.BlockSpec(shape, lambda f: (0,) * len(shape))
    once = lambda shape: pl.BlockSpec(shape, lambda f: (0,) * len(shape),
                                      pipeline_mode=pl.Buffered(1))
    bias = jnp.asarray(_attn_bias_table())
    lane_is_even = np.broadcast_to(np.arange(LANES) < HEAD_DIM, (2 * WINDOW, LANES))
    half = jnp.asarray(np.stack([lane_is_even, ~lane_is_even]), BF16)
    a_scratch = pltpu.VMEM((TOK_OUT, D_ATTN), BF16)
    return pl.pallas_call(
        functools.partial(_attn_out_kernel, blocks_per_seq=per_seq, n_blocks=n_blocks),
        grid=(n_blocks + 1,),
        in_specs=[
            pl.BlockSpec(memory_space=pltpu.SMEM),
            back(D_MODEL), back(D_PLE), back(D_SSM), back(D_SSM), front(D_ATTN),
            front(D_KV), front(D_KV), prev_kv, prev_kv, front(D_ATTN),
            once(bias.shape), once(half.shape),
            once((D_MODEL, D_MODEL)), full((1, D_MODEL)),
            once((D_MODEL, D_MODEL)), full((1, D_MODEL)), once((D_PLE, D_MODEL)),
        ],
        out_specs=back(D_MODEL),
        out_shape=jax.ShapeDtypeStruct((bsz, seq, D_MODEL), F32),
        scratch_shapes=[pltpu.VMEM((D_MODEL, D_MODEL), BF16), pltpu.VMEM((D_MODEL, D_MODEL), BF16),
                        pltpu.VMEM((D_PLE, D_MODEL), BF16), a_scratch, a_scratch],
        compiler_params=pltpu.CompilerParams(
            dimension_semantics=("arbitrary",), vmem_limit_bytes=VMEM_LIMIT),
        name="attn_out",
    )(sinks, x, p, ssm, zs, q, k, v, k, v, za, bias, half, wout, pg, wgate, bgate, wproj)


def _ssm_params(lam_re, lam_im, log_step, b_re, b_im, c_re, c_im):
    lr, li = lam_re.astype(F32), lam_im.astype(F32)
    step = jnp.exp(log_step.astype(F32))[:, None]
    mag = jnp.exp(lr * step)
    lbr, lbi = mag * jnp.cos(li * step), mag * jnp.sin(li * step)
    den = lr * lr + li * li
    fr = ((lbr - 1.0) * lr + lbi * li) / den
    fi = (lbi * lr - (lbr - 1.0) * li) / den
    br, bi = b_re.astype(F32), b_im.astype(F32)
    bbr = fr[..., None] * br - fi[..., None] * bi
    bbi = fr[..., None] * bi + fi[..., None] * br
    cr, ci = c_re.astype(F32), c_im.astype(F32)
    l2 = jnp.stack([lbr * lbr - lbi * lbi, 2.0 * lbr * lbi])
    lbbr = lbr[..., None] * bbr - lbi[..., None] * bbi
    lbbi = lbr[..., None] * bbi + lbi[..., None] * bbr
    clr = cr * lbr[:, None, :] - ci * lbi[:, None, :]
    cli = cr * lbi[:, None, :] + ci * lbr[:, None, :]
    cb = jnp.sum(cr[:, :, :, None] * bbr[:, None, :, :] - ci[:, :, :, None] * bbi[:, None, :, :],
                 axis=2)
    gps = SLAB_CH // SSM_GROUP_CH
    n, p = SSM_STATE, SSM_GROUP_CH
    same_group = (np.arange(SLAB_CH)[:, None] // p == np.arange(SLAB_STATE)[None, :] // n)
    expand_p = np.tile(np.eye(p, dtype=np.float32), (gps, 1))
    expand_n = np.tile(np.eye(n, dtype=np.float32), (gps, 1))

    w_in = jnp.stack([jnp.stack([lbbr, bbr]), jnp.stack([lbbi, bbi])])
    w_in = w_in.reshape(2, 2, N_SLABS, SLAB_STATE, p)
    w_in = jnp.einsum('rp,absqp->asbrq', expand_p, w_in) * same_group.astype(np.float32)
    w_in = w_in.astype(BF16).reshape(2, N_SLABS, 2 * SLAB_CH, SLAB_STATE)
    w_out = jnp.stack([jnp.stack([cr, clr]), jnp.stack([-ci, -cli])])
    w_out = w_out.reshape(2, 2, N_SLABS, SLAB_CH, n)
    w_out = jnp.einsum('rn,aesqn->aesrq', expand_n, w_out) * same_group.T.astype(np.float32)
    w_out = jnp.concatenate([w_out[:, 0], w_out[:, 1]], axis=-1).astype(BF16)
    same_group_pp = same_group[:, ::n // p]
    w_dir = jnp.einsum('rq,sxq->srx', expand_p, cb.reshape(N_SLABS, SLAB_CH, p))
    w_dir = (w_dir * same_group_pp.astype(np.float32)).astype(BF16)
    return w_in, l2.reshape(2, 1, N_STATE), w_out, w_dir


def _layer(h, p, pre_norm_g, w_in, ssm_lam_re, ssm_lam_im, ssm_log_step, ssm_b_re, ssm_b_im,
           ssm_c_re, ssm_c_im, ssm_d, ssm_w_glu, ssm_b_glu, attn_sinks, w_out, post_norm_g,
           pl_w_proj, pl_w_gate, pl_b_gate):
    row = lambda a: a.astype(F32).reshape(1, -1)
    u, zs, q, k, v, za = _in_proj(h, row(pre_norm_g), w_in.astype(F32))
    params = _ssm_params(
        ssm_lam_re, ssm_lam_im, ssm_log_step, ssm_b_re, ssm_b_im, ssm_c_re, ssm_c_im)
    ssm = _ssm(u, params, row(ssm_d), ssm_w_glu.astype(F32), row(ssm_b_glu))
    return _attn_out(attn_sinks.astype(F32), h, p, ssm, zs, q, k, v, za,
                     w_out.astype(F32), row(post_norm_g), pl_w_gate.astype(F32), row(pl_b_gate),
                     pl_w_proj.astype(F32))


def kernel(x, p, pre_norm_g, w_in, ssm_lam_re, ssm_lam_im, ssm_log_step, ssm_b_re, ssm_b_im,
           ssm_c_re, ssm_c_im, ssm_d, ssm_w_glu, ssm_b_glu, attn_sinks, w_out, post_norm_g,
           pl_w_proj, pl_w_gate, pl_b_gate):
    h = x
    for i in range(p.shape[0]):
        h = _layer(h, p[i], pre_norm_g[i], w_in[i], ssm_lam_re[i], ssm_lam_im[i], ssm_log_step[i],
                   ssm_b_re[i], ssm_b_im[i], ssm_c_re[i], ssm_c_im[i], ssm_d[i], ssm_w_glu[i],
                   ssm_b_glu[i], attn_sinks[i], w_out[i], post_norm_g[i], pl_w_proj[i],
                   pl_w_gate[i], pl_b_gate[i])
    return h
```

```python
import functools
import math

import numpy as np
import jax
import jax.numpy as jnp
from jax import lax
from jax.experimental import pallas as pl
from jax.experimental.pallas import tpu as pltpu

F32 = jnp.float32
BF16 = jnp.bfloat16

D_MODEL = 1024
D_SSM = 512
D_ATTN = 512
SSM_GROUP_CH = 16
SSM_GROUPS = D_SSM // SSM_GROUP_CH
SSM_STATE = 64
N_STATE = SSM_GROUPS * SSM_STATE
HEAD_DIM = 64
N_HEADS = D_ATTN // HEAD_DIM
KV_HEADS = 2
Q_PER_KV = N_HEADS // KV_HEADS
D_KV = KV_HEADS * HEAD_DIM
WINDOW = 128
D_PLE = 256
EPS = 1e-6

LANES = 128
assert D_KV == LANES and 2 * HEAD_DIM == LANES

OFF_U, OFF_ZS, OFF_Q, OFF_K, OFF_V, OFF_ZA, D_IN = 0, 512, 1024, 1536, 1664, 1792, 2304

SLAB_CH = LANES
N_SLABS = D_SSM // SLAB_CH
SLAB_STATE = (SLAB_CH // SSM_GROUP_CH) * SSM_STATE
SCAN_COLS = 512

TOK_IN = 2048
NORM_ROWS = 256
T_SCAN = 64
T_SUB = 32
PIECE_ROWS = 128
PERM_BATCH = 8
TOK_OUT = 1024
PROJ_COLS = 256
OUT_COLS = 256
VMEM_LIMIT = 63 * 1024 * 1024


def _silu(z):
    return z * jax.nn.sigmoid(z)


def _pipelined(step, n_blocks, body, even_bufs, odd_bufs):
    @pl.when(step == 0)
    def _():
        body(even_bufs, odd_bufs, True, False)

    @pl.when((step > 0) & (step < n_blocks) & (step % 2 == 0))
    def _():
        body(even_bufs, odd_bufs, True, True)

    @pl.when((step < n_blocks) & (step % 2 == 1))
    def _():
        body(odd_bufs, even_bufs, True, True)

    @pl.when(step == n_blocks)
    def _():
        last_read = odd_bufs if n_blocks % 2 == 0 else even_bufs
        body(None, last_read, False, True)


def _in_proj_kernel(x_ref, g_ref, w_ref, u_ref, zs_ref, q_ref, k_ref, v_ref, za_ref, w_scr):
    @pl.when((pl.program_id(0) == 0) & (pl.program_id(1) == 0))
    def _():
        w_scr[...] = w_ref[...].astype(BF16)

    def emit(rows, tile, acc):
        lo = tile * PROJ_COLS
        if lo < OFF_ZS:
            u_ref[rows, lo - OFF_U:lo - OFF_U + PROJ_COLS] = acc.astype(BF16)
        elif lo < OFF_Q:
            zs_ref[rows, lo - OFF_ZS:lo - OFF_ZS + PROJ_COLS] = _silu(acc).astype(BF16)
        elif lo < OFF_K:
            q_ref[rows, lo - OFF_Q:lo - OFF_Q + PROJ_COLS] = (
                acc * (1.0 / math.sqrt(HEAD_DIM))).astype(BF16)
        elif lo < OFF_ZA:
            k_ref[rows, :] = acc[:, :D_KV].astype(BF16)
            v_ref[rows, :] = acc[:, D_KV:].astype(BF16)
        else:
            za_ref[rows, lo - OFF_ZA:lo - OFF_ZA + PROJ_COLS] = _silu(acc).astype(BF16)

    gain = g_ref[...]
    for piece in range(TOK_IN // NORM_ROWS):
        rows = slice(piece * NORM_ROWS, (piece + 1) * NORM_ROWS)
        x = x_ref[rows, :]
        ms = jnp.mean(x * x, axis=-1, keepdims=True)
        hn = (x * lax.rsqrt(ms + EPS) * gain).astype(BF16)
        for tile in range(D_IN // PROJ_COLS):
            cols = slice(tile * PROJ_COLS, (tile + 1) * PROJ_COLS)
            emit(rows, tile, jnp.dot(hn, w_scr[:, cols], preferred_element_type=F32))


def _in_proj(x, g, w_in):
    bsz, seq, _ = x.shape
    tok = lambda width: pl.BlockSpec((None, TOK_IN, width), lambda b, i: (b, i, 0))
    widths = (D_SSM, D_SSM, D_ATTN, D_KV, D_KV, D_ATTN)
    return pl.pallas_call(
        _in_proj_kernel,
        grid=(bsz, seq // TOK_IN),
        in_specs=[
            tok(D_MODEL),
            pl.BlockSpec((1, D_MODEL), lambda b, i: (0, 0)),
            pl.BlockSpec((D_MODEL, D_IN), lambda b, i: (0, 0), pipeline_mode=pl.Buffered(1)),
        ],
        out_specs=tuple(tok(w) for w in widths),
        out_shape=tuple(jax.ShapeDtypeStruct((bsz, seq, w), BF16) for w in widths),
        scratch_shapes=[pltpu.VMEM((D_MODEL, D_IN), BF16)],
        compiler_params=pltpu.CompilerParams(
            dimension_semantics=("arbitrary", "arbitrary"), vmem_limit_bytes=VMEM_LIMIT),
        name="in_proj",
    )(x, g, w_in)


def _ssm_kernel(u_ref, perm_ref, permt_ref, win_ref, l2_ref, wout_ref, wdir_ref, d_ref,
                wglu_f32, bglu_ref, o_ref,
                wglu_ref, sre_ref, sim_ref, st_ref, y_ref,
                g0_ref, ut0_ref, g1_ref, ut1_ref, *, bsz, n_blocks):
    subs = T_SCAN // T_SUB
    rows = bsz * T_SUB
    half = rows // 2
    step = pl.program_id(0)

    @pl.when(step == 0)
    def _():
        wglu_ref[...] = wglu_f32[...].astype(BF16)
        sre_ref[...] = jnp.zeros_like(sre_ref)
        sim_ref[...] = jnp.zeros_like(sim_ref)

    def body(write, read, front, back):
        g_w, ut_w = write if front else (None, None)
        g_r, ut_r = read if back else (None, None)
        groups = bsz // PERM_BATCH
        grows = PERM_BATCH * T_SUB
        u_new, u_before = [], []
        for sb in range(subs if front else 0):
            steps = slice(sb * T_SUB, (sb + 1) * T_SUB)
            permuted = [
                jnp.dot(perm_ref[...],
                        u_ref[g * PERM_BATCH:(g + 1) * PERM_BATCH, steps].reshape(grows, D_SSM),
                        preferred_element_type=F32).reshape(T_SUB, 1, PERM_BATCH, D_SSM)
                for g in range(groups)]
            u_new.append(jnp.concatenate(permuted, axis=1).reshape(rows, D_SSM).astype(BF16))
            ut_w[sb] = u_new[sb]
            if sb > 0:
                u_last = u_new[sb - 1][rows - bsz:]
            else:
                u_last = ut_r[subs - 1, rows - bsz:, :] if back else jnp.zeros((bsz, D_SSM), BF16)
            u_before.append(jnp.concatenate([u_last, u_new[sb][half:rows - bsz]], axis=0))
        for j in range(N_SLABS):
            ch = slice(j * SLAB_CH, (j + 1) * SLAB_CH)
            drive_in = [jnp.concatenate([u_before[sb][:, ch], u_new[sb][:half, ch]], axis=1)
                        for sb in range(subs if front else 0)]
            for c in range(SLAB_STATE // SCAN_COLS):
                lo = j * SLAB_STATE + c * SCAN_COLS
                cre = slice(c * SCAN_COLS, (c + 1) * SCAN_COLS)
                cim = slice(SLAB_STATE + c * SCAN_COLS, SLAB_STATE + (c + 1) * SCAN_COLS)
                if back:
                    l2r = jnp.broadcast_to(l2_ref[0, :, lo:lo + SCAN_COLS], (bsz, SCAN_COLS))
                    l2i = jnp.broadcast_to(l2_ref[1, :, lo:lo + SCAN_COLS], (bsz, SCAN_COLS))
                    sr = sre_ref[:, lo:lo + SCAN_COLS]
                    si = sim_ref[:, lo:lo + SCAN_COLS]
                for sb, piece in [(sb, piece) for sb in range(subs)
                                  for piece in range(half // PIECE_ROWS)]:
                    rk = slice(piece * PIECE_ROWS, (piece + 1) * PIECE_ROWS)
                    if front:
                        g_w[sb, j, rk, cre] = jnp.dot(drive_in[sb][rk], win_ref[0, j, :, cre],
                                                      preferred_element_type=F32)
                        g_w[sb, j, rk, cim] = jnp.dot(drive_in[sb][rk], win_ref[1, j, :, cre],
                                                      preferred_element_type=F32)
                    if not back:
                        continue
                    for k in range(rk.start // bsz, rk.stop // bsz):
                        r = slice(k * bsz, (k + 1) * bsz)
                        sr, si = (l2r * sr - l2i * si + g_r[sb, j, r, cre],
                                  l2r * si + l2i * sr + g_r[sb, j, r, cim])
                        st_ref[sb, j, r, cre] = sr.astype(BF16)
                        st_ref[sb, j, r, cim] = si.astype(BF16)
                    yk = (jnp.dot(st_ref[sb, j, rk, cre], wout_ref[0, j, cre, :],
                                  preferred_element_type=F32)
                          + jnp.dot(st_ref[sb, j, rk, cim], wout_ref[1, j, cre, :],
                                    preferred_element_type=F32))
                    odd_rows = slice(half + rk.start, half + rk.stop)
                    if c == 0:
                        y_ref[sb, rk, ch] = yk[:, :SLAB_CH]
                        y_ref[sb, odd_rows, ch] = yk[:, SLAB_CH:] + jnp.dot(
                            ut_r[sb, odd_rows, ch], wdir_ref[j], preferred_element_type=F32)
                    else:
                        y_ref[sb, rk, ch] += yk[:, :SLAB_CH]
                        y_ref[sb, odd_rows, ch] += yk[:, SLAB_CH:]
                if back:
                    sre_ref[:, lo:lo + SCAN_COLS] = sr
                    sim_ref[:, lo:lo + SCAN_COLS] = si
        if not back:
            return

        for sb in range(subs):
            y = y_ref[sb] + d_ref[...] * ut_r[sb].astype(F32)
            g = jax.nn.gelu(y)
            gate = jnp.dot(g.astype(BF16), wglu_ref[...], preferred_element_type=F32) + bglu_ref[...]
            out_tb = (g * jax.nn.sigmoid(gate)).reshape(T_SUB, groups, PERM_BATCH, D_SSM)
            for grp in range(groups):
                out_g = out_tb[:, grp].reshape(grows, D_SSM).astype(BF16)
                out_bt = jnp.dot(permt_ref[...], out_g, preferred_element_type=F32).astype(BF16)
                o_ref[grp * PERM_BATCH:(grp + 1) * PERM_BATCH, sb * T_SUB:(sb + 1) * T_SUB] = (
                    out_bt.reshape(PERM_BATCH, T_SUB, D_SSM))

    _pipelined(step, n_blocks, body, (g0_ref, ut0_ref), (g1_ref, ut1_ref))


def _row_permutation():
    rows = PERM_BATCH * T_SUB
    perm = np.zeros((rows, rows), np.float32)
    t, b = np.meshgrid(np.arange(T_SUB), np.arange(PERM_BATCH), indexing="ij")
    dst = (t % 2) * (rows // 2) + (t // 2) * PERM_BATCH + b
    perm[dst.ravel(), (b * T_SUB + t).ravel()] = 1.0
    return perm


def _ssm(u, params, d, wglu, bglu):
    bsz, seq, _ = u.shape
    subs = T_SCAN // T_SUB
    rows = T_SUB * bsz
    half = rows // 2
    nblk = seq // T_SCAN
    assert bsz % PERM_BATCH == 0
    perm = _row_permutation()
    full = lambda shape: pl.BlockSpec(shape, lambda f: (0,) * len(shape))
    g_scratch = pltpu.VMEM((subs, N_SLABS, half, 2 * SLAB_STATE), F32)
    ut_scratch = pltpu.VMEM((subs, rows, D_SSM), BF16)
    return pl.pallas_call(
        functools.partial(_ssm_kernel, bsz=bsz, n_blocks=nblk),
        grid=(nblk + 1,),
        in_specs=[
            pl.BlockSpec((bsz, T_SCAN, D_SSM), lambda f: (0, jnp.minimum(f, nblk - 1), 0)),
            full(perm.shape), full(perm.shape),
            full((2, N_SLABS, 2 * SLAB_CH, SLAB_STATE)), full((2, 1, N_STATE)),
            full((2, N_SLABS, SLAB_STATE, 2 * SLAB_CH)), full((N_SLABS, SLAB_CH, SLAB_CH)),
            full((1, D_SSM)),
            pl.BlockSpec((D_SSM, D_SSM), lambda f: (0, 0), pipeline_mode=pl.Buffered(1)),
            full((1, D_SSM)),
        ],
        out_specs=pl.BlockSpec((bsz, T_SCAN, D_SSM), lambda f: (0, jnp.maximum(f - 1, 0), 0)),
        out_shape=jax.ShapeDtypeStruct((bsz, seq, D_SSM), BF16),
        scratch_shapes=[
            pltpu.VMEM((D_SSM, D_SSM), BF16),
            pltpu.VMEM((bsz, N_STATE), F32),
            pltpu.VMEM((bsz, N_STATE), F32),
            pltpu.VMEM((subs, N_SLABS, half, 2 * SLAB_STATE), BF16),
            pltpu.VMEM((subs, rows, D_SSM), F32),
            g_scratch, ut_scratch, g_scratch, ut_scratch,
        ],
        compiler_params=pltpu.CompilerParams(
            dimension_semantics=("arbitrary",), vmem_limit_bytes=VMEM_LIMIT),
        name="s5_scan",
    )(u, jnp.asarray(perm, BF16), jnp.asarray(perm.T, BF16), *params, d, wglu, bglu)


def _attention_pieces(first, sink_ref, q_ref, k_ref, v_ref, kp_ref, vp_ref, za_ref, bias_ref,
                      half_ref, a_ref):
    low_q = lax.broadcasted_iota(jnp.int32, (WINDOW, LANES), 1) < HEAD_DIM
    kv_cache = {}

    def swap_halves(t):
        packed = pltpu.bitcast(t, jnp.uint32)
        return pltpu.bitcast(pltpu.roll(packed, HEAD_DIM, axis=1), BF16)

    def keys_values(jj, g):
        if jj not in kv_cache:
            rows = slice(jj * WINDOW, (jj + 1) * WINDOW)
            if jj == 0:
                kp, vp = kp_ref[...], vp_ref[...]
            else:
                prev = slice((jj - 1) * WINDOW, jj * WINDOW)
                kp, vp = k_ref[prev, :], v_ref[prev, :]
            kk = jnp.concatenate([kp, k_ref[rows, :]], axis=0)
            vv = jnp.concatenate([vp, v_ref[rows, :]], axis=0)
            kv_cache[jj] = ((kk, swap_halves(kk)), (vv, swap_halves(vv)))
        if (jj, g) not in kv_cache:
            keep_even, keep_odd = half_ref[0], half_ref[1]
            k2, v2 = kv_cache[jj]
            kcat = jnp.concatenate([k2[g] * keep_even, k2[1 - g] * keep_odd], axis=0)
            vcat = jnp.concatenate(
                [jnp.concatenate([v2[g] * keep_even, keep_even], axis=1),
                 jnp.concatenate([v2[1 - g] * keep_odd, keep_odd], axis=1)], axis=0)
            kv_cache[(jj, g)] = (kcat, vcat)
        return kv_cache[(jj, g)]

    items = [(jj, pair) for jj in range(TOK_OUT // WINDOW) for pair in range(N_HEADS // 2)]
    scores, probs = {}, {}

    def where(i):
        jj, pair = items[i]
        return (jj, pair, slice(jj * WINDOW, (jj + 1) * WINDOW),
                slice(pair * LANES, (pair + 1) * LANES))

    def stage_scores(i):
        jj, pair, rows, lanes = where(i)
        kcat, _ = keys_values(jj, pair // (Q_PER_KV // 2))
        s = lax.dot_general(q_ref[rows, lanes], kcat, (((1,), (1,)), ((), ())),
                            preferred_element_type=F32)
        scores[i] = s + (bias_ref[pair + (N_HEADS // 2) * first] if jj == 0 else bias_ref[pair])

    def stage_softmax(i):
        _, pair, _, _ = where(i)
        s = scores.pop(i)
        sink_e, sink_o = sink_ref[2 * pair], sink_ref[2 * pair + 1]
        m_e = jnp.maximum(jnp.max(s[:, :2 * WINDOW], axis=-1, keepdims=True), sink_e)
        m_o = jnp.maximum(jnp.max(s[:, 2 * WINDOW:], axis=-1, keepdims=True), sink_o)
        e = jnp.concatenate([jnp.exp(s[:, :2 * WINDOW] - m_e),
                             jnp.exp(s[:, 2 * WINDOW:] - m_o)], axis=1).astype(BF16)
        probs[i] = (e, jnp.where(low_q, jnp.exp(sink_e - m_e), jnp.exp(sink_o - m_o)))

    def stage_output(i):
        jj, pair, rows, lanes = where(i)
        _, vcat = keys_values(jj, pair // (Q_PER_KV // 2))
        e, sink_term = probs.pop(i)
        o = jnp.dot(e, vcat, preferred_element_type=F32)
        den = o[:, LANES:] + sink_term
        a_ref[rows, lanes] = (o[:, :LANES] / den * za_ref[rows, lanes].astype(F32)).astype(BF16)

    def slot(it):
        if it < len(items):
            stage_scores(it)
        if 0 <= it - 1 < len(items):
            stage_softmax(it - 1)
        if 0 <= it - 2 < len(items):
            stage_output(it - 2)

    return [functools.partial(slot, it) for it in range(len(items) + 2)]


def _attn_out_kernel(sink_ref, x_ref, p_ref, ssm_ref, zs_ref, q_ref, k_ref, v_ref, kp_ref, vp_ref,
                     za_ref, bias_ref, half_ref, wout_f32, pg_ref, wgate_f32, bgate_ref, wproj_f32,
                     o_ref, wout_ref, wgate_ref, wproj_ref, a0_ref, a1_ref,
                     *, blocks_per_seq, n_blocks):
    step = pl.program_id(0)
    seq_block = jnp.minimum(step, n_blocks - 1) % blocks_per_seq
    first = (seq_block == 0).astype(jnp.int32)

    @pl.when(step == 0)
    def _():
        wout_ref[...] = wout_f32[...].astype(BF16)
        wgate_ref[...] = wgate_f32[...].astype(BF16)
        wproj_ref[...] = wproj_f32[...].astype(BF16)

    def body(a_w, a_r, front, back):
        pieces = []
        if front:
            pieces = _attention_pieces(first, sink_ref, q_ref, k_ref, v_ref, kp_ref, vp_ref,
                                       za_ref, bias_ref, half_ref, a_w)
        n_chunks = D_MODEL // OUT_COLS
        per_chunk = max(len(pieces) - 2, 0) // (2 * n_chunks)
        pieces = iter(pieces)

        def run_pieces(count=per_chunk):
            for piece in (list(pieces) if count is None else [p for _, p in zip(range(count), pieces)]):
                piece()

        if not back:
            run_pieces(None)
            return
        run_pieces(2)

        s5 = (ssm_ref[...].astype(F32) * zs_ref[...].astype(F32)).astype(BF16)
        a_prev = a_r[...]
        mixed = []
        for n in range(n_chunks):
            cols = slice(n * OUT_COLS, (n + 1) * OUT_COLS)
            mixed.append(jnp.dot(s5, wout_ref[:D_SSM, cols], preferred_element_type=F32)
                         + jnp.dot(a_prev, wout_ref[D_SSM:, cols], preferred_element_type=F32))
            run_pieces()
        mixed = jnp.concatenate(mixed, axis=1)
        ms = jnp.mean(mixed * mixed, axis=-1, keepdims=True)
        h1 = x_ref[...] + mixed * lax.rsqrt(ms + EPS) * pg_ref[...]
        h1b = h1.astype(BF16)
        pb = p_ref[...].astype(BF16)
        for n in range(n_chunks):
            cols = slice(n * OUT_COLS, (n + 1) * OUT_COLS)
            gate = jax.nn.sigmoid(
                jnp.dot(h1b, wgate_ref[:, cols], preferred_element_type=F32) + bgate_ref[:, cols])
            ple = jnp.dot(pb, wproj_ref[:, cols], preferred_element_type=F32)
            o_ref[:, cols] = h1[:, cols] + gate * ple
            run_pieces()
        run_pieces(None)

    _pipelined(step, n_blocks, body, a0_ref, a1_ref)


def _attn_bias_table():
    q_idx = np.arange(WINDOW)[:, None]
    s_idx = np.arange(2 * WINDOW)[None, :]
    dist = q_idx + WINDOW - s_idx
    valid = (dist >= 0) & (dist < WINDOW)
    slopes = np.exp2(-8.0 * (np.arange(N_HEADS, dtype=np.float32) + 1.0) / N_HEADS).astype(np.float32)
    bias = -slopes[:, None, None] * dist.astype(np.float32)[None]
    normal = np.where(valid[None], bias, -np.inf)
    first = np.where((valid & (s_idx >= WINDOW))[None], bias, -np.inf)
    paired = lambda t: t.reshape(N_HEADS // 2, 2, WINDOW, 2 * WINDOW).transpose(0, 2, 1, 3).reshape(
        N_HEADS // 2, WINDOW, 4 * WINDOW)
    return np.concatenate([paired(normal), paired(first)], axis=0).astype(np.float32)


def _attn_out(sinks, x, p, ssm, zs, q, k, v, za, wout, pg, wgate, bgate, wproj):
    bsz, seq, _ = x.shape
    sub = TOK_OUT // WINDOW
    per_seq = seq // TOK_OUT
    n_blocks = bsz * per_seq

    def cur(f):
        f = jnp.minimum(f, n_blocks - 1)
        return f // per_seq, f % per_seq

    def prv(f):
        f = jnp.maximum(f - 1, 0)
        return f // per_seq, f % per_seq

    front = lambda width: pl.BlockSpec((None, TOK_OUT, width), lambda f: (*cur(f), 0))
    back = lambda width: pl.BlockSpec((None, TOK_OUT, width), lambda f: (*prv(f), 0))
    prev_kv = pl.BlockSpec(
        (None, WINDOW, D_KV),
        lambda f: (cur(f)[0], jnp.maximum(cur(f)[1] * sub - 1, 0), 0))
    full = lambda shape: pl.BlockSpec(shape, lambda f: (0,) * len(shape))
    once = lambda shape: pl.BlockSpec(shape, lambda f: (0,) * len(shape),
                                      pipeline_mode=pl.Buffered(1))
    bias = jnp.asarray(_attn_bias_table())
    lane_is_even = np.broadcast_to(np.arange(LANES) < HEAD_DIM, (2 * WINDOW, LANES))
    half = jnp.asarray(np.stack([lane_is_even, ~lane_is_even]), BF16)
    a_scratch = pltpu.VMEM((TOK_OUT, D_ATTN), BF16)
    return pl.pallas_call(
        functools.partial(_attn_out_kernel, blocks_per_seq=per_seq, n_blocks=n_blocks),
        grid=(n_blocks + 1,),
        in_specs=[
            pl.BlockSpec(memory_space=pltpu.SMEM),
            back(D_MODEL), back(D_PLE), back(D_SSM), back(D_SSM), front(D_ATTN),
            front(D_KV), front(D_KV), prev_kv, prev_kv, front(D_ATTN),
            once(bias.shape), once(half.shape),
            once((D_MODEL, D_MODEL)), full((1, D_MODEL)),
            once((D_MODEL, D_MODEL)), full((1, D_MODEL)), once((D_PLE, D_MODEL)),
        ],
        out_specs=back(D_MODEL),
        out_shape=jax.ShapeDtypeStruct((bsz, seq, D_MODEL), F32),
        scratch_shapes=[pltpu.VMEM((D_MODEL, D_MODEL), BF16), pltpu.VMEM((D_MODEL, D_MODEL), BF16),
                        pltpu.VMEM((D_PLE, D_MODEL), BF16), a_scratch, a_scratch],
        compiler_params=pltpu.CompilerParams(
            dimension_semantics=("arbitrary",), vmem_limit_bytes=VMEM_LIMIT),
        name="attn_out",
    )(sinks, x, p, ssm, zs, q, k, v, k, v, za, bias, half, wout, pg, wgate, bgate, wproj)


def _ssm_params(lam_re, lam_im, log_step, b_re, b_im, c_re, c_im):
    lr, li = lam_re.astype(F32), lam_im.astype(F32)
    step = jnp.exp(log_step.astype(F32))[:, None]
    mag = jnp.exp(lr * step)
    lbr, lbi = mag * jnp.cos(li * step), mag * jnp.sin(li * step)
    den = lr * lr + li * li
    fr = ((lbr - 1.0) * lr + lbi * li) / den
    fi = (lbi * lr - (lbr - 1.0) * li) / den
    br, bi = b_re.astype(F32), b_im.astype(F32)
    bbr = fr[..., None] * br - fi[..., None] * bi
    bbi = fr[..., None] * bi + fi[..., None] * br
    cr, ci = c_re.astype(F32), c_im.astype(F32)
    l2 = jnp.stack([lbr * lbr - lbi * lbi, 2.0 * lbr * lbi])
    lbbr = lbr[..., None] * bbr - lbi[..., None] * bbi
    lbbi = lbr[..., None] * bbi + lbi[..., None] * bbr
    clr = cr * lbr[:, None, :] - ci * lbi[:, None, :]
    cli = cr * lbi[:, None, :] + ci * lbr[:, None, :]
    cb = jnp.sum(cr[:, :, :, None] * bbr[:, None, :, :] - ci[:, :, :, None] * bbi[:, None, :, :],
                 axis=2)
    gps = SLAB_CH // SSM_GROUP_CH
    n, p = SSM_STATE, SSM_GROUP_CH
    same_group = (np.arange(SLAB_CH)[:, None] // p == np.arange(SLAB_STATE)[None, :] // n)
    expand_p = np.tile(np.eye(p, dtype=np.float32), (gps, 1))
    expand_n = np.tile(np.eye(n, dtype=np.float32), (gps, 1))

    w_in = jnp.stack([jnp.stack([lbbr, bbr]), jnp.stack([lbbi, bbi])])
    w_in = w_in.reshape(2, 2, N_SLABS, SLAB_STATE, p)
    w_in = jnp.einsum('rp,absqp->asbrq', expand_p, w_in) * same_group.astype(np.float32)
    w_in = w_in.astype(BF16).reshape(2, N_SLABS, 2 * SLAB_CH, SLAB_STATE)
    w_out = jnp.stack([jnp.stack([cr, clr]), jnp.stack([-ci, -cli])])
    w_out = w_out.reshape(2, 2, N_SLABS, SLAB_CH, n)
    w_out = jnp.einsum('rn,aesqn->aesrq', expand_n, w_out) * same_group.T.astype(np.float32)
    w_out = jnp.concatenate([w_out[:, 0], w_out[:, 1]], axis=-1).astype(BF16)
    same_group_pp = same_group[:, ::n // p]
    w_dir = jnp.einsum('rq,sxq->srx', expand_p, cb.reshape(N_SLABS, SLAB_CH, p))
    w_dir = (w_dir * same_group_pp.astype(np.float32)).astype(BF16)
    return w_in, l2.reshape(2, 1, N_STATE), w_out, w_dir


def _layer(h, p, pre_norm_g, w_in, ssm_lam_re, ssm_lam_im, ssm_log_step, ssm_b_re, ssm_b_im,
           ssm_c_re, ssm_c_im, ssm_d, ssm_w_glu, ssm_b_glu, attn_sinks, w_out, post_norm_g,
           pl_w_proj, pl_w_gate, pl_b_gate):
    row = lambda a: a.astype(F32).reshape(1, -1)
    u, zs, q, k, v, za = _in_proj(h, row(pre_norm_g), w_in.astype(F32))
    params = _ssm_params(
        ssm_lam_re, ssm_lam_im, ssm_log_step, ssm_b_re, ssm_b_im, ssm_c_re, ssm_c_im)
    ssm = _ssm(u, params, row(ssm_d), ssm_w_glu.astype(F32), row(ssm_b_glu))
    return _attn_out(attn_sinks.astype(F32), h, p, ssm, zs, q, k, v, za,
                     w_out.astype(F32), row(post_norm_g), pl_w_gate.astype(F32), row(pl_b_gate),
                     pl_w_proj.astype(F32))


def kernel(x, p, pre_norm_g, w_in, ssm_lam_re, ssm_lam_im, ssm_log_step, ssm_b_re, ssm_b_im,
           ssm_c_re, ssm_c_im, ssm_d, ssm_w_glu, ssm_b_glu, attn_sinks, w_out, post_norm_g,
           pl_w_proj, pl_w_gate, pl_b_gate):
    h = x
    for i in range(p.shape[0]):
        h = _layer(h, p[i], pre_norm_g[i], w_in[i], ssm_lam_re[i], ssm_lam_im[i], ssm_log_step[i],
                   ssm_b_re[i], ssm_b_im[i], ssm_c_re[i], ssm_c_im[i], ssm_d[i], ssm_w_glu[i],
                   ssm_b_glu[i], attn_sinks[i], w_out[i], post_norm_g[i], pl_w_proj[i],
                   pl_w_gate[i], pl_b_gate[i])
    return h
```

```python
import functools
import math

import numpy as np
import jax
import jax.numpy as jnp
from jax import lax
from jax.experimental import pallas as pl
from jax.experimental.pallas import tpu as pltpu

F32 = jnp.float32
BF16 = jnp.bfloat16

D_MODEL = 1024
D_SSM = 512
D_ATTN = 512
SSM_GROUP_CH = 16
SSM_GROUPS = D_SSM // SSM_GROUP_CH
SSM_STATE = 64
N_STATE = SSM_GROUPS * SSM_STATE
HEAD_DIM = 64
N_HEADS = D_ATTN // HEAD_DIM
KV_HEADS = 2
Q_PER_KV = N_HEADS // KV_HEADS
D_KV = KV_HEADS * HEAD_DIM
WINDOW = 128
D_PLE = 256
EPS = 1e-6

LANES = 128
assert D_KV == LANES and 2 * HEAD_DIM == LANES

OFF_U, OFF_ZS, OFF_Q, OFF_K, OFF_V, OFF_ZA, D_IN = 0, 512, 1024, 1536, 1664, 1792, 2304

SLAB_CH = LANES
N_SLABS = D_SSM // SLAB_CH
SLAB_STATE = (SLAB_CH // SSM_GROUP_CH) * SSM_STATE
SCAN_COLS = 512

TOK_IN = 2048
NORM_ROWS = 256
T_SCAN = 64
T_SUB = 32
PIECE_ROWS = 256
PERM_BATCH = 8
TOK_OUT = 1024
PROJ_COLS = 256
OUT_COLS = 256
VMEM_LIMIT = 63 * 1024 * 1024


def _silu(z):
    return z * jax.nn.sigmoid(z)


def _pipelined(step, n_blocks, body, even_bufs, odd_bufs):
    @pl.when(step == 0)
    def _():
        body(even_bufs, odd_bufs, True, False)

    @pl.when((step > 0) & (step < n_blocks) & (step % 2 == 0))
    def _():
        body(even_bufs, odd_bufs, True, True)

    @pl.when((step < n_blocks) & (step % 2 == 1))
    def _():
        body(odd_bufs, even_bufs, True, True)

    @pl.when(step == n_blocks)
    def _():
        last_read = odd_bufs if n_blocks % 2 == 0 else even_bufs
        body(None, last_read, False, True)


def _in_proj_kernel(x_ref, g_ref, w_ref, u_ref, zs_ref, q_ref, k_ref, v_ref, za_ref, w_scr):
    @pl.when((pl.program_id(0) == 0) & (pl.program_id(1) == 0))
    def _():
        w_scr[...] = w_ref[...].astype(BF16)

    def emit(rows, tile, acc):
        lo = tile * PROJ_COLS
        if lo < OFF_ZS:
            u_ref[rows, lo - OFF_U:lo - OFF_U + PROJ_COLS] = acc.astype(BF16)
        elif lo < OFF_Q:
            zs_ref[rows, lo - OFF_ZS:lo - OFF_ZS + PROJ_COLS] = _silu(acc).astype(BF16)
        elif lo < OFF_K:
            q_ref[rows, lo - OFF_Q:lo - OFF_Q + PROJ_COLS] = (
                acc * (1.0 / math.sqrt(HEAD_DIM))).astype(BF16)
        elif lo < OFF_ZA:
            k_ref[rows, :] = acc[:, :D_KV].astype(BF16)
            v_ref[rows, :] = acc[:, D_KV:].astype(BF16)
        else:
            za_ref[rows, lo - OFF_ZA:lo - OFF_ZA + PROJ_COLS] = _silu(acc).astype(BF16)

    gain = g_ref[...]
    for piece in range(TOK_IN // NORM_ROWS):
        rows = slice(piece * NORM_ROWS, (piece + 1) * NORM_ROWS)
        x = x_ref[rows, :]
        ms = jnp.mean(x * x, axis=-1, keepdims=True)
        hn = (x * lax.rsqrt(ms + EPS) * gain).astype(BF16)
        for tile in range(D_IN // PROJ_COLS):
            cols = slice(tile * PROJ_COLS, (tile + 1) * PROJ_COLS)
            emit(rows, tile, jnp.dot(hn, w_scr[:, cols], preferred_element_type=F32))


def _in_proj(x, g, w_in):
    bsz, seq, _ = x.shape
    tok = lambda width: pl.BlockSpec((None, TOK_IN, width), lambda b, i: (b, i, 0))
    widths = (D_SSM, D_SSM, D_ATTN, D_KV, D_KV, D_ATTN)
    return pl.pallas_call(
        _in_proj_kernel,
        grid=(bsz, seq // TOK_IN),
        in_specs=[
            tok(D_MODEL),
            pl.BlockSpec((1, D_MODEL), lambda b, i: (0, 0)),
            pl.BlockSpec((D_MODEL, D_IN), lambda b, i: (0, 0), pipeline_mode=pl.Buffered(1)),
        ],
        out_specs=tuple(tok(w) for w in widths),
        out_shape=tuple(jax.ShapeDtypeStruct((bsz, seq, w), BF16) for w in widths),
        scratch_shapes=[pltpu.VMEM((D_MODEL, D_IN), BF16)],
        compiler_params=pltpu.CompilerParams(
            dimension_semantics=("arbitrary", "arbitrary"), vmem_limit_bytes=VMEM_LIMIT),
        name="in_proj",
    )(x, g, w_in)


def _ssm_kernel(u_ref, perm_ref, permt_ref, win_ref, l2_ref, wout_ref, wdir_ref, d_ref,
                wglu_f32, bglu_ref, o_ref,
                wglu_ref, sre_ref, sim_ref, st_ref, y_ref,
                g0_ref, ut0_ref, g1_ref, ut1_ref, *, bsz, n_blocks):
    subs = T_SCAN // T_SUB
    rows = bsz * T_SUB
    half = rows // 2
    step = pl.program_id(0)

    @pl.when(step == 0)
    def _():
        wglu_ref[...] = wglu_f32[...].astype(BF16)
        sre_ref[...] = jnp.zeros_like(sre_ref)
        sim_ref[...] = jnp.zeros_like(sim_ref)

    def body(write, read, front, back):
        g_w, ut_w = write if front else (None, None)
        g_r, ut_r = read if back else (None, None)
        groups = bsz // PERM_BATCH
        grows = PERM_BATCH * T_SUB
        u_new, u_before = [], []
        for sb in range(subs if front else 0):
            steps = slice(sb * T_SUB, (sb + 1) * T_SUB)
            permuted = [
                jnp.dot(perm_ref[...],
                        u_ref[g * PERM_BATCH:(g + 1) * PERM_BATCH, steps].reshape(grows, D_SSM),
                        preferred_element_type=F32).reshape(T_SUB, 1, PERM_BATCH, D_SSM)
                for g in range(groups)]
            u_new.append(jnp.concatenate(permuted, axis=1).reshape(rows, D_SSM).astype(BF16))
            ut_w[sb] = u_new[sb]
            if sb > 0:
                u_last = u_new[sb - 1][rows - bsz:]
            else:
                u_last = ut_r[subs - 1, rows - bsz:, :] if back else jnp.zeros((bsz, D_SSM), BF16)
            u_before.append(jnp.concatenate([u_last, u_new[sb][half:rows - bsz]], axis=0))
        for j in range(N_SLABS):
            ch = slice(j * SLAB_CH, (j + 1) * SLAB_CH)
            drive_in = [jnp.concatenate([u_before[sb][:, ch], u_new[sb][:half, ch]], axis=1)
                        for sb in range(subs if front else 0)]
            for c in range(SLAB_STATE // SCAN_COLS):
                lo = j * SLAB_STATE + c * SCAN_COLS
                cre = slice(c * SCAN_COLS, (c + 1) * SCAN_COLS)
                cim = slice(SLAB_STATE + c * SCAN_COLS, SLAB_STATE + (c + 1) * SCAN_COLS)
                if back:
                    l2r = jnp.broadcast_to(l2_ref[0, :, lo:lo + SCAN_COLS], (bsz, SCAN_COLS))
                    l2i = jnp.broadcast_to(l2_ref[1, :, lo:lo + SCAN_COLS], (bsz, SCAN_COLS))
                    sr = sre_ref[:, lo:lo + SCAN_COLS]
                    si = sim_ref[:, lo:lo + SCAN_COLS]
                for sb, piece in [(sb, piece) for sb in range(subs)
                                  for piece in range(half // PIECE_ROWS)]:
                    rk = slice(piece * PIECE_ROWS, (piece + 1) * PIECE_ROWS)
                    if front:
                        g_w[sb, j, rk, cre] = jnp.dot(drive_in[sb][rk], win_ref[0, j, :, cre],
                                                      preferred_element_type=F32)
                        g_w[sb, j, rk, cim] = jnp.dot(drive_in[sb][rk], win_ref[1, j, :, cre],
                                                      preferred_element_type=F32)
                    if not back:
                        continue
                    for k in range(rk.start // bsz, rk.stop // bsz):
                        r = slice(k * bsz, (k + 1) * bsz)
                        sr, si = (l2r * sr - l2i * si + g_r[sb, j, r, cre],
                                  l2r * si + l2i * sr + g_r[sb, j, r, cim])
                        st_ref[sb, j, r, cre] = sr.astype(BF16)
                        st_ref[sb, j, r, cim] = si.astype(BF16)
                    yk = (jnp.dot(st_ref[sb, j, rk, cre], wout_ref[0, j, cre, :],
                                  preferred_element_type=F32)
                          + jnp.dot(st_ref[sb, j, rk, cim], wout_ref[1, j, cre, :],
                                    preferred_element_type=F32))
                    odd_rows = slice(half + rk.start, half + rk.stop)
                    if c == 0:
                        y_ref[sb, rk, ch] = yk[:, :SLAB_CH]
                        y_ref[sb, odd_rows, ch] = yk[:, SLAB_CH:] + jnp.dot(
                            ut_r[sb, odd_rows, ch], wdir_ref[j], preferred_element_type=F32)
                    else:
                        y_ref[sb, rk, ch] += yk[:, :SLAB_CH]
                        y_ref[sb, odd_rows, ch] += yk[:, SLAB_CH:]
                if back:
                    sre_ref[:, lo:lo + SCAN_COLS] = sr
                    sim_ref[:, lo:lo + SCAN_COLS] = si
        if not back:
            return

        for sb in range(subs):
            y = y_ref[sb] + d_ref[...] * ut_r[sb].astype(F32)
            g = jax.nn.gelu(y)
            gate = jnp.dot(g.astype(BF16), wglu_ref[...], preferred_element_type=F32) + bglu_ref[...]
            out_tb = (g * jax.nn.sigmoid(gate)).reshape(T_SUB, groups, PERM_BATCH, D_SSM)
            for grp in range(groups):
                out_g = out_tb[:, grp].reshape(grows, D_SSM).astype(BF16)
                out_bt = jnp.dot(permt_ref[...], out_g, preferred_element_type=F32).astype(BF16)
                o_ref[grp * PERM_BATCH:(grp + 1) * PERM_BATCH, sb * T_SUB:(sb + 1) * T_SUB] = (
                    out_bt.reshape(PERM_BATCH, T_SUB, D_SSM))

    _pipelined(step, n_blocks, body, (g0_ref, ut0_ref), (g1_ref, ut1_ref))


def _row_permutation():
    rows = PERM_BATCH * T_SUB
    perm = np.zeros((rows, rows), np.float32)
    t, b = np.meshgrid(np.arange(T_SUB), np.arange(PERM_BATCH), indexing="ij")
    dst = (t % 2) * (rows // 2) + (t // 2) * PERM_BATCH + b
    perm[dst.ravel(), (b * T_SUB + t).ravel()] = 1.0
    return perm


def _ssm(u, params, d, wglu, bglu):
    bsz, seq, _ = u.shape
    subs = T_SCAN // T_SUB
    rows = T_SUB * bsz
    half = rows // 2
    nblk = seq // T_SCAN
    assert bsz % PERM_BATCH == 0
    perm = _row_permutation()
    full = lambda shape: pl.BlockSpec(shape, lambda f: (0,) * len(shape))
    g_scratch = pltpu.VMEM((subs, N_SLABS, half, 2 * SLAB_STATE), F32)
    ut_scratch = pltpu.VMEM((subs, rows, D_SSM), BF16)
    return pl.pallas_call(
        functools.partial(_ssm_kernel, bsz=bsz, n_blocks=nblk),
        grid=(nblk + 1,),
        in_specs=[
            pl.BlockSpec((bsz, T_SCAN, D_SSM), lambda f: (0, jnp.minimum(f, nblk - 1), 0)),
            full(perm.shape), full(perm.shape),
            full((2, N_SLABS, 2 * SLAB_CH, SLAB_STATE)), full((2, 1, N_STATE)),
            full((2, N_SLABS, SLAB_STATE, 2 * SLAB_CH)), full((N_SLABS, SLAB_CH, SLAB_CH)),
            full((1, D_SSM)),
            pl.BlockSpec((D_SSM, D_SSM), lambda f: (0, 0), pipeline_mode=pl.Buffered(1)),
            full((1, D_SSM)),
        ],
        out_specs=pl.BlockSpec((bsz, T_SCAN, D_SSM), lambda f: (0, jnp.maximum(f - 1, 0), 0)),
        out_shape=jax.ShapeDtypeStruct((bsz, seq, D_SSM), BF16),
        scratch_shapes=[
            pltpu.VMEM((D_SSM, D_SSM), BF16),
            pltpu.VMEM((bsz, N_STATE), F32),
            pltpu.VMEM((bsz, N_STATE), F32),
            pltpu.VMEM((subs, N_SLABS, half, 2 * SLAB_STATE), BF16),
            pltpu.VMEM((subs, rows, D_SSM), F32),
            g_scratch, ut_scratch, g_scratch, ut_scratch,
        ],
        compiler_params=pltpu.CompilerParams(
            dimension_semantics=("arbitrary",), vmem_limit_bytes=VMEM_LIMIT),
        name="s5_scan",
    )(u, jnp.asarray(perm, BF16), jnp.asarray(perm.T, BF16), *params, d, wglu, bglu)


def _attention_pieces(first, sink_ref, q_ref, k_ref, v_ref, kp_ref, vp_ref, za_ref, bias_ref,
                      half_ref, a_ref):
    low_q = lax.broadcasted_iota(jnp.int32, (WINDOW, LANES), 1) < HEAD_DIM
    kv_cache = {}

    def swap_halves(t):
        packed = pltpu.bitcast(t, jnp.uint32)
        return pltpu.bitcast(pltpu.roll(packed, HEAD_DIM, axis=1), BF16)

    def keys_values(jj, g):
        if jj not in kv_cache:
            rows = slice(jj * WINDOW, (jj + 1) * WINDOW)
            if jj == 0:
                kp, vp = kp_ref[...], vp_ref[...]
            else:
                prev = slice((jj - 1) * WINDOW, jj * WINDOW)
                kp, vp = k_ref[prev, :], v_ref[prev, :]
            kk = jnp.concatenate([kp, k_ref[rows, :]], axis=0)
            vv = jnp.concatenate([vp, v_ref[rows, :]], axis=0)
            kv_cache[jj] = ((kk, swap_halves(kk)), (vv, swap_halves(vv)))
        if (jj, g) not in kv_cache:
            keep_even, keep_odd = half_ref[0], half_ref[1]
            k2, v2 = kv_cache[jj]
            kcat = jnp.concatenate([k2[g] * keep_even, k2[1 - g] * keep_odd], axis=0)
            vcat = jnp.concatenate(
                [jnp.concatenate([v2[g] * keep_even, keep_even], axis=1),
                 jnp.concatenate([v2[1 - g] * keep_odd, keep_odd], axis=1)], axis=0)
            kv_cache[(jj, g)] = (kcat, vcat)
        return kv_cache[(jj, g)]

    items = [(jj, pair) for jj in range(TOK_OUT // WINDOW) for pair in range(N_HEADS // 2)]
    scores, probs = {}, {}

    def where(i):
        jj, pair = items[i]
        return (jj, pair, slice(jj * WINDOW, (jj + 1) * WINDOW),
                slice(pair * LANES, (pair + 1) * LANES))

    def stage_scores(i):
        jj, pair, rows, lanes = where(i)
        kcat, _ = keys_values(jj, pair // (Q_PER_KV // 2))
        s = lax.dot_general(q_ref[rows, lanes], kcat, (((1,), (1,)), ((), ())),
                            preferred_element_type=F32)
        scores[i] = s + (bias_ref[pair + (N_HEADS // 2) * first] if jj == 0 else bias_ref[pair])

    def stage_softmax(i):
        _, pair, _, _ = where(i)
        s = scores.pop(i)
        sink_e, sink_o = sink_ref[2 * pair], sink_ref[2 * pair + 1]
        m_e = jnp.maximum(jnp.max(s[:, :2 * WINDOW], axis=-1, keepdims=True), sink_e)
        m_o = jnp.maximum(jnp.max(s[:, 2 * WINDOW:], axis=-1, keepdims=True), sink_o)
        e = jnp.concatenate([jnp.exp(s[:, :2 * WINDOW] - m_e),
                             jnp.exp(s[:, 2 * WINDOW:] - m_o)], axis=1).astype(BF16)
        probs[i] = (e, jnp.where(low_q, jnp.exp(sink_e - m_e), jnp.exp(sink_o - m_o)))

    def stage_output(i):
        jj, pair, rows, lanes = where(i)
        _, vcat = keys_values(jj, pair // (Q_PER_KV // 2))
        e, sink_term = probs.pop(i)
        o = jnp.dot(e, vcat, preferred_element_type=F32)
        den = o[:, LANES:] + sink_term
        a_ref[rows, lanes] = (o[:, :LANES] / den * za_ref[rows, lanes].astype(F32)).astype(BF16)

    def slot(it):
        if it < len(items):
            stage_scores(it)
        if 0 <= it - 1 < len(items):
            stage_softmax(it - 1)
        if 0 <= it - 2 < len(items):
            stage_output(it - 2)

    return [functools.partial(slot, it) for it in range(len(items) + 2)]


def _attn_out_kernel(sink_ref, x_ref, p_ref, ssm_ref, zs_ref, q_ref, k_ref, v_ref, kp_ref, vp_ref,
                     za_ref, bias_ref, half_ref, wout_f32, pg_ref, wgate_f32, bgate_ref, wproj_f32,
                     o_ref, wout_ref, wgate_ref, wproj_ref, a0_ref, a1_ref,
                     *, blocks_per_seq, n_blocks):
    step = pl.program_id(0)
    seq_block = jnp.minimum(step, n_blocks - 1) % blocks_per_seq
    first = (seq_block == 0).astype(jnp.int32)

    @pl.when(step == 0)
    def _():
        wout_ref[...] = wout_f32[...].astype(BF16)
        wgate_ref[...] = wgate_f32[...].astype(BF16)
        wproj_ref[...] = wproj_f32[...].astype(BF16)

    def body(a_w, a_r, front, back):
        pieces = []
        if front:
            pieces = _attention_pieces(first, sink_ref, q_ref, k_ref, v_ref, kp_ref, vp_ref,
                                       za_ref, bias_ref, half_ref, a_w)
        n_chunks = D_MODEL // OUT_COLS
        per_chunk = max(len(pieces) - 2, 0) // (2 * n_chunks)
        pieces = iter(pieces)

        def run_pieces(count=per_chunk):
            for piece in (list(pieces) if count is None else [p for _, p in zip(range(count), pieces)]):
                piece()

        if not back:
            run_pieces(None)
            return
        run_pieces(2)

        s5 = (ssm_ref[...].astype(F32) * zs_ref[...].astype(F32)).astype(BF16)
        a_prev = a_r[...]
        mixed = []
        for n in range(n_chunks):
            cols = slice(n * OUT_COLS, (n + 1) * OUT_COLS)
            mixed.append(jnp.dot(s5, wout_ref[:D_SSM, cols], preferred_element_type=F32)
                         + jnp.dot(a_prev, wout_ref[D_SSM:, cols], preferred_element_type=F32))
            run_pieces()
        mixed = jnp.concatenate(mixed, axis=1)
        ms = jnp.mean(mixed * mixed, axis=-1, keepdims=True)
        h1 = x_ref[...] + mixed * lax.rsqrt(ms + EPS) * pg_ref[...]
        h1b = h1.astype(BF16)
        pb = p_ref[...].astype(BF16)
        for n in range(n_chunks):
            cols = slice(n * OUT_COLS, (n + 1) * OUT_COLS)
            gate = jax.nn.sigmoid(
                jnp.dot(h1b, wgate_ref[:, cols], preferred_element_type=F32) + bgate_ref[:, cols])
            ple = jnp.dot(pb, wproj_ref[:, cols], preferred_element_type=F32)
            o_ref[:, cols] = h1[:, cols] + gate * ple
            run_pieces()
        run_pieces(None)

    _pipelined(step, n_blocks, body, a0_ref, a1_ref)


def _attn_bias_table():
    q_idx = np.arange(WINDOW)[:, None]
    s_idx = np.arange(2 * WINDOW)[None, :]
    dist = q_idx + WINDOW - s_idx
    valid = (dist >= 0) & (dist < WINDOW)
    slopes = np.exp2(-8.0 * (np.arange(N_HEADS, dtype=np.float32) + 1.0) / N_HEADS).astype(np.float32)
    bias = -slopes[:, None, None] * dist.astype(np.float32)[None]
    normal = np.where(valid[None], bias, -np.inf)
    first = np.where((valid & (s_idx >= WINDOW))[None], bias, -np.inf)
    paired = lambda t: t.reshape(N_HEADS // 2, 2, WINDOW, 2 * WINDOW).transpose(0, 2, 1, 3).reshape(
        N_HEADS // 2, WINDOW, 4 * WINDOW)
    return np.concatenate([paired(normal), paired(first)], axis=0).astype(np.float32)


def _attn_out(sinks, x, p, ssm, zs, q, k, v, za, wout, pg, wgate, bgate, wproj):
    bsz, seq, _ = x.shape
    sub = TOK_OUT // WINDOW
    per_seq = seq // TOK_OUT
    n_blocks = bsz * per_seq

    def cur(f):
        f = jnp.minimum(f, n_blocks - 1)
        return f // per_seq, f % per_seq

    def prv(f):
        f = jnp.maximum(f - 1, 0)
        return f // per_seq, f % per_seq

    front = lambda width: pl.BlockSpec((None, TOK_OUT, width), lambda f: (*cur(f), 0))
    back = lambda width: pl.BlockSpec((None, TOK_OUT, width), lambda f: (*prv(f), 0))
    prev_kv = pl.BlockSpec(
        (None, WINDOW, D_KV),
        lambda f: (cur(f)[0], jnp.maximum(cur(f)[1] * sub - 1, 0), 0))
    full = lambda shape: pl.BlockSpec(shape, lambda f: (0,) * len(shape))
    once = lambda shape: pl.BlockSpec(shape, lambda f: (0,) * len(shape),
                                      pipeline_mode=pl.Buffered(1))
    bias = jnp.asarray(_attn_bias_table())
    lane_is_even = np.broadcast_to(np.arange(LANES) < HEAD_DIM, (2 * WINDOW, LANES))
    half = jnp.asarray(np.stack([lane_is_even, ~lane_is_even]), BF16)
    a_scratch = pltpu.VMEM((TOK_OUT, D_ATTN), BF16)
    return pl.pallas_call(
        functools.partial(_attn_out_kernel, blocks_per_seq=per_seq, n_blocks=n_blocks),
        grid=(n_blocks + 1,),
        in_specs=[
            pl.BlockSpec(memory_space=pltpu.SMEM),
            back(D_MODEL), back(D_PLE), back(D_SSM), back(D_SSM), front(D_ATTN),
            front(D_KV), front(D_KV), prev_kv, prev_kv, front(D_ATTN),
            once(bias.shape), once(half.shape),
            once((D_MODEL, D_MODEL)), full((1, D_MODEL)),
            once((D_MODEL, D_MODEL)), full((1, D_MODEL)), once((D_PLE, D_MODEL)),
        ],
        out_specs=back(D_MODEL),
        out_shape=jax.ShapeDtypeStruct((bsz, seq, D_MODEL), F32),
        scratch_shapes=[pltpu.VMEM((D_MODEL, D_MODEL), BF16), pltpu.VMEM((D_MODEL, D_MODEL), BF16),
                        pltpu.VMEM((D_PLE, D_MODEL), BF16), a_scratch, a_scratch],
        compiler_params=pltpu.CompilerParams(
            dimension_semantics=("arbitrary",), vmem_limit_bytes=VMEM_LIMIT),
        name="attn_out",
    )(sinks, x, p, ssm, zs, q, k, v, k, v, za, bias, half, wout, pg, wgate, bgate, wproj)


def _ssm_params(lam_re, lam_im, log_step, b_re, b_im, c_re, c_im):
    lr, li = lam_re.astype(F32), lam_im.astype(F32)
    step = jnp.exp(log_step.astype(F32))[:, None]
    mag = jnp.exp(lr * step)
    lbr, lbi = mag * jnp.cos(li * step), mag * jnp.sin(li * step)
    den = lr * lr + li * li
    fr = ((lbr - 1.0) * lr + lbi * li) / den
    fi = (lbi * lr - (lbr - 1.0) * li) / den
    br, bi = b_re.astype(F32), b_im.astype(F32)
    bbr = fr[..., None] * br - fi[..., None] * bi
    bbi = fr[..., None] * bi + fi[..., None] * br
    cr, ci = c_re.astype(F32), c_im.astype(F32)
    l2 = jnp.stack([lbr * lbr - lbi * lbi, 2.0 * lbr * lbi])
    lbbr = lbr[..., None] * bbr - lbi[..., None] * bbi
    lbbi = lbr[..., None] * bbi + lbi[..., None] * bbr
    clr = cr * lbr[:, None, :] - ci * lbi[:, None, :]
    cli = cr * lbi[:, None, :] + ci * lbr[:, None, :]
    cb = jnp.sum(cr[:, :, :, None] * bbr[:, None, :, :] - ci[:, :, :, None] * bbi[:, None, :, :],
                 axis=2)
    gps = SLAB_CH // SSM_GROUP_CH
    n, p = SSM_STATE, SSM_GROUP_CH
    same_group = (np.arange(SLAB_CH)[:, None] // p == np.arange(SLAB_STATE)[None, :] // n)
    expand_p = np.tile(np.eye(p, dtype=np.float32), (gps, 1))
    expand_n = np.tile(np.eye(n, dtype=np.float32), (gps, 1))

    w_in = jnp.stack([jnp.stack([lbbr, bbr]), jnp.stack([lbbi, bbi])])
    w_in = w_in.reshape(2, 2, N_SLABS, SLAB_STATE, p)
    w_in = jnp.einsum('rp,absqp->asbrq', expand_p, w_in) * same_group.astype(np.float32)
    w_in = w_in.astype(BF16).reshape(2, N_SLABS, 2 * SLAB_CH, SLAB_STATE)
    w_out = jnp.stack([jnp.stack([cr, clr]), jnp.stack([-ci, -cli])])
    w_out = w_out.reshape(2, 2, N_SLABS, SLAB_CH, n)
    w_out = jnp.einsum('rn,aesqn->aesrq', expand_n, w_out) * same_group.T.astype(np.float32)
    w_out = jnp.concatenate([w_out[:, 0], w_out[:, 1]], axis=-1).astype(BF16)
    same_group_pp = same_group[:, ::n // p]
    w_dir = jnp.einsum('rq,sxq->srx', expand_p, cb.reshape(N_SLABS, SLAB_CH, p))
    w_dir = (w_dir * same_group_pp.astype(np.float32)).astype(BF16)
    return w_in, l2.reshape(2, 1, N_STATE), w_out, w_dir


def _layer(h, p, pre_norm_g, w_in, ssm_lam_re, ssm_lam_im, ssm_log_step, ssm_b_re, ssm_b_im,
           ssm_c_re, ssm_c_im, ssm_d, ssm_w_glu, ssm_b_glu, attn_sinks, w_out, post_norm_g,
           pl_w_proj, pl_w_gate, pl_b_gate):
    row = lambda a: a.astype(F32).reshape(1, -1)
    u, zs, q, k, v, za = _in_proj(h, row(pre_norm_g), w_in.astype(F32))
    params = _ssm_params(
        ssm_lam_re, ssm_lam_im, ssm_log_step, ssm_b_re, ssm_b_im, ssm_c_re, ssm_c_im)
    ssm = _ssm(u, params, row(ssm_d), ssm_w_glu.astype(F32), row(ssm_b_glu))
    return _attn_out(attn_sinks.astype(F32), h, p, ssm, zs, q, k, v, za,
                     w_out.astype(F32), row(post_norm_g), pl_w_gate.astype(F32), row(pl_b_gate),
                     pl_w_proj.astype(F32))


def kernel(x, p, pre_norm_g, w_in, ssm_lam_re, ssm_lam_im, ssm_log_step, ssm_b_re, ssm_b_im,
           ssm_c_re, ssm_c_im, ssm_d, ssm_w_glu, ssm_b_glu, attn_sinks, w_out, post_norm_g,
           pl_w_proj, pl_w_gate, pl_b_gate):
    h = x
    for i in range(p.shape[0]):
        h = _layer(h, p[i], pre_norm_g[i], w_in[i], ssm_lam_re[i], ssm_lam_im[i], ssm_log_step[i],
                   ssm_b_re[i], ssm_b_im[i], ssm_c_re[i], ssm_c_im[i], ssm_d[i], ssm_w_glu[i],
                   ssm_b_glu[i], attn_sinks[i], w_out[i], post_norm_g[i], pl_w_proj[i],
                   pl_w_gate[i], pl_b_gate[i])
    return h
```
